```python
import math
import jax, jax.numpy as jnp
from jax import lax
import numpy as np

D_MODEL = 2048
BATCH = 16
SEQ = 256
DEPTH = 4
DEC_BATCH = 2
DEC_SEQ = 1024
PAST_LEN = 256

GRID_W = 64
POS_BASE = 10000.0
NORM_EPS = 1e-6
SSD_HEADS = 32
SSD_HEAD_DIM = 64
SSD_INNER = SSD_HEADS * SSD_HEAD_DIM
SSD_GROUPS = 4
SSD_STATE = 128
SSD_CONV = 4
SSD_CHUNK = 128
SSD_XBC = SSD_INNER + 2 * SSD_GROUPS * SSD_STATE
SC_WIDTH = 1024
SC_CONV = 3
FT_WIDTH = 1024
FT_GROUPS = 4
FT_GROUP_DIM = FT_WIDTH // FT_GROUPS
D_FF = 5504
N_BRANCH = 3
N_MOD = 9
IN_SIZES = (SSD_INNER, SSD_XBC, 2 * SSD_HEADS, SC_WIDTH, SC_WIDTH, SC_WIDTH, FT_WIDTH, N_BRANCH * D_MODEL)
IN_COLS = SSD_INNER + SSD_XBC + 2 * SSD_HEADS + 3 * SC_WIDTH + FT_WIDTH + N_BRANCH * D_MODEL

kernel_name = "hybrid_ssd_shortconv_fnet_diffusion_step"


def rms_norm(x, g):
    xf = x.astype(jnp.float32)
    y = xf * lax.rsqrt(jnp.mean(xf * xf, axis=-1, keepdims=True) + NORM_EPS)
    return (y * g.astype(jnp.float32)).astype(x.dtype)


def modulate(x, shift, scale):
    return x * (1.0 + scale[:, None, :]) + shift[:, None, :]


def dwconv(x, w):
    K = w.shape[0]
    L = x.shape[1]
    left = (K - 1) // 2
    xp = jnp.pad(x, ((0, 0), (left, K - 1 - left), (0, 0)))
    out = xp[:, 0:L] * w[0]
    for k in range(1, K):
        out = out + xp[:, k:k + L] * w[k]
    return out


def swiglu(x, w_gu, w_d):
    g, u = jnp.split(x @ w_gu, 2, axis=-1)
    return (jax.nn.silu(g) * u) @ w_d


def segsum(a):
    T = a.shape[-1]
    cs = jnp.cumsum(a, axis=-1)
    diff = cs[..., :, None] - cs[..., None, :]
    mask = jnp.tril(jnp.ones((T, T), dtype=bool))
    return jnp.where(mask, diff, -jnp.inf)


def ssd_scan(x, dt, a_neg, b, c, h0):
    bt, L, H, P = x.shape
    Q = SSD_CHUNK
    nc = L // Q
    G = SSD_GROUPS
    hg = H // G
    xd = (x * dt[..., None]).reshape(bt, nc, Q, G, hg, P)
    a = jnp.moveaxis((dt * a_neg).reshape(bt, nc, Q, G, hg), 2, -1)
    a_cs = jnp.cumsum(a, axis=-1)
    bc = b.reshape(bt, nc, Q, G, SSD_STATE)
    cc = c.reshape(bt, nc, Q, G, SSD_STATE)
    decay_in = jnp.exp(segsum(a))
    y_diag = jnp.einsum("bclgn,bcsgn,bcgjls,bcsgjp->bclgjp", cc, bc, decay_in, xd)
    decay_to_end = jnp.exp(a_cs[..., -1:] - a_cs)
    chunk_states = jnp.einsum("bcsgn,bcgjs,bcsgjp->bcgjpn", bc, decay_to_end, xd)
    chunk_decay = jnp.exp(a_cs[..., -1])

    def step(h, inp):
        s, d = inp
        return h * d[..., None, None] + s, h

    h0g = h0.reshape(bt, G, hg, P, SSD_STATE)
    h_fin, h_prev = lax.scan(step, h0g, (jnp.moveaxis(chunk_states, 1, 0), jnp.moveaxis(chunk_decay, 1, 0)))
    h_prev = jnp.moveaxis(h_prev, 0, 1)
    y_off = jnp.einsum("bclgn,bcgjpn,bcgjl->bclgjp", cc, h_prev, jnp.exp(a_cs))
    y = (y_diag + y_off).reshape(bt, L, H, P)
    return y, h_fin.reshape(bt, H, P, SSD_STATE)


def ssd_mixer(z, xbc_raw, dt_raw, conv_w, conv_b, dt_bias, a_log, d_skip, norm_g, h0):
    bt, L, _ = z.shape
    f32 = jnp.float32
    xbc = jax.nn.silu(dwconv(xbc_raw, conv_w) + conv_b).astype(f32)
    xs = xbc[..., :SSD_INNER].reshape(bt, L, SSD_HEADS, SSD_HEAD_DIM)
    bm = xbc[..., SSD_INNER:SSD_INNER + SSD_GROUPS * SSD_STATE].reshape(bt, L, SSD_GROUPS, SSD_STATE)
    cm = xbc[..., SSD_INNER + SSD_GROUPS * SSD_STATE:].reshape(bt, L, SSD_GROUPS, SSD_STATE)
    dt = jax.nn.softplus(dt_raw.astype(f32).reshape(bt, L, 2, SSD_HEADS) + dt_bias.astype(f32))
    a_neg = -jnp.exp(a_log.astype(f32))
    h0 = h0.astype(f32)
    y_f, h_f = ssd_scan(xs, dt[:, :, 0], a_neg[0], bm, cm, h0[:, 0])
    flip = lambda t: jnp.flip(t, axis=1)
    y_b, h_b = ssd_scan(flip(xs), flip(dt[:, :, 1]), a_neg[1], flip(bm), flip(cm), h0[:, 1])
    y = y_f + flip(y_b) + xs * d_skip.astype(f32)[:, None]
    y = y.reshape(bt, L, SSD_INNER) * jax.nn.silu(z.astype(f32))
    y = rms_norm(y, norm_g)
    return y.astype(z.dtype), jnp.stack([h_f, h_b], axis=1)


def fourier_mixer(u):
    bt, L, _ = u.shape
    uf = u.astype(jnp.float32).reshape(bt, L, FT_GROUPS, FT_GROUP_DIM)
    y = jnp.fft.fft2(uf, axes=(1, 3), norm="ortho").real
    return y.reshape(bt, L, FT_WIDTH).astype(u.dtype)


def grid_pos_emb(n_tok):
    rows = n_tok // GRID_W
    t = jnp.arange(rows * GRID_W)
    r = (t // GRID_W).astype(jnp.float32)[:, None]
    col = (t % GRID_W).astype(jnp.float32)[:, None]
    nf = D_MODEL // 4
    omega = 1.0 / (POS_BASE ** (jnp.arange(nf, dtype=jnp.float32) / nf))
    return jnp.concatenate([jnp.sin(r * omega), jnp.cos(r * omega), jnp.sin(col * omega), jnp.cos(col * omega)], axis=-1)


def trunk_layer(x, mod, h0, lp):
    bt, L, D = x.shape
    sh1, sc1, g1, sh2, sc2, g2, sh3, sc3, g3 = jnp.split(mod.astype(x.dtype), N_MOD, axis=-1)
    ng = lp["norm_g"]
    u = modulate(rms_norm(x, ng[0]), sh1, sc1)
    x = x + 0.5 * g1[:, None, :] * rms_norm(swiglu(u, lp["ffn1_wgu"], lp["ffn1_wd"]), ng[1])
    u = modulate(rms_norm(x, ng[2]), sh2, sc2)
    proj = u @ lp["w_in"]
    offs = [int(o) for o in np.cumsum(IN_SIZES)[:-1]]
    z, xbc, dt_raw, sc_b, sc_c, sc_x, ft_in, gate_raw = jnp.split(proj, offs, axis=-1)
    y_ssd, h_fin = ssd_mixer(z, xbc, dt_raw, lp["ssd_conv_w"], lp["ssd_conv_b"], lp["ssd_dt_bias"],
                             lp["ssd_a_log"], lp["ssd_d"], lp["ssd_norm_g"], h0)
    y_sc = sc_b * dwconv(sc_c * sc_x, lp["sc_conv_w"])
    y_ft = fourier_mixer(ft_in)
    gates = jax.nn.sigmoid(gate_raw.astype(jnp.float32)).astype(x.dtype).reshape(bt, L, N_BRANCH, D)
    merged = (gates[:, :, 0] * (y_ssd @ lp["w_br_ssd"])
              + gates[:, :, 1] * (y_sc @ lp["w_br_sc"])
              + gates[:, :, 2] * (y_ft @ lp["w_br_ft"]))
    x = x + g2[:, None, :] * rms_norm(merged @ lp["w_out"], ng[3])
    u = modulate(rms_norm(x, ng[4]), sh3, sc3)
    x = x + 0.5 * g3[:, None, :] * rms_norm(swiglu(u, lp["ffn2_wgu"], lp["ffn2_wd"]), ng[5])
    return x, h_fin


def setup_inputs(seed: int = 0) -> dict:
    key = jax.random.key(seed)
    ks = iter(jax.random.split(key, 32))
    f32 = jnp.float32
    D = D_MODEL

    def nrm(shape, scale):
        return jax.random.normal(next(ks), shape, f32) * scale

    x_prompt = nrm((BATCH, SEQ, D), 1.0)
    x_sample = nrm((DEC_BATCH, DEC_SEQ, D), 1.0)
    state_ssd = nrm((DEC_BATCH, DEPTH, 2, SSD_HEADS, SSD_HEAD_DIM, SSD_STATE), 0.1)
    c = nrm((DEC_BATCH, D), 1.0)
    c_ctx = nrm((D,), 1.0)
    ada_w = nrm((DEPTH, D, N_MOD * D), 0.5 * D ** -0.5)
    ada_b = nrm((DEPTH, N_MOD * D), 0.02)
    norm_g = 1.0 + nrm((DEPTH, 6, D), 0.02)
    ffn1_wgu = nrm((DEPTH, D, 2 * D_FF), D ** -0.5)
    ffn1_wd = nrm((DEPTH, D_FF, D), D_FF ** -0.5)
    w_in = nrm((DEPTH, D, IN_COLS), D ** -0.5)
    ssd_conv_w = nrm((DEPTH, SSD_CONV, SSD_XBC), SSD_CONV ** -0.5)
    ssd_conv_b = nrm((DEPTH, SSD_XBC), 0.02)
    dt0 = jnp.exp(jax.random.uniform(next(ks), (DEPTH, 2, SSD_HEADS), f32, math.log(1e-3), math.log(1e-1)))
    ssd_dt_bias = dt0 + jnp.log(-jnp.expm1(-dt0))
    ssd_a_log = jnp.log(jax.random.uniform(next(ks), (DEPTH, 2, SSD_HEADS), f32, 1.0, 16.0))
    ssd_d = 1.0 + nrm((DEPTH, SSD_HEADS), 0.1)
    ssd_norm_g = 1.0 + nrm((DEPTH, SSD_INNER), 0.02)
    sc_conv_w = nrm((DEPTH, SC_CONV, SC_WIDTH), SC_CONV ** -0.5)
    w_br_ssd = nrm((DEPTH, SSD_INNER, D), SSD_INNER ** -0.5)
    w_br_sc = nrm((DEPTH, SC_WIDTH, D), SC_WIDTH ** -0.5)
    w_br_ft = nrm((DEPTH, FT_WIDTH, D), FT_WIDTH ** -0.5)
    w_out = nrm((DEPTH, D, D), D ** -0.5)
    ffn2_wgu = nrm((DEPTH, D, 2 * D_FF), D ** -0.5)
    ffn2_wd = nrm((DEPTH, D_FF, D), D_FF ** -0.5)
    return {
        "x_prompt": x_prompt, "x_sample": x_sample, "state_ssd": state_ssd, "c": c, "c_ctx": c_ctx,
        "ada_w": ada_w, "ada_b": ada_b, "norm_g": norm_g, "ffn1_wgu": ffn1_wgu, "ffn1_wd": ffn1_wd,
        "w_in": w_in, "ssd_conv_w": ssd_conv_w, "ssd_conv_b": ssd_conv_b, "ssd_dt_bias": ssd_dt_bias,
        "ssd_a_log": ssd_a_log, "ssd_d": ssd_d, "ssd_norm_g": ssd_norm_g, "sc_conv_w": sc_conv_w,
        "w_br_ssd": w_br_ssd, "w_br_sc": w_br_sc, "w_br_ft": w_br_ft, "w_out": w_out,
        "ffn2_wgu": ffn2_wgu, "ffn2_wd": ffn2_wd,
    }


def reference(x_prompt, x_sample, state_ssd, c, c_ctx, ada_w, ada_b, norm_g, ffn1_wgu, ffn1_wd, w_in,
              ssd_conv_w, ssd_conv_b, ssd_dt_bias, ssd_a_log, ssd_d, ssd_norm_g, sc_conv_w,
              w_br_ssd, w_br_sc, w_br_ft, w_out, ffn2_wgu, ffn2_wd):
    n_ctx = x_prompt.shape[0]
    h_zero = jnp.zeros((n_ctx, 2, SSD_HEADS, SSD_HEAD_DIM, SSD_STATE), jnp.float32)
    x_ctx = x_prompt
    x_lat = x_sample + grid_pos_emb(x_sample.shape[1]).astype(x_sample.dtype)[None]
    ctx_states = []
    for l in range(DEPTH):
        lp = {
            "norm_g": norm_g[l], "ffn1_wgu": ffn1_wgu[l], "ffn1_wd": ffn1_wd[l], "w_in": w_in[l],
            "ssd_conv_w": ssd_conv_w[l], "ssd_conv_b": ssd_conv_b[l], "ssd_dt_bias": ssd_dt_bias[l],
            "ssd_a_log": ssd_a_log[l], "ssd_d": ssd_d[l], "ssd_norm_g": ssd_norm_g[l],
            "sc_conv_w": sc_conv_w[l], "w_br_ssd": w_br_ssd[l], "w_br_sc": w_br_sc[l],
            "w_br_ft": w_br_ft[l], "w_out": w_out[l], "ffn2_wgu": ffn2_wgu[l], "ffn2_wd": ffn2_wd[l],
        }
        mod_ctx = (jax.nn.silu(c_ctx) @ ada_w[l] + ada_b[l])[None, :]
        mod_lat = jax.nn.silu(c) @ ada_w[l] + ada_b[l]
        x_ctx, h_ctx = trunk_layer(x_ctx, mod_ctx, h_zero, lp)
        ctx_states.append(h_ctx)
        x_lat, _ = trunk_layer(x_lat, mod_lat, state_ssd[:, l], lp)
    new_state_ssd = jnp.stack(ctx_states, axis=1)
    return (x_ctx, x_lat, new_state_ssd)
```

```python
import functools
import math

import numpy as np
import jax
import jax.numpy as jnp
from jax import lax
from jax.experimental import pallas as pl
from jax.experimental.pallas import tpu as pltpu

F32 = jnp.float32
BF16 = jnp.bfloat16

D_MODEL = 2048
DEPTH = 4
GRID_W = 64
POS_BASE = 10000.0
NORM_EPS = 1e-6
SSD_HEADS = 32
SSD_HEAD_DIM = 64
SSD_INNER = SSD_HEADS * SSD_HEAD_DIM
SSD_GROUPS = 4
SSD_STATE = 128
SSD_CONV = 4
SSD_CHUNK = 128
SSD_XBC = SSD_INNER + 2 * SSD_GROUPS * SSD_STATE
SC_WIDTH = 1024
SC_CONV = 3
FT_WIDTH = 1024
FT_GROUPS = 4
FT_GROUP_DIM = FT_WIDTH // FT_GROUPS
D_FF = 5504
N_BRANCH = 3
N_MOD = 9
HEADS_PER_GROUP = SSD_HEADS // SSD_GROUPS
GROUP_WIDTH = HEADS_PER_GROUP * SSD_HEAD_DIM
HEAD_COLS = SSD_INNER + SSD_XBC
DT_COL = HEAD_COLS
TAIL_COL = HEAD_COLS + 2 * SSD_HEADS
TAIL_COLS = 3 * SC_WIDTH + FT_WIDTH + N_BRANCH * D_MODEL

LANES = 128
VMEM_LIMIT_BYTES = 56 * 1024 * 1024


def _cparams(n_axes):
    return pltpu.CompilerParams(dimension_semantics=("arbitrary",) * n_axes,
                                vmem_limit_bytes=VMEM_LIMIT_BYTES)


def _silu(x):
    return x * jax.nn.sigmoid(x)


def _rms(x, g):
    ms = jnp.mean(x * x, axis=-1, keepdims=True)
    return x * lax.rsqrt(ms + NORM_EPS) * g


def _norm_mod(x, g, shift, scale):
    return _rms(x, g) * (1.0 + scale) + shift


def _split_bf16(x, n):
    parts = []
    r = x
    for i in range(n):
        p = r.astype(BF16)
        parts.append(p)
        if i + 1 < n:
            r = r - p.astype(F32)
    return parts


def _dot_r01(x, m01, n=3):
    acc = None
    for p in _split_bf16(x, n):
        t = jnp.dot(p, m01, preferred_element_type=F32)
        acc = t if acc is None else acc + t
    return acc


def _dot_l01(m01, x, n=3):
    acc = None
    for p in _split_bf16(x, n):
        t = jnp.dot(m01, p, preferred_element_type=F32)
        acc = t if acc is None else acc + t
    return acc


def _mod_kernel(c_ref, w_ref, b_ref, o_ref):
    s = _silu(c_ref[...]).astype(BF16)
    o_ref[...] = jnp.dot(s, w_ref[...].astype(BF16), preferred_element_type=F32) + b_ref[...]


def _modulation(c8, ada_w, ada_b, tn=1024):
    depth, d, n = ada_w.shape
    return pl.pallas_call(
        _mod_kernel,
        grid=(depth, n // tn),
        in_specs=[
            pl.BlockSpec((8, d), lambda l, j: (0, 0)),
            pl.BlockSpec((None, d, tn), lambda l, j: (l, 0, j)),
            pl.BlockSpec((None, 1, tn), lambda l, j: (l, 0, j)),
        ],
        out_specs=pl.BlockSpec((None, 8, tn), lambda l, j: (l, 0, j)),
        out_shape=jax.ShapeDtypeStruct((depth, 8, n), F32),
        compiler_params=_cparams(2),
        name="adaln_mod",
    )(c8, ada_w, ada_b.reshape(depth, 1, n))


class _Rows:
    def __init__(self, n_ctx_tok, n_lat, lat_len):
        self.n_ctx_tok = n_ctx_tok
        self.n_lat = n_lat
        self.lat_len = lat_len
        self.n_tok = n_ctx_tok + n_lat * lat_len

    def group_of_tile(self, i, tm):
        assert self.n_ctx_tok % tm == 0 and self.lat_len % tm == 0
        n_ctx_tiles = self.n_ctx_tok // tm
        per_lat = self.lat_len // tm
        return jnp.where(i < n_ctx_tiles, 0, 1 + (i - n_ctx_tiles) // per_lat)


def _single(block_shape, index_map):
    return pl.BlockSpec(block_shape, index_map, pipeline_mode=pl.Buffered(1))


FFN_SUB = 128
FFN_NSUB = D_FF // FFN_SUB


def _ffn_kernel(x_ref, mod_ref, ng_ref, wg0, wg1, wu0, wu1, wd0, wd1, o_ref,
                u_sc, wup_sc, wdn_sc, *, nk, mod_row, ng_row):
    k = pl.program_id(1)

    @pl.when(k == 0)
    def _():
        u = _norm_mod(x_ref[...], ng_ref[ng_row:ng_row + 1, :],
                      mod_ref[mod_row:mod_row + 1, :], mod_ref[mod_row + 1:mod_row + 2, :])
        u_sc[...] = u.astype(BF16)
        o_ref[...] = jnp.zeros_like(o_ref)

    s = FFN_SUB
    wup_sc[:, 0 * s:1 * s] = wg0[...].astype(BF16)
    wup_sc[:, 1 * s:2 * s] = wg1[...].astype(BF16)
    wup_sc[:, 2 * s:3 * s] = wu0[...].astype(BF16)
    wup_sc[:, 3 * s:4 * s] = wu1[...].astype(BF16)
    wdn_sc[0 * s:1 * s, :] = wd0[...].astype(BF16)
    wdn_sc[1 * s:2 * s, :] = wd1[...].astype(BF16)

    h = jnp.dot(u_sc[...], wup_sc[...], preferred_element_type=F32)
    a = _silu(h[:, :2 * s]) * h[:, 2 * s:]
    second_valid = (2 * k + 1) < FFN_NSUB
    lane = lax.broadcasted_iota(jnp.int32, a.shape, 1)
    a = jnp.where(jnp.logical_or(lane < s, second_valid), a, 0.0)
    o_ref[...] += jnp.dot(a.astype(BF16), wdn_sc[...], preferred_element_type=F32)

    @pl.when(k == nk - 1)
    def _():
        y = _rms(o_ref[...], ng_ref[ng_row + 1:ng_row + 2, :])
        o_ref[...] = x_ref[...] + 0.5 * mod_ref[mod_row + 2:mod_row + 3, :] * y


def _ffn(x, mod4, norm_g, wgu, wd, l, which, rows, tm=1024):
    n_tok, d = x.shape
    nk = pl.cdiv(FFN_NSUB, 2)
    last = FFN_NSUB - 1
    mod_row = 0 if which == 0 else 6
    ng_row = 0 if which == 0 else 4
    grp = lambda i: rows.group_of_tile(i, tm)
    kern = functools.partial(_ffn_kernel, nk=nk, mod_row=mod_row, ng_row=ng_row)
    return pl.pallas_call(
        kern,
        grid=(n_tok // tm, nk),
        in_specs=[
            _single((tm, d), lambda i, k: (i, 0)),
            pl.BlockSpec((None, None, N_MOD, d), lambda i, k: (l, grp(i), 0, 0)),
            pl.BlockSpec((None, 6, d), lambda i, k: (l, 0, 0)),
            pl.BlockSpec((None, d, FFN_SUB), lambda i, k: (l, 0, 2 * k)),
            pl.BlockSpec((None, d, FFN_SUB), lambda i, k: (l, 0, jnp.minimum(2 * k + 1, last))),
            pl.BlockSpec((None, d, FFN_SUB), lambda i, k: (l, 0, FFN_NSUB + 2 * k)),
            pl.BlockSpec((None, d, FFN_SUB), lambda i, k: (l, 0, FFN_NSUB + jnp.minimum(2 * k + 1, last))),
            pl.BlockSpec((None, FFN_SUB, d), lambda i, k: (l, 2 * k, 0)),
            pl.BlockSpec((None, FFN_SUB, d), lambda i, k: (l, jnp.minimum(2 * k + 1, last), 0)),
        ],
        out_specs=_single((tm, d), lambda i, k: (i, 0)),
        out_shape=jax.ShapeDtypeStruct((n_tok, d), F32),
        scratch_shapes=[
            pltpu.VMEM((tm, d), BF16),
            pltpu.VMEM((d, 4 * FFN_SUB), BF16),
            pltpu.VMEM((2 * FFN_SUB, d), BF16),
        ],
        compiler_params=_cparams(2),
        name="ffn",
    )(x, mod4, norm_g, wgu, wgu, wgu, wgu, wd, wd)


def _inproj_kernel(x_ref, mod_ref, ng_ref, wh_ref, wdt_ref, wt_ref, oh_ref, odt_ref, ot_ref, u_sc,
                   *, n_head):
    j = pl.program_id(1)

    @pl.when(j == 0)
    def _():
        u = _norm_mod(x_ref[...], ng_ref[2:3, :], mod_ref[3:4, :], mod_ref[4:5, :])
        u_sc[...] = u.astype(BF16)

    @pl.when(j < n_head)
    def _():
        oh_ref[...] = jnp.dot(u_sc[...], wh_ref[...].astype(BF16), preferred_element_type=F32)

    @pl.when(j == n_head)
    def _():
        odt_ref[...] = jnp.dot(u_sc[...], wdt_ref[...].astype(BF16), preferred_element_type=F32)

    @pl.when(j > n_head)
    def _():
        ot_ref[...] = jnp.dot(u_sc[...], wt_ref[...].astype(BF16), preferred_element_type=F32)


def _inproj(x, mod4, norm_g, w_in, w_tail, l, rows, tm=1024, tn=512):
    n_tok, d = x.shape
    n_head = HEAD_COLS // tn
    n_tail = TAIL_COLS // tn
    grp = lambda i: rows.group_of_tile(i, tm)
    head_j = lambda j: jnp.minimum(j, n_head - 1)
    tail_j = lambda j: jnp.clip(j - n_head - 1, 0, n_tail - 1)
    kern = functools.partial(_inproj_kernel, n_head=n_head)
    return pl.pallas_call(
        kern,
        grid=(n_tok // tm, n_head + 1 + n_tail),
        in_specs=[
            _single((tm, d), lambda i, j: (i, 0)),
            pl.BlockSpec((None, None, N_MOD, d), lambda i, j: (l, grp(i), 0, 0)),
            pl.BlockSpec((None, 6, d), lambda i, j: (l, 0, 0)),
            pl.BlockSpec((None, d, tn), lambda i, j: (l, 0, head_j(j))),
            pl.BlockSpec((None, d, LANES), lambda i, j: (l, 0, DT_COL // LANES)),
            pl.BlockSpec((None, d, tn), lambda i, j: (l, 0, tail_j(j))),
        ],
        out_specs=[
            pl.BlockSpec((tm, tn), lambda i, j: (i, head_j(j))),
            pl.BlockSpec((tm, LANES), lambda i, j: (i, 0)),
            pl.BlockSpec((tm, tn), lambda i, j: (i, tail_j(j))),
        ],
        out_shape=[
            jax.ShapeDtypeStruct((n_tok, HEAD_COLS), F32),
            jax.ShapeDtypeStruct((n_tok, LANES), F32),
            jax.ShapeDtypeStruct((n_tok, TAIL_COLS), F32),
        ],
        scratch_shapes=[pltpu.VMEM((tm, d), BF16)],
        compiler_params=_cparams(2),
        name="inproj",
    )(x, mod4, norm_g, w_in, w_in, w_tail)


def _dwconv_rows(x, w, left):
    n_rows = x.shape[0]
    row = lax.broadcasted_iota(jnp.int32, x.shape, 0)
    out = None
    for k in range(w.shape[0]):
        off = k - left
        if off == 0:
            term = x
        else:
            shifted = pltpu.roll(x, (-off) % n_rows, axis=0)
            valid = jnp.logical_and(row + off >= 0, row + off < n_rows)
            term = jnp.where(valid, shifted, 0.0)
        term = term * w[k:k + 1, :]
        out = term if out is None else out + term
    return out


def _ssd_kernel(*refs, seq_len, has_h0, emit_state):
    (z_ref, xr_ref, br_ref, cr_ref, dt_ref, cwx_ref, cwb_ref, cwc_ref, cbx_ref, cbb_ref, cbc_ref,
     dtb_ref, alog_ref, dsk_ref, sel_ref, e2_ref, tri_ref, trit_ref) = refs[:18]
    pos = 18
    h0_ref = None
    if has_h0:
        h0_ref = refs[pos]
        pos += 1
    pos += 2 if has_h0 else 0
    y_ref, ssq_ref = refs[pos], refs[pos + 1]
    pos += 2
    hfin_ref = None
    if emit_state:
        hfin_ref = refs[pos]
        pos += 1
    xa_sc, ba_sc, ca_sc, dts_sc, a_sc, ht_sc = refs[pos:pos + 6]

    q = SSD_CHUNK
    nc = seq_len // q
    left = (SSD_CONV - 1) // 2
    hpg = HEADS_PER_GROUP
    gw = GROUP_WIDTH

    xa_sc[...] = _silu(_dwconv_rows(xr_ref[...], cwx_ref[...], left) + cbx_ref[...])
    ba_sc[...] = _silu(_dwconv_rows(br_ref[...], cwb_ref[...], left) + cbb_ref[...])
    ca_sc[...] = _silu(_dwconv_rows(cr_ref[...], cwc_ref[...], left) + cbc_ref[...])

    draw = dt_ref[...] + dtb_ref[...]
    dt_all = jnp.maximum(draw, 0.0) + jnp.log1p(jnp.exp(-jnp.abs(draw)))
    dts = _dot_r01(dt_all, sel_ref[...])
    dts_sc[...] = dts
    a_sc[...] = dts * (-jnp.exp(alog_ref[...]))

    for d in range(2):
        if has_h0:
            for p in range(hpg // 2):
                blk = h0_ref[d, 2 * p:2 * p + 2].reshape(2 * SSD_HEAD_DIM, SSD_STATE)
                ht_sc[d, :, p * LANES:(p + 1) * LANES] = blk.T
        else:
            ht_sc[d] = jnp.zeros((SSD_STATE, gw), F32)

    ri = lax.broadcasted_iota(jnp.int32, (q, q), 0)
    ci = lax.broadcasted_iota(jnp.int32, (q, q), 1)
    lane_lo = lax.broadcasted_iota(jnp.int32, (q, LANES), 1) < SSD_HEAD_DIM
    neg_inf = jnp.float32(-jnp.inf)

    def chunk_step(c, d):
        start = pl.multiple_of(c * q, q)
        rows = pl.ds(start, q)
        a_c = a_sc[rows, :]
        if d == 0:
            cs = _dot_l01(tri_ref[...], a_c)
            tot = cs[q - 1:q, :]
            mask = ri >= ci
        else:
            cs = _dot_l01(trit_ref[...], a_c)
            tot = cs[0:1, :]
            mask = ri <= ci
        cst = cs.T
        e2d = e2_ref[:, d * gw:(d + 1) * gw]
        dtx = _dot_r01(dts_sc[rows, :], e2d)
        ecs_x = _dot_r01(jnp.exp(cs), e2d)
        dte_x = _dot_r01(jnp.exp(tot - cs), e2d)
        ea_x = _dot_r01(jnp.broadcast_to(jnp.exp(tot), (8, LANES)), e2d)[0:1, :]
        xa_c = xa_sc[rows, :]
        b_c = ba_sc[rows, :]
        c_bf = ca_sc[rows, :].astype(BF16)
        xdt = xa_c * dtx
        g = lax.dot_general(c_bf, b_c.astype(BF16), (((1,), (1,)), ((), ())),
                            preferred_element_type=F32)
        ht = ht_sc[d]
        y_off = jnp.dot(c_bf, ht.astype(BF16), preferred_element_type=F32) * ecs_x
        pieces = []
        for p in range(hpg // 2):
            ms = []
            for jj in range(2):
                r = d * hpg + 2 * p + jj
                diff = cs[:, r:r + 1] - cst[r:r + 1, :]
                ms.append((g * jnp.exp(jnp.where(mask, diff, neg_inf))).astype(BF16))
            xp = xdt[:, p * LANES:(p + 1) * LANES]
            rhs = jnp.concatenate([jnp.where(lane_lo, xp, 0.0), jnp.where(lane_lo, 0.0, xp)], axis=0)
            pieces.append(jnp.dot(jnp.concatenate(ms, axis=1), rhs.astype(BF16),
                                  preferred_element_type=F32))
        y_c = jnp.concatenate(pieces, axis=1) + y_off
        s_t = jnp.dot(b_c.T.astype(BF16), (xdt * dte_x).astype(BF16), preferred_element_type=F32)
        ht_sc[d] = ht * ea_x + s_t
        if d == 0:
            y_ref[rows, :] = y_c + xa_c * dsk_ref[...]
        else:
            y_ref[rows, :] += y_c

    def fwd_body(c, carry):
        chunk_step(c, 0)
        return carry

    def bwd_body(c, carry):
        chunk_step(nc - 1 - c, 1)
        return carry

    lax.fori_loop(0, nc, fwd_body, 0)
    lax.fori_loop(0, nc, bwd_body, 0)

    yz = y_ref[...] * _silu(z_ref[...])
    y_ref[...] = yz
    ssq_ref[...] = jnp.broadcast_to(jnp.sum(yz * yz, axis=-1, keepdims=True), ssq_ref.shape)

    if emit_state:
        for d in range(2):
            for p in range(hpg // 2):
                blk = ht_sc[d, :, p * LANES:(p + 1) * LANES].T
                hfin_ref[d, 2 * p:2 * p + 2] = blk.reshape(2, SSD_HEAD_DIM, SSD_STATE)


def _ssd_constants():
    hpg = HEADS_PER_GROUP
    sel = np.zeros((SSD_GROUPS, LANES, LANES), np.float32)
    for g in range(SSD_GROUPS):
        for d in range(2):
            for j in range(hpg):
                sel[g, d * SSD_HEADS + hpg * g + j, d * hpg + j] = 1.0
    e2 = np.zeros((LANES, 2 * GROUP_WIDTH), np.float32)
    for d in range(2):
        for j in range(hpg):
            lo = d * GROUP_WIDTH + j * SSD_HEAD_DIM
            e2[d * hpg + j, lo:lo + SSD_HEAD_DIM] = 1.0
    tri = np.tril(np.ones((SSD_CHUNK, SSD_CHUNK), np.float32))
    return (jnp.asarray(sel, BF16), jnp.asarray(e2, BF16), jnp.asarray(tri, BF16),
            jnp.asarray(tri.T, BF16))


def _ssd(head, dt, conv_w, conv_b3, dtb, alog_g, dsk_g, consts, l, *, row0, n_seq, seq_len,
         h0=None, prev=None, emit_state=False):
    n_tok = head.shape[0]
    sel, e2, tri, trit = consts
    assert row0 % seq_len == 0 and seq_len % SSD_CHUNK == 0
    blk0 = row0 // seq_len
    gw = GROUP_WIDTH
    has_h0 = h0 is not None
    xcol = SSD_INNER // gw
    bcol = (SSD_INNER + SSD_INNER) // LANES
    ccol = bcol + SSD_GROUPS
    cwb = SSD_INNER // LANES
    cwc = cwb + SSD_GROUPS
    row = lambda b, g: blk0 + b
    in_specs = [
        pl.BlockSpec((seq_len, gw), lambda b, g: (row(b, g), g)),
        pl.BlockSpec((seq_len, gw), lambda b, g: (row(b, g), xcol + g)),
        pl.BlockSpec((seq_len, LANES), lambda b, g: (row(b, g), bcol + g)),
        pl.BlockSpec((seq_len, LANES), lambda b, g: (row(b, g), ccol + g)),
        pl.BlockSpec((seq_len, LANES), lambda b, g: (row(b, g), 0)),
        pl.BlockSpec((None, SSD_CONV, gw), lambda b, g: (l, 0, g)),
        pl.BlockSpec((None, SSD_CONV, LANES), lambda b, g: (l, 0, cwb + g)),
        pl.BlockSpec((None, SSD_CONV, LANES), lambda b, g: (l, 0, cwc + g)),
        pl.BlockSpec((None, 1, gw), lambda b, g: (l, 0, g)),
        pl.BlockSpec((None, 1, LANES), lambda b, g: (l, 0, cwb + g)),
        pl.BlockSpec((None, 1, LANES), lambda b, g: (l, 0, cwc + g)),
        pl.BlockSpec((None, 1, LANES), lambda b, g: (l, 0, 0)),
        pl.BlockSpec((None, None, 1, LANES), lambda b, g: (l, g, 0, 0)),
        pl.BlockSpec((None, None, 1, gw), lambda b, g: (l, g, 0, 0)),
        pl.BlockSpec((None, LANES, LANES), lambda b, g: (g, 0, 0)),
        pl.BlockSpec((LANES, 2 * gw), lambda b, g: (0, 0)),
        pl.BlockSpec((SSD_CHUNK, SSD_CHUNK), lambda b, g: (0, 0)),
        pl.BlockSpec((SSD_CHUNK, SSD_CHUNK), lambda b, g: (0, 0)),
    ]
    args = [head, head, head, head, dt, conv_w, conv_w, conv_w, conv_b3, conv_b3, conv_b3,
            dtb, alog_g, dsk_g, sel, e2, tri, trit]
    aliases = {}
    if has_h0:
        in_specs.append(pl.BlockSpec((None, None, 2, HEADS_PER_GROUP, SSD_HEAD_DIM, SSD_STATE),
                                     lambda b, g: (b, l, 0, g, 0, 0)))
        args.append(h0)
    if prev is not None:
        assert has_h0
        in_specs += [pl.BlockSpec(memory_space=pl.ANY), pl.BlockSpec(memory_space=pl.ANY)]
        aliases = {len(args): 0, len(args) + 1: 1}
        args += list(prev)
    out_specs = [
        pl.BlockSpec((seq_len, gw), lambda b, g: (row(b, g), g)),
        pl.BlockSpec((seq_len, LANES), lambda b, g: (row(b, g), g)),
    ]
    out_shape = [
        jax.ShapeDtypeStruct((n_tok, SSD_INNER), F32),
        jax.ShapeDtypeStruct((n_tok, SSD_GROUPS * LANES), F32),
    ]
    if emit_state:
        out_specs.append(pl.BlockSpec((None, 2, HEADS_PER_GROUP, SSD_HEAD_DIM, SSD_STATE),
                                      lambda b, g: (b, 0, g, 0, 0)))
        out_shape.append(jax.ShapeDtypeStruct((n_seq, 2, SSD_HEADS, SSD_HEAD_DIM, SSD_STATE), F32))
    kern = functools.partial(_ssd_kernel, seq_len=seq_len, has_h0=has_h0, emit_state=emit_state)
    return pl.pallas_call(
        kern,
        grid=(n_seq, SSD_GROUPS),
        in_specs=in_specs,
        out_specs=out_specs,
        out_shape=out_shape,
        input_output_aliases=aliases,
        scratch_shapes=[
            pltpu.VMEM((seq_len, gw), F32),
            pltpu.VMEM((seq_len, LANES), F32),
            pltpu.VMEM((seq_len, LANES), F32),
            pltpu.VMEM((seq_len, LANES), F32),
            pltpu.VMEM((seq_len, LANES), F32),
            pltpu.VMEM((2, SSD_STATE, gw), F32),
        ],
        compiler_params=_cparams(2),
        name="ssd_lat" if has_h0 else "ssd_ctx",
    )(*args)


def _scft_kernel(*refs, has_prev):
    b_ref, c_ref, x_ref, f_ref, cw_ref, cl_ref, sl_ref, cc_ref, sc_ref = refs[:9]
    pos = 9 + (2 if has_prev else 0)
    ysc_ref, yft_ref = refs[pos], refs[pos + 1]
    v = c_ref[...] * x_ref[...]
    ysc_ref[...] = b_ref[...] * _dwconv_rows(v, cw_ref[...], (SC_CONV - 1) // 2)
    u = f_ref[...].astype(BF16)
    p = jnp.dot(u, cc_ref[...], preferred_element_type=F32).astype(BF16)
    s = jnp.dot(u, sc_ref[...], preferred_element_type=F32).astype(BF16)
    yft_ref[...] = (jnp.dot(cl_ref[...], p, preferred_element_type=F32)
                    - jnp.dot(sl_ref[...], s, preferred_element_type=F32))


def _dft_mats(n):
    k = np.arange(n)
    ang = 2.0 * np.pi * ((k[:, None] * k[None, :]) % n) / n
    scale = 1.0 / math.sqrt(n)
    return jnp.asarray(np.cos(ang) * scale, BF16), jnp.asarray(np.sin(ang) * scale, BF16)


def _scft(tail, sc_conv_w, l, *, row0, n_seq, seq_len, prev=None):
    n_tok = tail.shape[0]
    tw = FT_GROUP_DIM
    assert row0 % seq_len == 0
    blk0 = row0 // seq_len
    nt = SC_WIDTH // tw
    cl, sl = _dft_mats(seq_len)
    cc, sc = _dft_mats(tw)
    in_specs = [
        pl.BlockSpec((seq_len, tw), lambda b, j: (blk0 + b, j)),
        pl.BlockSpec((seq_len, tw), lambda b, j: (blk0 + b, nt + j)),
        pl.BlockSpec((seq_len, tw), lambda b, j: (blk0 + b, 2 * nt + j)),
        pl.BlockSpec((seq_len, tw), lambda b, j: (blk0 + b, 3 * nt + j)),
        pl.BlockSpec((None, SC_CONV, tw), lambda b, j: (l, 0, j)),
        pl.BlockSpec((seq_len, seq_len), lambda b, j: (0, 0)),
        pl.BlockSpec((seq_len, seq_len), lambda b, j: (0, 0)),
        pl.BlockSpec((tw, tw), lambda b, j: (0, 0)),
        pl.BlockSpec((tw, tw), lambda b, j: (0, 0)),
    ]
    args = [tail, tail, tail, tail, sc_conv_w, cl, sl, cc, sc]
    aliases = {}
    if prev is not None:
        in_specs += [pl.BlockSpec(memory_space=pl.ANY), pl.BlockSpec(memory_space=pl.ANY)]
        aliases = {len(args): 0, len(args) + 1: 1}
        args += list(prev)
    kern = functools.partial(_scft_kernel, has_prev=prev is not None)
    return pl.pallas_call(
        kern,
        grid=(n_seq, nt),
        in_specs=in_specs,
        out_specs=[
            pl.BlockSpec((seq_len, tw), lambda b, j: (blk0 + b, j)),
            pl.BlockSpec((seq_len, tw), lambda b, j: (blk0 + b, j)),
        ],
        out_shape=[
            jax.ShapeDtypeStruct((n_tok, SC_WIDTH), F32),
            jax.ShapeDtypeStruct((n_tok, FT_WIDTH), F32),
        ],
        input_output_aliases=aliases,
        compiler_params=_cparams(2),
        name="scft_lat" if prev is not None else "scft_ctx",
    )(*args)


def _merge_kernel(yssd_ref, ssq_ref, gssd_ref, ysc_ref, yft_ref, g0_ref, g1_ref, g2_ref,
                  w0_ref, w1_ref, w2_ref, o_ref, a0_sc, a1_sc, a2_sc):
    j = pl.program_id(1)

    @pl.when(j == 0)
    def _():
        ssq = ssq_ref[...]
        tot = ssq[:, 0:1]
        for g in range(1, SSD_GROUPS):
            tot = tot + ssq[:, g * LANES:g * LANES + 1]
        inv = lax.rsqrt(tot * (1.0 / SSD_INNER) + NORM_EPS)
        a0_sc[...] = (yssd_ref[...] * inv * gssd_ref[...]).astype(BF16)
        a1_sc[...] = ysc_ref[...].astype(BF16)
        a2_sc[...] = yft_ref[...].astype(BF16)

    acc = jax.nn.sigmoid(g0_ref[...]) * jnp.dot(a0_sc[...], w0_ref[...].astype(BF16),
                                                 preferred_element_type=F32)
    acc += jax.nn.sigmoid(g1_ref[...]) * jnp.dot(a1_sc[...], w1_ref[...].astype(BF16),
                                                  preferred_element_type=F32)
    acc += jax.nn.sigmoid(g2_ref[...]) * jnp.dot(a2_sc[...], w2_ref[...].astype(BF16),
                                                  preferred_element_type=F32)
    o_ref[...] = acc.astype(o_ref.dtype)


def _merge(yssd, ssq, ssd_norm_g3, ysc, yft, tail, w_br_ssd, w_br_sc, w_br_ft, l, tm=512, tn=512):
    n_tok = yssd.shape[0]
    d = D_MODEL
    gate0 = (3 * SC_WIDTH + FT_WIDTH) // tn
    per = d // tn
    return pl.pallas_call(
        _merge_kernel,
        grid=(n_tok // tm, d // tn),
        in_specs=[
            _single((tm, SSD_INNER), lambda i, j: (i, 0)),
            _single((tm, SSD_GROUPS * LANES), lambda i, j: (i, 0)),
            pl.BlockSpec((None, 1, SSD_INNER), lambda i, j: (l, 0, 0)),
            _single((tm, SC_WIDTH), lambda i, j: (i, 0)),
            _single((tm, FT_WIDTH), lambda i, j: (i, 0)),
            pl.BlockSpec((tm, tn), lambda i, j: (i, gate0 + j)),
            pl.BlockSpec((tm, tn), lambda i, j: (i, gate0 + per + j)),
            pl.BlockSpec((tm, tn), lambda i, j: (i, gate0 + 2 * per + j)),
            pl.BlockSpec((None, SSD_INNER, tn), lambda i, j: (l, 0, j)),
            pl.BlockSpec((None, SC_WIDTH, tn), lambda i, j: (l, 0, j)),
            pl.BlockSpec((None, FT_WIDTH, tn), lambda i, j: (l, 0, j)),
        ],
        out_specs=pl.BlockSpec((tm, tn), lambda i, j: (i, j)),
        out_shape=jax.ShapeDtypeStruct((n_tok, d), BF16),
        scratch_shapes=[
            pltpu.VMEM((tm, SSD_INNER), BF16),
            pltpu.VMEM((tm, SC_WIDTH), BF16),
            pltpu.VMEM((tm, FT_WIDTH), BF16),
        ],
        compiler_params=_cparams(2),
        name="merge",
    )(yssd, ssq, ssd_norm_g3, ysc, yft, tail, tail, tail, w_br_ssd, w_br_sc, w_br_ft)


def _outproj_kernel(m_ref, w_ref, x_ref, mod_ref, ng_ref, o_ref, *, nk):
    k = pl.program_id(1)

    @pl.when(k == 0)
    def _():
        o_ref[...] = jnp.zeros_like(o_ref)

    o_ref[...] += jnp.dot(m_ref[...], w_ref[...].astype(BF16), preferred_element_type=F32)

    @pl.when(k == nk - 1)
    def _():
        y = _rms(o_ref[...], ng_ref[3:4, :])
        o_ref[...] = x_ref[...] + mod_ref[5:6, :] * y


def _outproj(merged, w_out, x, mod4, norm_g, l, rows, tm=512, tk=512):
    n_tok, d = x.shape
    nk = d // tk
    grp = lambda i: rows.group_of_tile(i, tm)
    kern = functools.partial(_outproj_kernel, nk=nk)
    return pl.pallas_call(
        kern,
        grid=(n_tok // tm, nk),
        in_specs=[
            pl.BlockSpec((tm, tk), lambda i, k: (i, k)),
            pl.BlockSpec((None, tk, d), lambda i, k: (l, k, 0)),
            _single((tm, d), lambda i, k: (i, 0)),
            pl.BlockSpec((None, None, N_MOD, d), lambda i, k: (l, grp(i), 0, 0)),
            pl.BlockSpec((None, 6, d), lambda i, k: (l, 0, 0)),
        ],
        out_specs=pl.BlockSpec((tm, d), lambda i, k: (i, 0)),
        out_shape=jax.ShapeDtypeStruct((n_tok, d), F32),
        compiler_params=_cparams(2),
        name="outproj",
    )(merged, w_out, x, mod4, norm_g)


def _grid_pos_emb(n_tok):
    rows = n_tok // GRID_W
    t = np.arange(rows * GRID_W)
    r = (t // GRID_W).astype(np.float32)[:, None]
    col = (t % GRID_W).astype(np.float32)[:, None]
    nf = D_MODEL // 4
    omega = (1.0 / (np.float32(POS_BASE) ** (np.arange(nf, dtype=np.float32) / np.float32(nf)))).astype(np.float32)
    ro = (r * omega).astype(np.float32).astype(np.float64)
    co = (col * omega).astype(np.float32).astype(np.float64)
    return np.concatenate([np.sin(ro), np.cos(ro), np.sin(co), np.cos(co)], axis=-1).astype(np.float32)


def _group_lanes(p, hpg=HEADS_PER_GROUP):
    depth = p.shape[0]
    t = p.reshape(depth, 2, SSD_GROUPS, hpg).transpose(0, 2, 1, 3).reshape(depth, SSD_GROUPS, 1, 2 * hpg)
    return jnp.pad(t, ((0, 0), (0, 0), (0, 0), (0, LANES - 2 * hpg)))


def kernel(x_prompt, x_sample, state_ssd, c, c_ctx, ada_w, ada_b, norm_g, ffn1_wgu, ffn1_wd, w_in,
           ssd_conv_w, ssd_conv_b, ssd_dt_bias, ssd_a_log, ssd_d, ssd_norm_g, sc_conv_w,
           w_br_ssd, w_br_sc, w_br_ft, w_out, ffn2_wgu, ffn2_wd):
    n_ctx, ctx_len, d = x_prompt.shape
    n_lat, lat_len, _ = x_sample.shape
    depth = ada_w.shape[0]
    rows = _Rows(n_ctx * ctx_len, n_lat, lat_len)

    c8 = jnp.concatenate([c_ctx[None, :], c, jnp.zeros((8 - 1 - n_lat, d), F32)], axis=0)
    mod4 = _modulation(c8, ada_w, ada_b)[:, :1 + n_lat].reshape(depth, 1 + n_lat, N_MOD, d)

    x = jnp.concatenate([x_prompt.reshape(n_ctx * ctx_len, d),
                         (x_sample + jnp.asarray(_grid_pos_emb(lat_len))[None]).reshape(n_lat * lat_len, d)],
                        axis=0)

    consts = _ssd_constants()
    w_tail = w_in[:, :, TAIL_COL:]
    conv_b3 = ssd_conv_b.reshape(depth, 1, SSD_XBC)
    dtb = jnp.pad(ssd_dt_bias.reshape(depth, 1, 2 * SSD_HEADS), ((0, 0), (0, 0), (0, LANES - 2 * SSD_HEADS)))
    alog_g = _group_lanes(ssd_a_log)
    dsk_g = jnp.repeat(ssd_d, SSD_HEAD_DIM, axis=-1).reshape(depth, SSD_GROUPS, 1, GROUP_WIDTH)
    ssd_norm_g3 = ssd_norm_g.reshape(depth, 1, SSD_INNER)

    states = []
    for l in range(depth):
        x = _ffn(x, mod4, norm_g, ffn1_wgu, ffn1_wd, l, 0, rows)
        head, dt, tail = _inproj(x, mod4, norm_g, w_in, w_tail, l, rows)
        ssd_args = (head, dt, ssd_conv_w, conv_b3, dtb, alog_g, dsk_g, consts, l)
        yssd, ssq, st = _ssd(*ssd_args, row0=0, n_seq=n_ctx, seq_len=ctx_len, emit_state=True)
        yssd, ssq = _ssd(*ssd_args, row0=rows.n_ctx_tok, n_seq=n_lat, seq_len=lat_len,
                         h0=state_ssd, prev=(yssd, ssq))
        states.append(st)
        ysc, yft = _scft(tail, sc_conv_w, l, row0=0, n_seq=n_ctx, seq_len=ctx_len)
        ysc, yft = _scft(tail, sc_conv_w, l, row0=rows.n_ctx_tok, n_seq=n_lat, seq_len=lat_len,
                         prev=(ysc, yft))
        merged = _merge(yssd, ssq, ssd_norm_g3, ysc, yft, tail, w_br_ssd, w_br_sc, w_br_ft, l)
        x = _outproj(merged, w_out, x, mod4, norm_g, l, rows)
        x = _ffn(x, mod4, norm_g, ffn2_wgu, ffn2_wd, l, 1, rows)

    y_prompt = x[:rows.n_ctx_tok].reshape(n_ctx, ctx_len, d)
    y_sample = x[rows.n_ctx_tok:].reshape(n_lat, lat_len, d)
    return (y_prompt, y_sample, jnp.stack(states, axis=1))
```

```python
import functools
import math

import numpy as np
import jax
import jax.numpy as jnp
from jax import lax
from jax.experimental import pallas as pl
from jax.experimental.pallas import tpu as pltpu

F32 = jnp.float32
BF16 = jnp.bfloat16

D_MODEL = 2048
DEPTH = 4
GRID_W = 64
POS_BASE = 10000.0
NORM_EPS = 1e-6
SSD_HEADS = 32
SSD_HEAD_DIM = 64
SSD_INNER = SSD_HEADS * SSD_HEAD_DIM
SSD_GROUPS = 4
SSD_STATE = 128
SSD_CONV = 4
SSD_CHUNK = 128
SSD_XBC = SSD_INNER + 2 * SSD_GROUPS * SSD_STATE
SC_WIDTH = 1024
SC_CONV = 3
FT_WIDTH = 1024
FT_GROUPS = 4
FT_GROUP_DIM = FT_WIDTH // FT_GROUPS
D_FF = 5504
N_BRANCH = 3
N_MOD = 9
HEADS_PER_GROUP = SSD_HEADS // SSD_GROUPS
GROUP_WIDTH = HEADS_PER_GROUP * SSD_HEAD_DIM
HEAD_COLS = SSD_INNER + SSD_XBC
DT_COL = HEAD_COLS
DT_WIDTH = 2 * SSD_HEADS
TAIL_COL = HEAD_COLS + DT_WIDTH
TAIL_COLS = 3 * SC_WIDTH + FT_WIDTH + N_BRANCH * D_MODEL
IN_COLS = TAIL_COL + TAIL_COLS

LANES = 128
BF16_ROWS = 16
VMEM_LIMIT_BYTES = 56 * 1024 * 1024


def _cparams(n_axes):
    return pltpu.CompilerParams(dimension_semantics=("arbitrary",) * n_axes,
                                vmem_limit_bytes=VMEM_LIMIT_BYTES)


def _silu(x):
    return x * jax.nn.sigmoid(x)


def _rms(x, g):
    ms = jnp.mean(x * x, axis=-1, keepdims=True)
    return x * lax.rsqrt(ms + NORM_EPS) * g


def _norm_mod(x, g, shift, scale):
    return _rms(x, g) * (1.0 + scale) + shift


def _split_bf16(x, n):
    parts = []
    r = x
    for i in range(n):
        p = r.astype(BF16)
        parts.append(p)
        if i + 1 < n:
            r = r - p.astype(F32)
    return parts


def _dot_r01(x, m01, n=3):
    acc = None
    for p in _split_bf16(x, n):
        t = jnp.dot(p, m01, preferred_element_type=F32)
        acc = t if acc is None else acc + t
    return acc


def _dot_l01(m01, x, n=3):
    acc = None
    for p in _split_bf16(x, n):
        t = jnp.dot(m01, p, preferred_element_type=F32)
        acc = t if acc is None else acc + t
    return acc


def _mod_kernel(c_ref, w_ref, b_ref, o_ref):
    s = _silu(c_ref[...]).astype(BF16)
    o_ref[...] = jnp.dot(s, w_ref[...].astype(BF16), preferred_element_type=F32) + b_ref[...]


def _modulation(c8, ada_w, ada_b, tn=1024):
    depth, d, n = ada_w.shape
    return pl.pallas_call(
        _mod_kernel,
        grid=(depth, n // tn),
        in_specs=[
            pl.BlockSpec((8, d), lambda l, j: (0, 0)),
            pl.BlockSpec((None, d, tn), lambda l, j: (l, 0, j)),
            pl.BlockSpec((None, 1, tn), lambda l, j: (l, 0, j)),
        ],
        out_specs=pl.BlockSpec((None, 8, tn), lambda l, j: (l, 0, j)),
        out_shape=jax.ShapeDtypeStruct((depth, 8, n), F32),
        compiler_params=_cparams(2),
        name="adaln_mod",
    )(c8, ada_w, ada_b.reshape(depth, 1, n))


class _Rows:
    def __init__(self, n_ctx_tok, n_lat, lat_len):
        self.n_ctx_tok = n_ctx_tok
        self.n_lat = n_lat
        self.lat_len = lat_len
        self.n_tok = n_ctx_tok + n_lat * lat_len

    def group_of_tile(self, i, tm):
        assert self.n_ctx_tok % tm == 0 and self.lat_len % tm == 0
        n_ctx_tiles = self.n_ctx_tok // tm
        per_lat = self.lat_len // tm
        return jnp.where(i < n_ctx_tiles, 0, 1 + (i - n_ctx_tiles) // per_lat)


def _single(block_shape, index_map):
    return pl.BlockSpec(block_shape, index_map, pipeline_mode=pl.Buffered(1))


FFN_SUB = 128
FFN_NSUB = D_FF // FFN_SUB


def _ffn_kernel(*refs, nk, mod_row, ng_row, emit_next):
    x_ref, mod_ref, ng_ref, wg0, wg1, wu0, wu1, wd0, wd1, o_ref = refs[:10]
    pos = 10
    if emit_next:
        un_ref = refs[pos]
        pos += 1
    u_sc, wup_sc, wdn_sc = refs[pos:pos + 3]
    k = pl.program_id(1)

    @pl.when(k == 0)
    def _():
        u = _norm_mod(x_ref[...], ng_ref[ng_row:ng_row + 1, :],
                      mod_ref[mod_row:mod_row + 1, :], mod_ref[mod_row + 1:mod_row + 2, :])
        u_sc[...] = u.astype(BF16)
        o_ref[...] = jnp.zeros_like(o_ref)

    s = FFN_SUB
    wup_sc[:, 0 * s:1 * s] = wg0[...].astype(BF16)
    wup_sc[:, 1 * s:2 * s] = wg1[...].astype(BF16)
    wup_sc[:, 2 * s:3 * s] = wu0[...].astype(BF16)
    wup_sc[:, 3 * s:4 * s] = wu1[...].astype(BF16)
    wdn_sc[0 * s:1 * s, :] = wd0[...].astype(BF16)
    wdn_sc[1 * s:2 * s, :] = wd1[...].astype(BF16)

    h = jnp.dot(u_sc[...], wup_sc[...], preferred_element_type=F32)
    a = _silu(h[:, :2 * s]) * h[:, 2 * s:]
    second_valid = (2 * k + 1) < FFN_NSUB
    lane = lax.broadcasted_iota(jnp.int32, a.shape, 1)
    a = jnp.where(jnp.logical_or(lane < s, second_valid), a, 0.0)
    o_ref[...] += jnp.dot(a.astype(BF16), wdn_sc[...], preferred_element_type=F32)

    @pl.when(k == nk - 1)
    def _():
        y = _rms(o_ref[...], ng_ref[ng_row + 1:ng_row + 2, :])
        x_new = x_ref[...] + 0.5 * mod_ref[mod_row + 2:mod_row + 3, :] * y
        o_ref[...] = x_new
        if emit_next:
            un = _norm_mod(x_new, ng_ref[ng_row + 2:ng_row + 3, :],
                           mod_ref[mod_row + 3:mod_row + 4, :], mod_ref[mod_row + 4:mod_row + 5, :])
            un_ref[...] = un.astype(BF16)


def _ffn(x, mod4, norm_g, wgu, wd, l, which, rows, tm=1024):
    n_tok, d = x.shape
    nk = pl.cdiv(FFN_NSUB, 2)
    last = FFN_NSUB - 1
    mod_row = 0 if which == 0 else 6
    ng_row = 0 if which == 0 else 4
    emit_next = which == 0
    grp = lambda i: rows.group_of_tile(i, tm)
    kern = functools.partial(_ffn_kernel, nk=nk, mod_row=mod_row, ng_row=ng_row, emit_next=emit_next)
    out_specs = [_single((tm, d), lambda i, k: (i, 0))]
    out_shape = [jax.ShapeDtypeStruct((n_tok, d), F32)]
    if emit_next:
        out_specs.append(pl.BlockSpec((tm, d), lambda i, k: (i, 0)))
        out_shape.append(jax.ShapeDtypeStruct((n_tok, d), BF16))
    return pl.pallas_call(
        kern,
        grid=(n_tok // tm, nk),
        in_specs=[
            _single((tm, d), lambda i, k: (i, 0)),
            pl.BlockSpec((None, None, N_MOD, d), lambda i, k: (l, grp(i), 0, 0)),
            pl.BlockSpec((None, 6, d), lambda i, k: (l, 0, 0)),
            pl.BlockSpec((None, d, FFN_SUB), lambda i, k: (l, 0, 2 * k)),
            pl.BlockSpec((None, d, FFN_SUB), lambda i, k: (l, 0, jnp.minimum(2 * k + 1, last))),
            pl.BlockSpec((None, d, FFN_SUB), lambda i, k: (l, 0, FFN_NSUB + 2 * k)),
            pl.BlockSpec((None, d, FFN_SUB), lambda i, k: (l, 0, FFN_NSUB + jnp.minimum(2 * k + 1, last))),
            pl.BlockSpec((None, FFN_SUB, d), lambda i, k: (l, 2 * k, 0)),
            pl.BlockSpec((None, FFN_SUB, d), lambda i, k: (l, jnp.minimum(2 * k + 1, last), 0)),
        ],
        out_specs=out_specs,
        out_shape=out_shape,
        scratch_shapes=[
            pltpu.VMEM((tm, d), BF16),
            pltpu.VMEM((d, 4 * FFN_SUB), BF16),
            pltpu.VMEM((2 * FFN_SUB, d), BF16),
        ],
        compiler_params=_cparams(2),
        name="ffn",
    )(x, mod4, norm_g, wgu, wgu, wgu, wgu, wd, wd)


def _inproj_kernel(u_ref, wa_ref, wb_ref, oh_ref, odt_ref, ot_ref, w_sc, *, n_head, tm):
    j = pl.program_id(0)
    m = pl.program_id(1)
    off = TAIL_COL % LANES

    @pl.when(jnp.logical_and(m == 0, j < n_head))
    def _():
        w_sc[...] = wa_ref[...].astype(BF16)

    @pl.when(jnp.logical_and(m == 0, j == n_head))
    def _():
        w_sc[:, :LANES] = wb_ref[...].astype(BF16)

    @pl.when(jnp.logical_and(m == 0, j > n_head))
    def _():
        w = jnp.concatenate([wa_ref[:, off:], wb_ref[:, :off]], axis=1)
        w_sc[...] = w.astype(BF16)

    u = u_ref[pl.ds(pl.multiple_of(m * tm, tm), tm), :]

    @pl.when(j < n_head)
    def _():
        oh_ref[...] = jnp.dot(u, w_sc[...], preferred_element_type=F32)

    @pl.when(j == n_head)
    def _():
        odt_ref[...] = jnp.dot(u, w_sc[:, :LANES], preferred_element_type=F32)

    @pl.when(j > n_head)
    def _():
        ot_ref[...] = jnp.dot(u, w_sc[...], preferred_element_type=F32)


def _inproj(u, w_in, l, tm=1024, tn=512):
    n_tok, d = u.shape
    assert TAIL_COL % tn == TAIL_COL % LANES and HEAD_COLS % tn == 0 and TAIL_COLS % tn == 0
    n_head = HEAD_COLS // tn
    n_tail = TAIL_COLS // tn
    n_m = n_tok // tm
    sub = tn // LANES
    head_j = lambda j: jnp.minimum(j, n_head - 1)
    tail_j = lambda j: jnp.clip(j - n_head - 1, 0, n_tail - 1)
    head_m = lambda j, m: jnp.where(j < n_head, m, n_m - 1)
    dt_m = lambda j, m: jnp.where(j < n_head, 0, jnp.where(j == n_head, m, n_m - 1))
    tail_m = lambda j, m: jnp.where(j > n_head, m, 0)
    wide_j = lambda j: jnp.where(j <= n_head, jnp.minimum(j, n_head), j - 1)
    narrow_j = lambda j: jnp.where(j <= n_head, DT_COL // LANES, j * sub)
    kern = functools.partial(_inproj_kernel, n_head=n_head, tm=tm)
    return pl.pallas_call(
        kern,
        grid=(n_head + 1 + n_tail, n_m),
        in_specs=[
            _single((n_tok, d), lambda j, m: (0, 0)),
            pl.BlockSpec((None, d, tn), lambda j, m: (l, 0, wide_j(j))),
            pl.BlockSpec((None, d, LANES), lambda j, m: (l, 0, narrow_j(j))),
        ],
        out_specs=[
            pl.BlockSpec((tm, tn), lambda j, m: (head_m(j, m), head_j(j))),
            pl.BlockSpec((tm, LANES), lambda j, m: (dt_m(j, m), 0)),
            pl.BlockSpec((tm, tn), lambda j, m: (tail_m(j, m), tail_j(j))),
        ],
        out_shape=[
            jax.ShapeDtypeStruct((n_tok, HEAD_COLS), F32),
            jax.ShapeDtypeStruct((n_tok, LANES), F32),
            jax.ShapeDtypeStruct((n_tok, TAIL_COLS), F32),
        ],
        scratch_shapes=[pltpu.VMEM((d, tn), BF16)],
        compiler_params=_cparams(2),
        name="inproj",
    )(u, w_in, w_in)


def _dwconv_rows(x, w, left):
    n_rows = x.shape[0]
    row = lax.broadcasted_iota(jnp.int32, x.shape, 0)
    out = None
    for k in range(w.shape[0]):
        off = k - left
        if off == 0:
            term = x
        else:
            shifted = pltpu.roll(x, (-off) % n_rows, axis=0)
            valid = jnp.logical_and(row + off >= 0, row + off < n_rows)
            term = jnp.where(valid, shifted, 0.0)
        term = term * w[k:k + 1, :]
        out = term if out is None else out + term
    return out


def _ssd_kernel(*refs, seq_len, has_h0, n_alias, emit_state):
    (z_ref, xr_ref, br_ref, cr_ref, dt_ref, cwx_ref, cwb_ref, cwc_ref, cbx_ref, cbb_ref, cbc_ref,
     dtb_ref, alog_ref, dsk_ref, gn_ref, sel_ref, e2_ref, tri_ref, trit_ref) = refs[:19]
    pos = 19
    h0_ref = None
    if has_h0:
        h0_ref = refs[pos]
        pos += 1
    pos += n_alias
    yn_ref = refs[pos]
    pos += 1
    hfin_ref = None
    if emit_state:
        hfin_ref = refs[pos]
        pos += 1
    xa_sc, ba_sc, ca_sc, dts_sc, a_sc, ht_sc, yg_sc, y_sc, ssq_sc = refs[pos:pos + 9]

    gi = pl.program_id(1)
    q = SSD_CHUNK
    nc = seq_len // q
    left = (SSD_CONV - 1) // 2
    hpg = HEADS_PER_GROUP
    gw = GROUP_WIDTH

    xa = _silu(_dwconv_rows(xr_ref[...], cwx_ref[...], left) + cbx_ref[...])
    xa_sc[...] = xa
    ba_sc[...] = _silu(_dwconv_rows(br_ref[...], cwb_ref[...], left) + cbb_ref[...])
    ca_sc[...] = _silu(_dwconv_rows(cr_ref[...], cwc_ref[...], left) + cbc_ref[...])
    yg_sc[...] = xa * dsk_ref[...]

    draw = dt_ref[...] + dtb_ref[...]
    dt_all = jnp.maximum(draw, 0.0) + jnp.log1p(jnp.exp(-jnp.abs(draw)))
    dts = _dot_r01(dt_all, sel_ref[...])
    dts_sc[...] = dts
    a_sc[...] = dts * (-jnp.exp(alog_ref[...]))

    for d in range(2):
        if has_h0:
            for p in range(hpg // 2):
                blk = h0_ref[d, 2 * p:2 * p + 2].reshape(2 * SSD_HEAD_DIM, SSD_STATE)
                ht_sc[d, :, p * LANES:(p + 1) * LANES] = blk.T
        else:
            ht_sc[d] = jnp.zeros((SSD_STATE, gw), F32)

    ri = lax.broadcasted_iota(jnp.int32, (q, q), 0)
    ci = lax.broadcasted_iota(jnp.int32, (q, q), 1)
    lane_lo = lax.broadcasted_iota(jnp.int32, (q, LANES), 1) < SSD_HEAD_DIM
    neg_inf = jnp.float32(-jnp.inf)

    def chunk_dir(c, d):
        rows = pl.ds(pl.multiple_of(c * q, q), q)
        a_c = a_sc[rows, :]
        if d == 0:
            cs = _dot_l01(tri_ref[...], a_c)
            tot = cs[q - 1:q, :]
            mask = ri >= ci
        else:
            cs = _dot_l01(trit_ref[...], a_c)
            tot = cs[0:1, :]
            mask = ri <= ci
        cst = cs.T
        stack = jnp.concatenate([
            dts_sc[rows, :].astype(BF16),
            jnp.exp(cs).astype(BF16),
            jnp.exp(tot - cs).astype(BF16),
            jnp.broadcast_to(jnp.exp(tot), (BF16_ROWS, LANES)).astype(BF16)], axis=0)
        ex = jnp.dot(stack, e2_ref[:, d * gw:(d + 1) * gw], preferred_element_type=F32)
        dtx, ecs_x, dte_x, ea_x = ex[0:q], ex[q:2 * q], ex[2 * q:3 * q], ex[3 * q:3 * q + 1]
        b_c = ba_sc[rows, :]
        c_bf = ca_sc[rows, :].astype(BF16)
        xdt = xa_sc[rows, :] * dtx
        g = lax.dot_general(c_bf, b_c.astype(BF16), (((1,), (1,)), ((), ())),
                            preferred_element_type=F32)
        ht = ht_sc[d]
        y_off = jnp.dot(c_bf, ht.astype(BF16), preferred_element_type=F32) * ecs_x
        pieces = []
        for p in range(hpg // 2):
            ms = []
            for jj in range(2):
                r = d * hpg + 2 * p + jj
                diff = cs[:, r:r + 1] - cst[r:r + 1, :]
                ms.append((g * jnp.exp(jnp.where(mask, diff, neg_inf))).astype(BF16))
            xp = xdt[:, p * LANES:(p + 1) * LANES]
            rhs = jnp.concatenate([jnp.where(lane_lo, xp, 0.0), jnp.where(lane_lo, 0.0, xp)], axis=0)
            pieces.append(jnp.dot(jnp.concatenate(ms, axis=1), rhs.astype(BF16),
                                  preferred_element_type=F32))
        s_t = jnp.dot(b_c.T.astype(BF16), (xdt * dte_x).astype(BF16), preferred_element_type=F32)
        ht_sc[d] = ht * ea_x + s_t
        yg_sc[rows, :] += jnp.concatenate(pieces, axis=1) + y_off

    def body(c, carry):
        chunk_dir(c, 0)
        chunk_dir(nc - 1 - c, 1)
        return carry

    lax.fori_loop(0, nc, body, 0, unroll=2)

    yz = yg_sc[...] * _silu(z_ref[...])
    y_sc[gi] = yz
    part = jnp.broadcast_to(jnp.sum(yz * yz, axis=-1, keepdims=True), ssq_sc.shape)

    @pl.when(gi == 0)
    def _():
        ssq_sc[...] = part

    @pl.when(gi > 0)
    def _():
        ssq_sc[...] += part

    @pl.when(gi == SSD_GROUPS - 1)
    def _():
        inv = lax.rsqrt(ssq_sc[:, 0:1] * (1.0 / SSD_INNER) + NORM_EPS)
        for gg in range(SSD_GROUPS):
            cols = slice(gg * gw, (gg + 1) * gw)
            yn_ref[:, cols] = (y_sc[gg] * inv * gn_ref[:, cols]).astype(BF16)

    if emit_state:
        for d in range(2):
            for p in range(hpg // 2):
                blk = ht_sc[d, :, p * LANES:(p + 1) * LANES].T
                hfin_ref[d, 2 * p:2 * p + 2] = blk.reshape(2, SSD_HEAD_DIM, SSD_STATE)


def _ssd_constants():
    hpg = HEADS_PER_GROUP
    sel = np.zeros((SSD_GROUPS, LANES, LANES), np.float32)
    for g in range(SSD_GROUPS):
        for d in range(2):
            for j in range(hpg):
                sel[g, d * SSD_HEADS + hpg * g + j, d * hpg + j] = 1.0
    e2 = np.zeros((LANES, 2 * GROUP_WIDTH), np.float32)
    for d in range(2):
        for j in range(hpg):
            lo = d * GROUP_WIDTH + j * SSD_HEAD_DIM
            e2[d * hpg + j, lo:lo + SSD_HEAD_DIM] = 1.0
    tri = np.tril(np.ones((SSD_CHUNK, SSD_CHUNK), np.float32))
    return (jnp.asarray(sel, BF16), jnp.asarray(e2, BF16), jnp.asarray(tri, BF16),
            jnp.asarray(tri.T, BF16))


def _ssd(head, dt, conv_w, conv_b3, dtb, alog_g, dsk_g, norm_g3, consts, l, *, row0, n_seq, seq_len,
         h0=None, prev_y=None, prev_state=None, state_shape=None):
    n_tok = head.shape[0]
    sel, e2, tri, trit = consts
    assert row0 % seq_len == 0 and seq_len % (2 * SSD_CHUNK) == 0
    blk0 = row0 // seq_len
    gw = GROUP_WIDTH
    has_h0 = h0 is not None
    emit_state = state_shape is not None
    xcol = SSD_INNER // gw
    bcol = (SSD_INNER + SSD_INNER) // LANES
    ccol = bcol + SSD_GROUPS
    cwb = SSD_INNER // LANES
    cwc = cwb + SSD_GROUPS
    in_specs = [
        pl.BlockSpec((seq_len, gw), lambda b, g: (blk0 + b, g)),
        pl.BlockSpec((seq_len, gw), lambda b, g: (blk0 + b, xcol + g)),
        pl.BlockSpec((seq_len, LANES), lambda b, g: (blk0 + b, bcol + g)),
        pl.BlockSpec((seq_len, LANES), lambda b, g: (blk0 + b, ccol + g)),
        pl.BlockSpec((seq_len, LANES), lambda b, g: (blk0 + b, 0)),
        pl.BlockSpec((None, SSD_CONV, gw), lambda b, g: (l, 0, g)),
        pl.BlockSpec((None, SSD_CONV, LANES), lambda b, g: (l, 0, cwb + g)),
        pl.BlockSpec((None, SSD_CONV, LANES), lambda b, g: (l, 0, cwc + g)),
        pl.BlockSpec((None, 1, gw), lambda b, g: (l, 0, g)),
        pl.BlockSpec((None, 1, LANES), lambda b, g: (l, 0, cwb + g)),
        pl.BlockSpec((None, 1, LANES), lambda b, g: (l, 0, cwc + g)),
        pl.BlockSpec((None, 1, LANES), lambda b, g: (l, 0, 0)),
        pl.BlockSpec((None, None, 1, LANES), lambda b, g: (l, g, 0, 0)),
        pl.BlockSpec((None, None, 1, gw), lambda b, g: (l, g, 0, 0)),
        pl.BlockSpec((None, 1, SSD_INNER), lambda b, g: (l, 0, 0)),
        pl.BlockSpec((None, LANES, LANES), lambda b, g: (g, 0, 0)),
        pl.BlockSpec((LANES, 2 * gw), lambda b, g: (0, 0)),
        pl.BlockSpec((SSD_CHUNK, SSD_CHUNK), lambda b, g: (0, 0)),
        pl.BlockSpec((SSD_CHUNK, SSD_CHUNK), lambda b, g: (0, 0)),
    ]
    args = [head, head, head, head, dt, conv_w, conv_w, conv_w, conv_b3, conv_b3, conv_b3,
            dtb, alog_g, dsk_g, norm_g3, sel, e2, tri, trit]
    if has_h0:
        in_specs.append(pl.BlockSpec((None, None, 2, HEADS_PER_GROUP, SSD_HEAD_DIM, SSD_STATE),
                                     lambda b, g: (b, l, 0, g, 0, 0)))
        args.append(h0)
    aliases = {}
    if prev_y is not None:
        in_specs.append(pl.BlockSpec(memory_space=pl.ANY))
        aliases[len(args)] = 0
        args.append(prev_y)
    if prev_state is not None:
        assert emit_state
        in_specs.append(pl.BlockSpec(memory_space=pl.ANY))
        aliases[len(args)] = 1
        args.append(prev_state)
    out_specs = [pl.BlockSpec((seq_len, SSD_INNER), lambda b, g: (blk0 + b, 0))]
    out_shape = [jax.ShapeDtypeStruct((n_tok, SSD_INNER), BF16)]
    if emit_state:
        out_specs.append(pl.BlockSpec((None, None, 2, HEADS_PER_GROUP, SSD_HEAD_DIM, SSD_STATE),
                                      lambda b, g: (b, l, 0, g, 0, 0)))
        out_shape.append(jax.ShapeDtypeStruct(state_shape, F32))
    kern = functools.partial(_ssd_kernel, seq_len=seq_len, has_h0=has_h0, n_alias=len(aliases),
                             emit_state=emit_state)
    return pl.pallas_call(
        kern,
        grid=(n_seq, SSD_GROUPS),
        in_specs=in_specs,
        out_specs=out_specs,
        out_shape=out_shape,
        input_output_aliases=aliases,
        scratch_shapes=[
            pltpu.VMEM((seq_len, gw), F32),
            pltpu.VMEM((seq_len, LANES), F32),
            pltpu.VMEM((seq_len, LANES), F32),
            pltpu.VMEM((seq_len, LANES), F32),
            pltpu.VMEM((seq_len, LANES), F32),
            pltpu.VMEM((2, SSD_STATE, gw), F32),
            pltpu.VMEM((seq_len, gw), F32),
            pltpu.VMEM((SSD_GROUPS, seq_len, gw), F32),
            pltpu.VMEM((seq_len, LANES), F32),
        ],
        compiler_params=_cparams(2),
        name="ssd_lat" if has_h0 else "ssd_ctx",
    )(*args)


def _scft_kernel(*refs, n_alias, n_grp):
    b_ref, c_ref, x_ref, f_ref, cw_ref, cl_ref, sl_ref, cc_ref, sc_ref = refs[:9]
    pos = 9 + n_alias
    ysc_ref, yft_ref = refs[pos], refs[pos + 1]
    v = c_ref[...] * x_ref[...]
    ysc_ref[...] = (b_ref[...] * _dwconv_rows(v, cw_ref[...], (SC_CONV - 1) // 2)).astype(BF16)
    tw = FT_GROUP_DIM
    for gg in range(n_grp):
        cols = slice(gg * tw, (gg + 1) * tw)
        u = f_ref[:, cols].astype(BF16)
        p = jnp.dot(u, cc_ref[...], preferred_element_type=F32).astype(BF16)
        s = jnp.dot(u, sc_ref[...], preferred_element_type=F32).astype(BF16)
        yft_ref[:, cols] = (jnp.dot(cl_ref[...], p, preferred_element_type=F32)
                            - jnp.dot(sl_ref[...], s, preferred_element_type=F32)).astype(BF16)


def _dft_mats(n):
    k = np.arange(n)
    ang = 2.0 * np.pi * ((k[:, None] * k[None, :]) % n) / n
    scale = 1.0 / math.sqrt(n)
    return jnp.asarray(np.cos(ang) * scale, BF16), jnp.asarray(np.sin(ang) * scale, BF16)


def _scft(tail, sc_conv_w, l, *, row0, n_seq, seq_len, n_grp, prev=None):
    n_tok = tail.shape[0]
    tw = n_grp * FT_GROUP_DIM
    assert row0 % seq_len == 0 and SC_WIDTH % tw == 0
    blk0 = row0 // seq_len
    nt = SC_WIDTH // tw
    cl, sl = _dft_mats(seq_len)
    cc, sc = _dft_mats(FT_GROUP_DIM)
    in_specs = [
        pl.BlockSpec((seq_len, tw), lambda b, j: (blk0 + b, j)),
        pl.BlockSpec((seq_len, tw), lambda b, j: (blk0 + b, nt + j)),
        pl.BlockSpec((seq_len, tw), lambda b, j: (blk0 + b, 2 * nt + j)),
        pl.BlockSpec((seq_len, tw), lambda b, j: (blk0 + b, 3 * nt + j)),
        pl.BlockSpec((None, SC_CONV, tw), lambda b, j: (l, 0, j)),
        pl.BlockSpec((seq_len, seq_len), lambda b, j: (0, 0)),
        pl.BlockSpec((seq_len, seq_len), lambda b, j: (0, 0)),
        pl.BlockSpec((FT_GROUP_DIM, FT_GROUP_DIM), lambda b, j: (0, 0)),
        pl.BlockSpec((FT_GROUP_DIM, FT_GROUP_DIM), lambda b, j: (0, 0)),
    ]
    args = [tail, tail, tail, tail, sc_conv_w, cl, sl, cc, sc]
    aliases = {}
    if prev is not None:
        in_specs += [pl.BlockSpec(memory_space=pl.ANY), pl.BlockSpec(memory_space=pl.ANY)]
        aliases = {len(args): 0, len(args) + 1: 1}
        args += list(prev)
    kern = functools.partial(_scft_kernel, n_alias=len(aliases), n_grp=n_grp)
    return pl.pallas_call(
        kern,
        grid=(n_seq, nt),
        in_specs=in_specs,
        out_specs=[
            pl.BlockSpec((seq_len, tw), lambda b, j: (blk0 + b, j)),
            pl.BlockSpec((seq_len, tw), lambda b, j: (blk0 + b, j)),
        ],
        out_shape=[
            jax.ShapeDtypeStruct((n_tok, SC_WIDTH), BF16),
            jax.ShapeDtypeStruct((n_tok, FT_WIDTH), BF16),
        ],
        input_output_aliases=aliases,
        compiler_params=_cparams(2),
        name="scft_lat" if prev is not None else "scft_ctx",
    )(*args)


def _mixout_kernel(a0_ref, a1_ref, a2_ref, g0_ref, g1_ref, g2_ref, w0_ref, w1_ref, w2_ref, wo_ref,
                   x_ref, mod_ref, ng_ref, o_ref, *, nk):
    k = pl.program_id(1)

    @pl.when(k == 0)
    def _():
        o_ref[...] = jnp.zeros_like(o_ref)

    m = jax.nn.sigmoid(g0_ref[...]) * jnp.dot(a0_ref[...], w0_ref[...].astype(BF16),
                                               preferred_element_type=F32)
    m += jax.nn.sigmoid(g1_ref[...]) * jnp.dot(a1_ref[...], w1_ref[...].astype(BF16),
                                                preferred_element_type=F32)
    m += jax.nn.sigmoid(g2_ref[...]) * jnp.dot(a2_ref[...], w2_ref[...].astype(BF16),
                                                preferred_element_type=F32)
    o_ref[...] += jnp.dot(m.astype(BF16), wo_ref[...].astype(BF16), preferred_element_type=F32)

    @pl.when(k == nk - 1)
    def _():
        y = _rms(o_ref[...], ng_ref[3:4, :])
        o_ref[...] = x_ref[...] + mod_ref[5:6, :] * y


def _mixout(yssd, ysc, yft, tail, w_br_ssd, w_br_sc, w_br_ft, w_out, x, mod4, norm_g, l, rows,
            tm=1024, tk=256):
    n_tok, d = x.shape
    nk = d // tk
    gate0 = (3 * SC_WIDTH + FT_WIDTH) // tk
    per = d // tk
    grp = lambda i: rows.group_of_tile(i, tm)
    kern = functools.partial(_mixout_kernel, nk=nk)
    return pl.pallas_call(
        kern,
        grid=(n_tok // tm, nk),
        in_specs=[
            _single((tm, SSD_INNER), lambda i, k: (i, 0)),
            _single((tm, SC_WIDTH), lambda i, k: (i, 0)),
            _single((tm, FT_WIDTH), lambda i, k: (i, 0)),
            pl.BlockSpec((tm, tk), lambda i, k: (i, gate0 + k)),
            pl.BlockSpec((tm, tk), lambda i, k: (i, gate0 + per + k)),
            pl.BlockSpec((tm, tk), lambda i, k: (i, gate0 + 2 * per + k)),
            pl.BlockSpec((None, SSD_INNER, tk), lambda i, k: (l, 0, k)),
            pl.BlockSpec((None, SC_WIDTH, tk), lambda i, k: (l, 0, k)),
            pl.BlockSpec((None, FT_WIDTH, tk), lambda i, k: (l, 0, k)),
            pl.BlockSpec((None, tk, d), lambda i, k: (l, k, 0)),
            _single((tm, d), lambda i, k: (i, 0)),
            pl.BlockSpec((None, None, N_MOD, d), lambda i, k: (l, grp(i), 0, 0)),
            pl.BlockSpec((None, 6, d), lambda i, k: (l, 0, 0)),
        ],
        out_specs=_single((tm, d), lambda i, k: (i, 0)),
        out_shape=jax.ShapeDtypeStruct((n_tok, d), F32),
        compiler_params=_cparams(2),
        name="mixout",
    )(yssd, ysc, yft, tail, tail, tail, w_br_ssd, w_br_sc, w_br_ft, w_out, x, mod4, norm_g)


def _grid_pos_emb(n_tok):
    rows = n_tok // GRID_W
    t = np.arange(rows * GRID_W)
    r = (t // GRID_W).astype(np.float32)[:, None]
    col = (t % GRID_W).astype(np.float32)[:, None]
    nf = D_MODEL // 4
    omega = (1.0 / (np.float32(POS_BASE) ** (np.arange(nf, dtype=np.float32) / np.float32(nf)))).astype(np.float32)
    ro = (r * omega).astype(np.float32).astype(np.float64)
    co = (col * omega).astype(np.float32).astype(np.float64)
    return np.concatenate([np.sin(ro), np.cos(ro), np.sin(co), np.cos(co)], axis=-1).astype(np.float32)


def _group_lanes(p, hpg=HEADS_PER_GROUP):
    depth = p.shape[0]
    t = p.reshape(depth, 2, SSD_GROUPS, hpg).transpose(0, 2, 1, 3).reshape(depth, SSD_GROUPS, 1, 2 * hpg)
    return jnp.pad(t, ((0, 0), (0, 0), (0, 0), (0, LANES - 2 * hpg)))


def kernel(x_prompt, x_sample, state_ssd, c, c_ctx, ada_w, ada_b, norm_g, ffn1_wgu, ffn1_wd, w_in,
           ssd_conv_w, ssd_conv_b, ssd_dt_bias, ssd_a_log, ssd_d, ssd_norm_g, sc_conv_w,
           w_br_ssd, w_br_sc, w_br_ft, w_out, ffn2_wgu, ffn2_wd):
    n_ctx, ctx_len, d = x_prompt.shape
    n_lat, lat_len, _ = x_sample.shape
    depth = ada_w.shape[0]
    rows = _Rows(n_ctx * ctx_len, n_lat, lat_len)

    c8 = jnp.concatenate([c_ctx[None, :], c, jnp.zeros((8 - 1 - n_lat, d), F32)], axis=0)
    mod4 = _modulation(c8, ada_w, ada_b)[:, :1 + n_lat].reshape(depth, 1 + n_lat, N_MOD, d)

    x = jnp.concatenate([x_prompt.reshape(n_ctx * ctx_len, d),
                         (x_sample + jnp.asarray(_grid_pos_emb(lat_len))[None]).reshape(n_lat * lat_len, d)],
                        axis=0)

    consts = _ssd_constants()
    conv_b3 = ssd_conv_b.reshape(depth, 1, SSD_XBC)
    dtb = jnp.pad(ssd_dt_bias.reshape(depth, 1, DT_WIDTH), ((0, 0), (0, 0), (0, LANES - DT_WIDTH)))
    alog_g = _group_lanes(ssd_a_log)
    dsk_g = jnp.repeat(ssd_d, SSD_HEAD_DIM, axis=-1).reshape(depth, SSD_GROUPS, 1, GROUP_WIDTH)
    ssd_norm_g3 = ssd_norm_g.reshape(depth, 1, SSD_INNER)
    state_shape = (n_ctx, depth, 2, SSD_HEADS, SSD_HEAD_DIM, SSD_STATE)

    states = None
    for l in range(depth):
        x, u = _ffn(x, mod4, norm_g, ffn1_wgu, ffn1_wd, l, 0, rows)
        head, dt, tail = _inproj(u, w_in, l)
        ssd_args = (head, dt, ssd_conv_w, conv_b3, dtb, alog_g, dsk_g, ssd_norm_g3, consts, l)
        yssd, states = _ssd(*ssd_args, row0=0, n_seq=n_ctx, seq_len=ctx_len,
                            prev_state=states, state_shape=state_shape)
        (yssd,) = _ssd(*ssd_args, row0=rows.n_ctx_tok, n_seq=n_lat, seq_len=lat_len,
                       h0=state_ssd, prev_y=yssd)
        ysc, yft = _scft(tail, sc_conv_w, l, row0=0, n_seq=n_ctx, seq_len=ctx_len, n_grp=FT_GROUPS)
        ysc, yft = _scft(tail, sc_conv_w, l, row0=rows.n_ctx_tok, n_seq=n_lat, seq_len=lat_len,
                         n_grp=1, prev=(ysc, yft))
        x = _mixout(yssd, ysc, yft, tail, w_br_ssd, w_br_sc, w_br_ft, w_out, x, mod4, norm_g, l, rows)
        (x,) = _ffn(x, mod4, norm_g, ffn2_wgu, ffn2_wd, l, 1, rows)

    y_prompt = x[:rows.n_ctx_tok].reshape(n_ctx, ctx_len, d)
    y_sample = x[rows.n_ctx_tok:].reshape(n_lat, lat_len, d)
    return (y_prompt, y_sample, states)
```

```python
import functools
import math

import numpy as np
import jax
import jax.numpy as jnp
from jax import lax
from jax.experimental import pallas as pl
from jax.experimental.pallas import tpu as pltpu

F32 = jnp.float32
BF16 = jnp.bfloat16

D_MODEL = 2048
DEPTH = 4
GRID_W = 64
POS_BASE = 10000.0
NORM_EPS = 1e-6
SSD_HEADS = 32
SSD_HEAD_DIM = 64
SSD_INNER = SSD_HEADS * SSD_HEAD_DIM
SSD_GROUPS = 4
SSD_STATE = 128
SSD_CONV = 4
SSD_CHUNK = 128
SSD_XBC = SSD_INNER + 2 * SSD_GROUPS * SSD_STATE
SC_WIDTH = 1024
SC_CONV = 3
FT_WIDTH = 1024
FT_GROUPS = 4
FT_GROUP_DIM = FT_WIDTH // FT_GROUPS
D_FF = 5504
N_BRANCH = 3
N_MOD = 9
HEADS_PER_GROUP = SSD_HEADS // SSD_GROUPS
GROUP_WIDTH = HEADS_PER_GROUP * SSD_HEAD_DIM
HEAD_COLS = SSD_INNER + SSD_XBC
DT_COL = HEAD_COLS
DT_WIDTH = 2 * SSD_HEADS
TAIL_COL = HEAD_COLS + DT_WIDTH
TAIL_COLS = 3 * SC_WIDTH + FT_WIDTH + N_BRANCH * D_MODEL
IN_COLS = TAIL_COL + TAIL_COLS

LANES = 128
BF16_ROWS = 16
VMEM_LIMIT_BYTES = 58 * 1024 * 1024


def _cparams(n_axes):
    return pltpu.CompilerParams(dimension_semantics=("arbitrary",) * n_axes,
                                vmem_limit_bytes=VMEM_LIMIT_BYTES)


def _silu(x):
    return x * jax.nn.sigmoid(x)


def _rms(x, g):
    ms = jnp.mean(x * x, axis=-1, keepdims=True)
    return x * lax.rsqrt(ms + NORM_EPS) * g


def _norm_mod(x, g, shift, scale):
    return _rms(x, g) * (1.0 + scale) + shift


def _split_bf16(x, n):
    parts = []
    r = x
    for i in range(n):
        p = r.astype(BF16)
        parts.append(p)
        if i + 1 < n:
            r = r - p.astype(F32)
    return parts


def _dot_r01(x, m01, n=3):
    acc = None
    for p in _split_bf16(x, n):
        t = jnp.dot(p, m01, preferred_element_type=F32)
        acc = t if acc is None else acc + t
    return acc


def _dot_l01(m01, x, n=3):
    acc = None
    for p in _split_bf16(x, n):
        t = jnp.dot(m01, p, preferred_element_type=F32)
        acc = t if acc is None else acc + t
    return acc


def _mod_kernel(c_ref, w_ref, b_ref, o_ref):
    s = _silu(c_ref[...]).astype(BF16)
    o_ref[...] = jnp.dot(s, w_ref[...].astype(BF16), preferred_element_type=F32) + b_ref[...]


def _modulation(c8, ada_w, ada_b, tn=1024):
    depth, d, n = ada_w.shape
    return pl.pallas_call(
        _mod_kernel,
        grid=(depth, n // tn),
        in_specs=[
            pl.BlockSpec((8, d), lambda l, j: (0, 0)),
            pl.BlockSpec((None, d, tn), lambda l, j: (l, 0, j)),
            pl.BlockSpec((None, 1, tn), lambda l, j: (l, 0, j)),
        ],
        out_specs=pl.BlockSpec((None, 8, tn), lambda l, j: (l, 0, j)),
        out_shape=jax.ShapeDtypeStruct((depth, 8, n), F32),
        compiler_params=_cparams(2),
        name="adaln_mod",
    )(c8, ada_w, ada_b.reshape(depth, 1, n))


class _Rows:
    def __init__(self, n_ctx_tok, n_lat, lat_len):
        self.n_ctx_tok = n_ctx_tok
        self.n_lat = n_lat
        self.lat_len = lat_len
        self.n_tok = n_ctx_tok + n_lat * lat_len

    def group_of_tile(self, i, tm):
        assert self.n_ctx_tok % tm == 0 and self.lat_len % tm == 0
        n_ctx_tiles = self.n_ctx_tok // tm
        per_lat = self.lat_len // tm
        return jnp.where(i < n_ctx_tiles, 0, 1 + (i - n_ctx_tiles) // per_lat)


def _single(block_shape, index_map):
    return pl.BlockSpec(block_shape, index_map, pipeline_mode=pl.Buffered(1))


FFN_TF = 256


def _ffn_kernel(*refs, nk, mod_row, ng_row, emit_next):
    x_ref, mod_ref, ng_ref, wg_ref, wu_ref, wd_ref, o_ref = refs[:7]
    pos = 7
    if emit_next:
        un_ref = refs[pos]
        pos += 1
    u_sc, wup_sc, wdn_sc = refs[pos:pos + 3]
    k = pl.program_id(1)
    tf = FFN_TF

    @pl.when(k == 0)
    def _():
        u = _norm_mod(x_ref[...], ng_ref[ng_row:ng_row + 1, :],
                      mod_ref[mod_row:mod_row + 1, :], mod_ref[mod_row + 1:mod_row + 2, :])
        u_sc[...] = u.astype(BF16)
        o_ref[...] = jnp.zeros_like(o_ref)

    wup_sc[:, :tf] = wg_ref[...].astype(BF16)
    wup_sc[:, tf:] = wu_ref[...].astype(BF16)
    wdn_sc[...] = wd_ref[...].astype(BF16)

    h = jnp.dot(u_sc[...], wup_sc[...], preferred_element_type=F32)
    a = _silu(h[:, :tf]) * h[:, tf:]
    overlap = nk * tf - D_FF
    lane = lax.broadcasted_iota(jnp.int32, a.shape, 1)
    a = jnp.where(jnp.logical_and(k == nk - 1, lane < overlap), 0.0, a)
    o_ref[...] += jnp.dot(a.astype(BF16), wdn_sc[...], preferred_element_type=F32)

    @pl.when(k == nk - 1)
    def _():
        y = _rms(o_ref[...], ng_ref[ng_row + 1:ng_row + 2, :])
        x_new = x_ref[...] + 0.5 * mod_ref[mod_row + 2:mod_row + 3, :] * y
        o_ref[...] = x_new
        if emit_next:
            un = _norm_mod(x_new, ng_ref[ng_row + 2:ng_row + 3, :],
                           mod_ref[mod_row + 3:mod_row + 4, :], mod_ref[mod_row + 4:mod_row + 5, :])
            un_ref[...] = un.astype(BF16)


def _ffn(x, mod4, norm_g, wgu, wd, l, which, rows, tm=1024):
    n_tok, d = x.shape
    tf = FFN_TF
    nk = pl.cdiv(D_FF, tf)
    mod_row = 0 if which == 0 else 6
    ng_row = 0 if which == 0 else 4
    emit_next = which == 0
    grp = lambda i: rows.group_of_tile(i, tm)
    assert tf % LANES == 0 and D_FF % LANES == 0
    col = lambda k, base=0: LANES * (base // LANES + jnp.minimum(k * (tf // LANES), (D_FF - tf) // LANES))
    kern = functools.partial(_ffn_kernel, nk=nk, mod_row=mod_row, ng_row=ng_row, emit_next=emit_next)
    out_specs = [_single((tm, d), lambda i, k: (i, 0))]
    out_shape = [jax.ShapeDtypeStruct((n_tok, d), F32)]
    if emit_next:
        out_specs.append(pl.BlockSpec((tm, d), lambda i, k: (i, 0)))
        out_shape.append(jax.ShapeDtypeStruct((n_tok, d), BF16))
    return pl.pallas_call(
        kern,
        grid=(n_tok // tm, nk),
        in_specs=[
            _single((tm, d), lambda i, k: (i, 0)),
            pl.BlockSpec((None, None, N_MOD, d), lambda i, k: (l, grp(i), 0, 0)),
            pl.BlockSpec((None, 6, d), lambda i, k: (l, 0, 0)),
            pl.BlockSpec((pl.squeezed, pl.Element(d), pl.Element(tf)), lambda i, k: (l, 0, col(k))),
            pl.BlockSpec((pl.squeezed, pl.Element(d), pl.Element(tf)), lambda i, k: (l, 0, col(k, D_FF))),
            pl.BlockSpec((pl.squeezed, pl.Element(tf), pl.Element(d)), lambda i, k: (l, col(k), 0)),
        ],
        out_specs=out_specs,
        out_shape=out_shape,
        scratch_shapes=[
            pltpu.VMEM((tm, d), BF16),
            pltpu.VMEM((d, 2 * tf), BF16),
            pltpu.VMEM((tf, d), BF16),
        ],
        compiler_params=_cparams(2),
        name="ffn",
    )(x, mod4, norm_g, wgu, wgu, wd)


def _dot_nt(a, b):
    return lax.dot_general(a, b, (((1,), (1,)), ((), ())), preferred_element_type=F32)


def _inproj_kernel(u_ref, wa_ref, wb_ref, oh_ref, odt_ref, ot_ref, w_sc, *, n_head, tm):
    j = pl.program_id(0)
    m = pl.program_id(1)
    tn = w_sc.shape[0]
    off = DT_WIDTH

    @pl.when(jnp.logical_and(m == 0, j < n_head))
    def _():
        w_sc[...] = wa_ref[...].astype(BF16)

    @pl.when(jnp.logical_and(m == 0, j == n_head))
    def _():
        w_sc[:off, :] = wb_ref[...].astype(BF16)

    @pl.when(jnp.logical_and(m == 0, j > n_head))
    def _():
        w_sc[:tn - off, :] = wa_ref[off:, :].astype(BF16)
        w_sc[tn - off:, :] = wb_ref[...].astype(BF16)

    u = u_ref[pl.ds(pl.multiple_of(m * tm, tm), tm), :]

    @pl.when(j < n_head)
    def _():
        oh_ref[...] = _dot_nt(u, w_sc[...])

    @pl.when(j == n_head)
    def _():
        odt_ref[...] = _dot_nt(u, w_sc[:LANES, :])

    @pl.when(j > n_head)
    def _():
        ot_ref[...] = _dot_nt(u, w_sc[...])


def _inproj(u, w_in_t, l, tm=1024, tn=512):
    n_tok, d = u.shape
    off = DT_WIDTH
    assert TAIL_COL % tn == off and HEAD_COLS % tn == 0 and TAIL_COLS % tn == 0 and tn % off == 0
    n_head = HEAD_COLS // tn
    n_tail = TAIL_COLS // tn
    n_m = n_tok // tm
    sub = tn // off
    head_j = lambda j: jnp.minimum(j, n_head - 1)
    tail_j = lambda j: jnp.clip(j - n_head - 1, 0, n_tail - 1)
    head_m = lambda j, m: jnp.where(j < n_head, m, n_m - 1)
    dt_m = lambda j, m: jnp.where(j < n_head, 0, jnp.where(j == n_head, m, n_m - 1))
    tail_m = lambda j, m: jnp.where(j > n_head, m, 0)
    wide_j = lambda j: jnp.where(j <= n_head, jnp.minimum(j, n_head), j - 1)
    narrow_j = lambda j: jnp.where(j <= n_head, DT_COL // off, j * sub)
    kern = functools.partial(_inproj_kernel, n_head=n_head, tm=tm)
    return pl.pallas_call(
        kern,
        grid=(n_head + 1 + n_tail, n_m),
        in_specs=[
            _single((n_tok, d), lambda j, m: (0, 0)),
            pl.BlockSpec((None, tn, d), lambda j, m: (l, wide_j(j), 0)),
            pl.BlockSpec((None, off, d), lambda j, m: (l, narrow_j(j), 0)),
        ],
        out_specs=[
            pl.BlockSpec((tm, tn), lambda j, m: (head_m(j, m), head_j(j))),
            pl.BlockSpec((tm, LANES), lambda j, m: (dt_m(j, m), 0)),
            pl.BlockSpec((tm, tn), lambda j, m: (tail_m(j, m), tail_j(j))),
        ],
        out_shape=[
            jax.ShapeDtypeStruct((n_tok, HEAD_COLS), F32),
            jax.ShapeDtypeStruct((n_tok, LANES), F32),
            jax.ShapeDtypeStruct((n_tok, TAIL_COLS), F32),
        ],
        scratch_shapes=[pltpu.VMEM((tn, d), BF16)],
        compiler_params=_cparams(2),
        name="inproj",
    )(u, w_in_t, w_in_t)


def _dwconv_rows(x, w, left):
    n_rows = x.shape[0]
    row = lax.broadcasted_iota(jnp.int32, x.shape, 0)
    out = None
    for k in range(w.shape[0]):
        off = k - left
        if off == 0:
            term = x
        else:
            shifted = pltpu.roll(x, (-off) % n_rows, axis=0)
            valid = jnp.logical_and(row + off >= 0, row + off < n_rows)
            term = jnp.where(valid, shifted, 0.0)
        term = term * w[k:k + 1, :]
        out = term if out is None else out + term
    return out


def _ssd_kernel(*refs, seq_len, has_h0, n_alias, emit_state):
    (z_ref, xr_ref, br_ref, cr_ref, dt_ref, cwx_ref, cwb_ref, cwc_ref, cbx_ref, cbb_ref, cbc_ref,
     dtb_ref, alog_ref, dsk_ref, gn_ref, sel_ref, e2_ref, tri_ref, trit_ref) = refs[:19]
    pos = 19
    h0_ref = None
    if has_h0:
        h0_ref = refs[pos]
        pos += 1
    pos += n_alias
    yn_ref = refs[pos]
    pos += 1
    hfin_ref = None
    if emit_state:
        hfin_ref = refs[pos]
        pos += 1
    xa_sc, ba_sc, ca_sc, dts_sc, a_sc, ht_sc, yg_sc, y_sc, ssq_sc = refs[pos:pos + 9]

    gi = pl.program_id(1)
    q = SSD_CHUNK
    nc = seq_len // q
    left = (SSD_CONV - 1) // 2
    hpg = HEADS_PER_GROUP
    gw = GROUP_WIDTH

    xa = _silu(_dwconv_rows(xr_ref[...], cwx_ref[...], left) + cbx_ref[...])
    xa_sc[...] = xa
    ba_sc[...] = _silu(_dwconv_rows(br_ref[...], cwb_ref[...], left) + cbb_ref[...])
    ca_sc[...] = _silu(_dwconv_rows(cr_ref[...], cwc_ref[...], left) + cbc_ref[...])
    yg_sc[...] = xa * dsk_ref[...]

    draw = dt_ref[...] + dtb_ref[...]
    dt_all = jnp.maximum(draw, 0.0) + jnp.log1p(jnp.exp(-jnp.abs(draw)))
    dts = _dot_r01(dt_all, sel_ref[...])
    dts_sc[...] = dts
    a_sc[...] = dts * (-jnp.exp(alog_ref[...]))

    for d in range(2):
        if has_h0:
            for p in range(hpg // 2):
                blk = h0_ref[d, 2 * p:2 * p + 2].reshape(2 * SSD_HEAD_DIM, SSD_STATE)
                ht_sc[d, :, p * LANES:(p + 1) * LANES] = blk.T
        else:
            ht_sc[d] = jnp.zeros((SSD_STATE, gw), F32)

    ri = lax.broadcasted_iota(jnp.int32, (q, q), 0)
    ci = lax.broadcasted_iota(jnp.int32, (q, q), 1)
    lane_lo = lax.broadcasted_iota(jnp.int32, (q, LANES), 1) < SSD_HEAD_DIM
    neg_inf = jnp.float32(-jnp.inf)

    def chunk_dir(c, d):
        rows = pl.ds(pl.multiple_of(c * q, q), q)
        a_c = a_sc[rows, :]
        if d == 0:
            cs = _dot_l01(tri_ref[...], a_c)
            tot = cs[q - 1:q, :]
            mask = ri >= ci
        else:
            cs = _dot_l01(trit_ref[...], a_c)
            tot = cs[0:1, :]
            mask = ri <= ci
        cst = cs.T
        stack = jnp.concatenate([
            dts_sc[rows, :].astype(BF16),
            jnp.exp(cs).astype(BF16),
            jnp.exp(tot - cs).astype(BF16),
            jnp.broadcast_to(jnp.exp(tot), (BF16_ROWS, LANES)).astype(BF16)], axis=0)
        ex = jnp.dot(stack, e2_ref[:, d * gw:(d + 1) * gw], preferred_element_type=F32)
        dtx, ecs_x, dte_x, ea_x = ex[0:q], ex[q:2 * q], ex[2 * q:3 * q], ex[3 * q:3 * q + 1]
        b_c = ba_sc[rows, :]
        c_bf = ca_sc[rows, :].astype(BF16)
        xdt = xa_sc[rows, :] * dtx
        g = lax.dot_general(c_bf, b_c.astype(BF16), (((1,), (1,)), ((), ())),
                            preferred_element_type=F32)
        ht = ht_sc[d]
        y_off = jnp.dot(c_bf, ht.astype(BF16), preferred_element_type=F32) * ecs_x
        pieces = []
        for p in range(hpg // 2):
            ms = []
            for jj in range(2):
                r = d * hpg + 2 * p + jj
                diff = cs[:, r:r + 1] - cst[r:r + 1, :]
                ms.append((g * jnp.exp(jnp.where(mask, diff, neg_inf))).astype(BF16))
            xp = xdt[:, p * LANES:(p + 1) * LANES]
            rhs = jnp.concatenate([jnp.where(lane_lo, xp, 0.0), jnp.where(lane_lo, 0.0, xp)], axis=0)
            pieces.append(jnp.dot(jnp.concatenate(ms, axis=1), rhs.astype(BF16),
                                  preferred_element_type=F32))
        s_t = jnp.dot(b_c.T.astype(BF16), (xdt * dte_x).astype(BF16), preferred_element_type=F32)
        ht_sc[d] = ht * ea_x + s_t
        yg_sc[rows, :] += jnp.concatenate(pieces, axis=1) + y_off

    def body(c, carry):
        chunk_dir(c, 0)
        chunk_dir(nc - 1 - c, 1)
        return carry

    lax.fori_loop(0, nc, body, 0, unroll=2)

    yz = yg_sc[...] * _silu(z_ref[...])
    y_sc[gi] = yz
    part = jnp.broadcast_to(jnp.sum(yz * yz, axis=-1, keepdims=True), ssq_sc.shape)

    @pl.when(gi == 0)
    def _():
        ssq_sc[...] = part

    @pl.when(gi > 0)
    def _():
        ssq_sc[...] += part

    @pl.when(gi == SSD_GROUPS - 1)
    def _():
        inv = lax.rsqrt(ssq_sc[:, 0:1] * (1.0 / SSD_INNER) + NORM_EPS)
        for gg in range(SSD_GROUPS):
            cols = slice(gg * gw, (gg + 1) * gw)
            yn_ref[:, cols] = (y_sc[gg] * inv * gn_ref[:, cols]).astype(BF16)

    if emit_state:
        for d in range(2):
            for p in range(hpg // 2):
                blk = ht_sc[d, :, p * LANES:(p + 1) * LANES].T
                hfin_ref[d, 2 * p:2 * p + 2] = blk.reshape(2, SSD_HEAD_DIM, SSD_STATE)


def _ssd_constants():
    hpg = HEADS_PER_GROUP
    sel = np.zeros((SSD_GROUPS, LANES, LANES), np.float32)
    for g in range(SSD_GROUPS):
        for d in range(2):
            for j in range(hpg):
                sel[g, d * SSD_HEADS + hpg * g + j, d * hpg + j] = 1.0
    e2 = np.zeros((LANES, 2 * GROUP_WIDTH), np.float32)
    for d in range(2):
        for j in range(hpg):
            lo = d * GROUP_WIDTH + j * SSD_HEAD_DIM
            e2[d * hpg + j, lo:lo + SSD_HEAD_DIM] = 1.0
    tri = np.tril(np.ones((SSD_CHUNK, SSD_CHUNK), np.float32))
    return (jnp.asarray(sel, BF16), jnp.asarray(e2, BF16), jnp.asarray(tri, BF16),
            jnp.asarray(tri.T, BF16))


def _ssd(head, dt, conv_w, conv_b3, dtb, alog_g, dsk_g, norm_g3, consts, l, *, row0, n_seq, seq_len,
         h0=None, prev_y=None, prev_state=None, state_shape=None):
    n_tok = head.shape[0]
    sel, e2, tri, trit = consts
    assert row0 % seq_len == 0 and seq_len % (2 * SSD_CHUNK) == 0
    blk0 = row0 // seq_len
    gw = GROUP_WIDTH
    has_h0 = h0 is not None
    emit_state = state_shape is not None
    xcol = SSD_INNER // gw
    bcol = (SSD_INNER + SSD_INNER) // LANES
    ccol = bcol + SSD_GROUPS
    cwb = SSD_INNER // LANES
    cwc = cwb + SSD_GROUPS
    in_specs = [
        pl.BlockSpec((seq_len, gw), lambda b, g: (blk0 + b, g)),
        pl.BlockSpec((seq_len, gw), lambda b, g: (blk0 + b, xcol + g)),
        pl.BlockSpec((seq_len, LANES), lambda b, g: (blk0 + b, bcol + g)),
        pl.BlockSpec((seq_len, LANES), lambda b, g: (blk0 + b, ccol + g)),
        pl.BlockSpec((seq_len, LANES), lambda b, g: (blk0 + b, 0)),
        pl.BlockSpec((None, SSD_CONV, gw), lambda b, g: (l, 0, g)),
        pl.BlockSpec((None, SSD_CONV, LANES), lambda b, g: (l, 0, cwb + g)),
        pl.BlockSpec((None, SSD_CONV, LANES), lambda b, g: (l, 0, cwc + g)),
        pl.BlockSpec((None, 1, gw), lambda b, g: (l, 0, g)),
        pl.BlockSpec((None, 1, LANES), lambda b, g: (l, 0, cwb + g)),
        pl.BlockSpec((None, 1, LANES), lambda b, g: (l, 0, cwc + g)),
        pl.BlockSpec((None, 1, LANES), lambda b, g: (l, 0, 0)),
        pl.BlockSpec((None, None, 1, LANES), lambda b, g: (l, g, 0, 0)),
        pl.BlockSpec((None, None, 1, gw), lambda b, g: (l, g, 0, 0)),
        pl.BlockSpec((None, 1, SSD_INNER), lambda b, g: (l, 0, 0)),
        pl.BlockSpec((None, LANES, LANES), lambda b, g: (g, 0, 0)),
        pl.BlockSpec((LANES, 2 * gw), lambda b, g: (0, 0)),
        pl.BlockSpec((SSD_CHUNK, SSD_CHUNK), lambda b, g: (0, 0)),
        pl.BlockSpec((SSD_CHUNK, SSD_CHUNK), lambda b, g: (0, 0)),
    ]
    args = [head, head, head, head, dt, conv_w, conv_w, conv_w, conv_b3, conv_b3, conv_b3,
            dtb, alog_g, dsk_g, norm_g3, sel, e2, tri, trit]
    if has_h0:
        in_specs.append(pl.BlockSpec((None, None, 2, HEADS_PER_GROUP, SSD_HEAD_DIM, SSD_STATE),
                                     lambda b, g: (b, l, 0, g, 0, 0)))
        args.append(h0)
    aliases = {}
    if prev_y is not None:
        in_specs.append(pl.BlockSpec(memory_space=pl.ANY))
        aliases[len(args)] = 0
        args.append(prev_y)
    if prev_state is not None:
        assert emit_state
        in_specs.append(pl.BlockSpec(memory_space=pl.ANY))
        aliases[len(args)] = 1
        args.append(prev_state)
    out_specs = [pl.BlockSpec((seq_len, SSD_INNER), lambda b, g: (blk0 + b, 0))]
    out_shape = [jax.ShapeDtypeStruct((n_tok, SSD_INNER), BF16)]
    if emit_state:
        out_specs.append(pl.BlockSpec((None, None, 2, HEADS_PER_GROUP, SSD_HEAD_DIM, SSD_STATE),
                                      lambda b, g: (b, l, 0, g, 0, 0)))
        out_shape.append(jax.ShapeDtypeStruct(state_shape, F32))
    kern = functools.partial(_ssd_kernel, seq_len=seq_len, has_h0=has_h0, n_alias=len(aliases),
                             emit_state=emit_state)
    return pl.pallas_call(
        kern,
        grid=(n_seq, SSD_GROUPS),
        in_specs=in_specs,
        out_specs=out_specs,
        out_shape=out_shape,
        input_output_aliases=aliases,
        scratch_shapes=[
            pltpu.VMEM((seq_len, gw), F32),
            pltpu.VMEM((seq_len, LANES), F32),
            pltpu.VMEM((seq_len, LANES), F32),
            pltpu.VMEM((seq_len, LANES), F32),
            pltpu.VMEM((seq_len, LANES), F32),
            pltpu.VMEM((2, SSD_STATE, gw), F32),
            pltpu.VMEM((seq_len, gw), F32),
            pltpu.VMEM((SSD_GROUPS, seq_len, gw), F32),
            pltpu.VMEM((seq_len, LANES), F32),
        ],
        compiler_params=_cparams(2),
        name="ssd_lat" if has_h0 else "ssd_ctx",
    )(*args)


def _scft_kernel(*refs, n_alias, n_grp):
    b_ref, c_ref, x_ref, f_ref, cw_ref, cl_ref, sl_ref, cc_ref, sc_ref = refs[:9]
    pos = 9 + n_alias
    ysc_ref, yft_ref = refs[pos], refs[pos + 1]
    v = c_ref[...] * x_ref[...]
    ysc_ref[...] = (b_ref[...] * _dwconv_rows(v, cw_ref[...], (SC_CONV - 1) // 2)).astype(BF16)
    tw = FT_GROUP_DIM
    for gg in range(n_grp):
        cols = slice(gg * tw, (gg + 1) * tw)
        u = f_ref[:, cols].astype(BF16)
        p = jnp.dot(u, cc_ref[...], preferred_element_type=F32).astype(BF16)
        s = jnp.dot(u, sc_ref[...], preferred_element_type=F32).astype(BF16)
        yft_ref[:, cols] = (jnp.dot(cl_ref[...], p, preferred_element_type=F32)
                            - jnp.dot(sl_ref[...], s, preferred_element_type=F32)).astype(BF16)


def _dft_mats(n):
    k = np.arange(n)
    ang = 2.0 * np.pi * ((k[:, None] * k[None, :]) % n) / n
    scale = 1.0 / math.sqrt(n)
    return jnp.asarray(np.cos(ang) * scale, BF16), jnp.asarray(np.sin(ang) * scale, BF16)


def _scft(tail, sc_conv_w, l, *, row0, n_seq, seq_len, n_grp, prev=None):
    n_tok = tail.shape[0]
    tw = n_grp * FT_GROUP_DIM
    assert row0 % seq_len == 0 and SC_WIDTH % tw == 0
    blk0 = row0 // seq_len
    nt = SC_WIDTH // tw
    cl, sl = _dft_mats(seq_len)
    cc, sc = _dft_mats(FT_GROUP_DIM)
    in_specs = [
        pl.BlockSpec((seq_len, tw), lambda b, j: (blk0 + b, j)),
        pl.BlockSpec((seq_len, tw), lambda b, j: (blk0 + b, nt + j)),
        pl.BlockSpec((seq_len, tw), lambda b, j: (blk0 + b, 2 * nt + j)),
        pl.BlockSpec((seq_len, tw), lambda b, j: (blk0 + b, 3 * nt + j)),
        pl.BlockSpec((None, SC_CONV, tw), lambda b, j: (l, 0, j)),
        pl.BlockSpec((seq_len, seq_len), lambda b, j: (0, 0)),
        pl.BlockSpec((seq_len, seq_len), lambda b, j: (0, 0)),
        pl.BlockSpec((FT_GROUP_DIM, FT_GROUP_DIM), lambda b, j: (0, 0)),
        pl.BlockSpec((FT_GROUP_DIM, FT_GROUP_DIM), lambda b, j: (0, 0)),
    ]
    args = [tail, tail, tail, tail, sc_conv_w, cl, sl, cc, sc]
    aliases = {}
    if prev is not None:
        in_specs += [pl.BlockSpec(memory_space=pl.ANY), pl.BlockSpec(memory_space=pl.ANY)]
        aliases = {len(args): 0, len(args) + 1: 1}
        args += list(prev)
    kern = functools.partial(_scft_kernel, n_alias=len(aliases), n_grp=n_grp)
    return pl.pallas_call(
        kern,
        grid=(n_seq, nt),
        in_specs=in_specs,
        out_specs=[
            pl.BlockSpec((seq_len, tw), lambda b, j: (blk0 + b, j)),
            pl.BlockSpec((seq_len, tw), lambda b, j: (blk0 + b, j)),
        ],
        out_shape=[
            jax.ShapeDtypeStruct((n_tok, SC_WIDTH), BF16),
            jax.ShapeDtypeStruct((n_tok, FT_WIDTH), BF16),
        ],
        input_output_aliases=aliases,
        compiler_params=_cparams(2),
        name="scft_lat" if prev is not None else "scft_ctx",
    )(*args)


def _mixout_kernel(a0_ref, a1_ref, a2_ref, g0_ref, g1_ref, g2_ref, w0_ref, w1_ref, w2_ref, wo_ref,
                   x_ref, mod_ref, ng_ref, o_ref, wb_sc, wo_sc, *, nk):
    i = pl.program_id(0)
    k = pl.program_id(1)
    k0, k1 = SSD_INNER, SSD_INNER + SC_WIDTH

    @pl.when(i == 0)
    def _():
        wb_sc[k, :k0, :] = w0_ref[...].astype(BF16)
        wb_sc[k, k0:k1, :] = w1_ref[...].astype(BF16)
        wb_sc[k, k1:, :] = w2_ref[...].astype(BF16)
        wo_sc[k] = wo_ref[...].astype(BF16)

    @pl.when(k == 0)
    def _():
        o_ref[...] = jnp.zeros_like(o_ref)

    m = jax.nn.sigmoid(g0_ref[...]) * jnp.dot(a0_ref[...], wb_sc[k, :k0, :], preferred_element_type=F32)
    m += jax.nn.sigmoid(g1_ref[...]) * jnp.dot(a1_ref[...], wb_sc[k, k0:k1, :], preferred_element_type=F32)
    m += jax.nn.sigmoid(g2_ref[...]) * jnp.dot(a2_ref[...], wb_sc[k, k1:, :], preferred_element_type=F32)
    o_ref[...] += jnp.dot(m.astype(BF16), wo_sc[k], preferred_element_type=F32)

    @pl.when(k == nk - 1)
    def _():
        y = _rms(o_ref[...], ng_ref[3:4, :])
        o_ref[...] = x_ref[...] + mod_ref[5:6, :] * y


def _mixout(yssd, ysc, yft, tail, w_br_ssd, w_br_sc, w_br_ft, w_out, x, mod4, norm_g, l, rows,
            tm=512, tk=256):
    n_tok, d = x.shape
    nk = d // tk
    gate0 = (3 * SC_WIDTH + FT_WIDTH) // tk
    per = d // tk
    grp = lambda i: rows.group_of_tile(i, tm)
    wk = lambda i, k: jnp.where(i == 0, k, nk - 1)
    kern = functools.partial(_mixout_kernel, nk=nk)
    return pl.pallas_call(
        kern,
        grid=(n_tok // tm, nk),
        in_specs=[
            _single((tm, SSD_INNER), lambda i, k: (i, 0)),
            _single((tm, SC_WIDTH), lambda i, k: (i, 0)),
            _single((tm, FT_WIDTH), lambda i, k: (i, 0)),
            pl.BlockSpec((tm, tk), lambda i, k: (i, gate0 + k)),
            pl.BlockSpec((tm, tk), lambda i, k: (i, gate0 + per + k)),
            pl.BlockSpec((tm, tk), lambda i, k: (i, gate0 + 2 * per + k)),
            pl.BlockSpec((None, SSD_INNER, tk), lambda i, k: (l, 0, wk(i, k))),
            pl.BlockSpec((None, SC_WIDTH, tk), lambda i, k: (l, 0, wk(i, k))),
            pl.BlockSpec((None, FT_WIDTH, tk), lambda i, k: (l, 0, wk(i, k))),
            pl.BlockSpec((None, tk, d), lambda i, k: (l, wk(i, k), 0)),
            _single((tm, d), lambda i, k: (i, 0)),
            pl.BlockSpec((None, None, N_MOD, d), lambda i, k: (l, grp(i), 0, 0)),
            pl.BlockSpec((None, 6, d), lambda i, k: (l, 0, 0)),
        ],
        out_specs=_single((tm, d), lambda i, k: (i, 0)),
        out_shape=jax.ShapeDtypeStruct((n_tok, d), F32),
        scratch_shapes=[
            pltpu.VMEM((nk, SSD_INNER + SC_WIDTH + FT_WIDTH, tk), BF16),
            pltpu.VMEM((nk, tk, d), BF16),
        ],
        compiler_params=_cparams(2),
        name="mixout",
    )(yssd, ysc, yft, tail, tail, tail, w_br_ssd, w_br_sc, w_br_ft, w_out, x, mod4, norm_g)


def _grid_pos_emb(n_tok):
    rows = n_tok // GRID_W
    t = np.arange(rows * GRID_W)
    r = (t // GRID_W).astype(np.float32)[:, None]
    col = (t % GRID_W).astype(np.float32)[:, None]
    nf = D_MODEL // 4
    omega = (1.0 / (np.float32(POS_BASE) ** (np.arange(nf, dtype=np.float32) / np.float32(nf)))).astype(np.float32)
    ro = (r * omega).astype(np.float32).astype(np.float64)
    co = (col * omega).astype(np.float32).astype(np.float64)
    return np.concatenate([np.sin(ro), np.cos(ro), np.sin(co), np.cos(co)], axis=-1).astype(np.float32)


def _group_lanes(p, hpg=HEADS_PER_GROUP):
    depth = p.shape[0]
    t = p.reshape(depth, 2, SSD_GROUPS, hpg).transpose(0, 2, 1, 3).reshape(depth, SSD_GROUPS, 1, 2 * hpg)
    return jnp.pad(t, ((0, 0), (0, 0), (0, 0), (0, LANES - 2 * hpg)))


def kernel(x_prompt, x_sample, state_ssd, c, c_ctx, ada_w, ada_b, norm_g, ffn1_wgu, ffn1_wd, w_in,
           ssd_conv_w, ssd_conv_b, ssd_dt_bias, ssd_a_log, ssd_d, ssd_norm_g, sc_conv_w,
           w_br_ssd, w_br_sc, w_br_ft, w_out, ffn2_wgu, ffn2_wd):
    n_ctx, ctx_len, d = x_prompt.shape
    n_lat, lat_len, _ = x_sample.shape
    depth = ada_w.shape[0]
    rows = _Rows(n_ctx * ctx_len, n_lat, lat_len)

    c8 = jnp.concatenate([c_ctx[None, :], c, jnp.zeros((8 - 1 - n_lat, d), F32)], axis=0)
    mod4 = _modulation(c8, ada_w, ada_b)[:, :1 + n_lat].reshape(depth, 1 + n_lat, N_MOD, d)

    x = jnp.concatenate([x_prompt.reshape(n_ctx * ctx_len, d),
                         (x_sample + jnp.asarray(_grid_pos_emb(lat_len))[None]).reshape(n_lat * lat_len, d)],
                        axis=0)

    consts = _ssd_constants()
    w_in_t = jnp.swapaxes(w_in, 1, 2)
    conv_b3 = ssd_conv_b.reshape(depth, 1, SSD_XBC)
    dtb = jnp.pad(ssd_dt_bias.reshape(depth, 1, DT_WIDTH), ((0, 0), (0, 0), (0, LANES - DT_WIDTH)))
    alog_g = _group_lanes(ssd_a_log)
    dsk_g = jnp.repeat(ssd_d, SSD_HEAD_DIM, axis=-1).reshape(depth, SSD_GROUPS, 1, GROUP_WIDTH)
    ssd_norm_g3 = ssd_norm_g.reshape(depth, 1, SSD_INNER)
    state_shape = (n_ctx, depth, 2, SSD_HEADS, SSD_HEAD_DIM, SSD_STATE)

    states = None
    for l in range(depth):
        x, u = _ffn(x, mod4, norm_g, ffn1_wgu, ffn1_wd, l, 0, rows)
        head, dt, tail = _inproj(u, w_in_t, l)
        ssd_args = (head, dt, ssd_conv_w, conv_b3, dtb, alog_g, dsk_g, ssd_norm_g3, consts, l)
        yssd, states = _ssd(*ssd_args, row0=0, n_seq=n_ctx, seq_len=ctx_len,
                            prev_state=states, state_shape=state_shape)
        (yssd,) = _ssd(*ssd_args, row0=rows.n_ctx_tok, n_seq=n_lat, seq_len=lat_len,
                       h0=state_ssd, prev_y=yssd)
        ysc, yft = _scft(tail, sc_conv_w, l, row0=0, n_seq=n_ctx, seq_len=ctx_len, n_grp=FT_GROUPS)
        ysc, yft = _scft(tail, sc_conv_w, l, row0=rows.n_ctx_tok, n_seq=n_lat, seq_len=lat_len,
                         n_grp=1, prev=(ysc, yft))
        x = _mixout(yssd, ysc, yft, tail, w_br_ssd, w_br_sc, w_br_ft, w_out, x, mod4, norm_g, l, rows)
        (x,) = _ffn(x, mod4, norm_g, ffn2_wgu, ffn2_wd, l, 1, rows)

    y_prompt = x[:rows.n_ctx_tok].reshape(n_ctx, ctx_len, d)
    y_sample = x[rows.n_ctx_tok:].reshape(n_lat, lat_len, d)
    return (y_prompt, y_sample, states)
```

```python
import functools
import math

import numpy as np
import jax
import jax.numpy as jnp
from jax import lax
from jax.experimental import pallas as pl
from jax.experimental.pallas import tpu as pltpu

F32 = jnp.float32
BF16 = jnp.bfloat16

D_MODEL = 2048
DEPTH = 4
GRID_W = 64
POS_BASE = 10000.0
NORM_EPS = 1e-6
SSD_HEADS = 32
SSD_HEAD_DIM = 64
SSD_INNER = SSD_HEADS * SSD_HEAD_DIM
SSD_GROUPS = 4
SSD_STATE = 128
SSD_CONV = 4
SSD_CHUNK = 128
SSD_XBC = SSD_INNER + 2 * SSD_GROUPS * SSD_STATE
SC_WIDTH = 1024
SC_CONV = 3
FT_WIDTH = 1024
FT_GROUPS = 4
FT_GROUP_DIM = FT_WIDTH // FT_GROUPS
D_FF = 5504
N_BRANCH = 3
N_MOD = 9
HEADS_PER_GROUP = SSD_HEADS // SSD_GROUPS
GROUP_WIDTH = HEADS_PER_GROUP * SSD_HEAD_DIM
HEAD_COLS = SSD_INNER + SSD_XBC
DT_COL = HEAD_COLS
DT_WIDTH = 2 * SSD_HEADS
TAIL_COL = HEAD_COLS + DT_WIDTH
TAIL_COLS = 3 * SC_WIDTH + FT_WIDTH + N_BRANCH * D_MODEL
IN_COLS = TAIL_COL + TAIL_COLS

LANES = 128
BF16_ROWS = 16
VMEM_LIMIT_BYTES = 58 * 1024 * 1024


def _cparams(n_axes):
    return pltpu.CompilerParams(dimension_semantics=("arbitrary",) * n_axes,
                                vmem_limit_bytes=VMEM_LIMIT_BYTES)


def _silu(x):
    return x * jax.nn.sigmoid(x)


def _rms(x, g):
    ms = jnp.mean(x * x, axis=-1, keepdims=True)
    return x * lax.rsqrt(ms + NORM_EPS) * g


def _norm_mod(x, g, shift, scale):
    return _rms(x, g) * (1.0 + scale) + shift


def _split_bf16(x, n):
    parts = []
    r = x
    for i in range(n):
        p = r.astype(BF16)
        parts.append(p)
        if i + 1 < n:
            r = r - p.astype(F32)
    return parts


def _dot_r01(x, m01, n=3):
    acc = None
    for p in _split_bf16(x, n):
        t = jnp.dot(p, m01, preferred_element_type=F32)
        acc = t if acc is None else acc + t
    return acc


def _dot_l01(m01, x, n=3):
    acc = None
    for p in _split_bf16(x, n):
        t = jnp.dot(m01, p, preferred_element_type=F32)
        acc = t if acc is None else acc + t
    return acc


def _mod_kernel(c_ref, w_ref, b_ref, o_ref):
    s = _silu(c_ref[...]).astype(BF16)
    o_ref[...] = jnp.dot(s, w_ref[...].astype(BF16), preferred_element_type=F32) + b_ref[...]


def _modulation(c8, ada_w, ada_b, tn=1024):
    depth, d, n = ada_w.shape
    return pl.pallas_call(
        _mod_kernel,
        grid=(depth, n // tn),
        in_specs=[
            pl.BlockSpec((8, d), lambda l, j: (0, 0)),
            pl.BlockSpec((None, d, tn), lambda l, j: (l, 0, j)),
            pl.BlockSpec((None, 1, tn), lambda l, j: (l, 0, j)),
        ],
        out_specs=pl.BlockSpec((None, 8, tn), lambda l, j: (l, 0, j)),
        out_shape=jax.ShapeDtypeStruct((depth, 8, n), F32),
        compiler_params=_cparams(2),
        name="adaln_mod",
    )(c8, ada_w, ada_b.reshape(depth, 1, n))


class _Rows:
    def __init__(self, n_ctx_tok, n_lat, lat_len):
        self.n_ctx_tok = n_ctx_tok
        self.n_lat = n_lat
        self.lat_len = lat_len
        self.n_tok = n_ctx_tok + n_lat * lat_len

    def group_of_tile(self, i, tm):
        assert self.n_ctx_tok % tm == 0 and self.lat_len % tm == 0
        n_ctx_tiles = self.n_ctx_tok // tm
        per_lat = self.lat_len // tm
        return jnp.where(i < n_ctx_tiles, 0, 1 + (i - n_ctx_tiles) // per_lat)


def _single(block_shape, index_map):
    return pl.BlockSpec(block_shape, index_map, pipeline_mode=pl.Buffered(1))


FFN_TF = 256


def _ffn_kernel(*refs, nk, mod_row, ng_row, next_kind):
    x_ref, u_ref, mod_ref, ng_ref, wg_ref, wu_ref, wd_ref = refs[:7]
    pos = 7
    if next_kind == "layer":
        modn_ref, ngn_ref = refs[pos:pos + 2]
        pos += 2
    o_ref = refs[pos]
    pos += 1
    if next_kind is not None:
        un_ref = refs[pos]
        pos += 1
    wup_sc, wdn_sc = refs[pos:pos + 2]
    k = pl.program_id(1)
    tf = FFN_TF

    @pl.when(k == 0)
    def _():
        o_ref[...] = jnp.zeros_like(o_ref)

    wup_sc[:, :tf] = wg_ref[...].astype(BF16)
    wup_sc[:, tf:] = wu_ref[...].astype(BF16)
    wdn_sc[...] = wd_ref[...].astype(BF16)

    h = jnp.dot(u_ref[...], wup_sc[...], preferred_element_type=F32)
    a = _silu(h[:, :tf]) * h[:, tf:]
    overlap = nk * tf - D_FF
    lane = lax.broadcasted_iota(jnp.int32, a.shape, 1)
    a = jnp.where(jnp.logical_and(k == nk - 1, lane < overlap), 0.0, a)
    o_ref[...] += jnp.dot(a.astype(BF16), wdn_sc[...], preferred_element_type=F32)

    @pl.when(k == nk - 1)
    def _():
        y = _rms(o_ref[...], ng_ref[ng_row + 1:ng_row + 2, :])
        x_new = x_ref[...] + 0.5 * mod_ref[mod_row + 2:mod_row + 3, :] * y
        o_ref[...] = x_new
        if next_kind == "same":
            un = _norm_mod(x_new, ng_ref[ng_row + 2:ng_row + 3, :],
                           mod_ref[mod_row + 3:mod_row + 4, :], mod_ref[mod_row + 4:mod_row + 5, :])
            un_ref[...] = un.astype(BF16)
        elif next_kind == "layer":
            un = _norm_mod(x_new, ngn_ref[0:1, :], modn_ref[0:1, :], modn_ref[1:2, :])
            un_ref[...] = un.astype(BF16)


def _ffn(x, u, mod4, norm_g, wgu, wd, l, which, rows, next_kind, tm=1024):
    n_tok, d = x.shape
    tf = FFN_TF
    nk = pl.cdiv(D_FF, tf)
    mod_row = 0 if which == 0 else 6
    ng_row = 0 if which == 0 else 4
    grp = lambda i: rows.group_of_tile(i, tm)
    assert tf % LANES == 0 and D_FF % LANES == 0
    col = lambda k, base=0: LANES * (base // LANES + jnp.minimum(k * (tf // LANES), (D_FF - tf) // LANES))
    kern = functools.partial(_ffn_kernel, nk=nk, mod_row=mod_row, ng_row=ng_row, next_kind=next_kind)
    in_specs = [
        _single((tm, d), lambda i, k: (i, 0)),
        pl.BlockSpec((tm, d), lambda i, k: (i, 0)),
        pl.BlockSpec((None, None, N_MOD, d), lambda i, k: (l, grp(i), 0, 0)),
        pl.BlockSpec((None, 6, d), lambda i, k: (l, 0, 0)),
        pl.BlockSpec((pl.squeezed, pl.Element(d), pl.Element(tf)), lambda i, k: (l, 0, col(k))),
        pl.BlockSpec((pl.squeezed, pl.Element(d), pl.Element(tf)), lambda i, k: (l, 0, col(k, D_FF))),
        pl.BlockSpec((pl.squeezed, pl.Element(tf), pl.Element(d)), lambda i, k: (l, col(k), 0)),
    ]
    args = [x, u, mod4, norm_g, wgu, wgu, wd]
    if next_kind == "layer":
        in_specs += [pl.BlockSpec((None, None, N_MOD, d), lambda i, k: (l + 1, grp(i), 0, 0)),
                     pl.BlockSpec((None, 6, d), lambda i, k: (l + 1, 0, 0))]
        args += [mod4, norm_g]
    out_specs = [_single((tm, d), lambda i, k: (i, 0))]
    out_shape = [jax.ShapeDtypeStruct((n_tok, d), F32)]
    if next_kind is not None:
        out_specs.append(_single((tm, d), lambda i, k: (i, 0)))
        out_shape.append(jax.ShapeDtypeStruct((n_tok, d), BF16))
    return pl.pallas_call(
        kern,
        grid=(n_tok // tm, nk),
        in_specs=in_specs,
        out_specs=out_specs,
        out_shape=out_shape,
        scratch_shapes=[
            pltpu.VMEM((d, 2 * tf), BF16),
            pltpu.VMEM((tf, d), BF16),
        ],
        compiler_params=_cparams(2),
        name="ffn",
    )(*args)


def _embed_kernel(xp_ref, xs_ref, pe_ref, mod_ref, ng_ref, x_ref, u_ref, *, n_ctx_tiles):
    i = pl.program_id(0)

    @pl.when(i < n_ctx_tiles)
    def _():
        x_ref[...] = xp_ref[...]

    @pl.when(i >= n_ctx_tiles)
    def _():
        x_ref[...] = xs_ref[...] + pe_ref[...]

    u_ref[...] = _norm_mod(x_ref[...], ng_ref[0:1, :], mod_ref[0:1, :], mod_ref[1:2, :]).astype(BF16)


def _embed(xp, xs, pe, mod4, norm_g, rows, tm=256):
    d = xp.shape[1]
    assert rows.lat_len % tm == 0 and rows.n_ctx_tok % tm == 0
    n_ctx_tiles = rows.n_ctx_tok // tm
    pe_tiles = rows.lat_len // tm
    grp = lambda i: rows.group_of_tile(i, tm)
    kern = functools.partial(_embed_kernel, n_ctx_tiles=n_ctx_tiles)
    return pl.pallas_call(
        kern,
        grid=(rows.n_tok // tm,),
        in_specs=[
            pl.BlockSpec((tm, d), lambda i: (jnp.minimum(i, n_ctx_tiles - 1), 0)),
            pl.BlockSpec((tm, d), lambda i: (jnp.maximum(i - n_ctx_tiles, 0), 0)),
            pl.BlockSpec((tm, d), lambda i: (jnp.maximum(i - n_ctx_tiles, 0) % pe_tiles, 0)),
            pl.BlockSpec((None, None, N_MOD, d), lambda i: (0, grp(i), 0, 0)),
            pl.BlockSpec((None, 6, d), lambda i: (0, 0, 0)),
        ],
        out_specs=[pl.BlockSpec((tm, d), lambda i: (i, 0)), pl.BlockSpec((tm, d), lambda i: (i, 0))],
        out_shape=[jax.ShapeDtypeStruct((rows.n_tok, d), F32), jax.ShapeDtypeStruct((rows.n_tok, d), BF16)],
        compiler_params=_cparams(1),
        name="embed",
    )(xp, xs, pe, mod4, norm_g)


def _dot_nt(a, b):
    return lax.dot_general(a, b, (((1,), (1,)), ((), ())), preferred_element_type=F32)


def _inproj_kernel(u_ref, wa_ref, wb_ref, op_ref, odt_ref, w_sc, *, n_head, tm):
    j = pl.program_id(0)
    m = pl.program_id(1)
    tn = w_sc.shape[0]
    off = DT_WIDTH

    @pl.when(jnp.logical_and(m == 0, j < n_head))
    def _():
        w_sc[...] = wa_ref[...].astype(BF16)

    @pl.when(jnp.logical_and(m == 0, j == n_head))
    def _():
        w_sc[:off, :] = wb_ref[...].astype(BF16)

    @pl.when(jnp.logical_and(m == 0, j > n_head))
    def _():
        w_sc[:tn - off, :] = wa_ref[off:, :].astype(BF16)
        w_sc[tn - off:, :] = wb_ref[...].astype(BF16)

    u = u_ref[pl.ds(pl.multiple_of(m * tm, tm), tm), :]

    @pl.when(j != n_head)
    def _():
        op_ref[...] = _dot_nt(u, w_sc[...])

    @pl.when(j == n_head)
    def _():
        odt_ref[...] = _dot_nt(u, w_sc[:LANES, :])


def _inproj(u, w_in_t, l, tm=2048, tn=512):
    n_tok, d = u.shape
    off = DT_WIDTH
    assert TAIL_COL % tn == off and HEAD_COLS % tn == 0 and TAIL_COLS % tn == 0 and tn % off == 0
    n_head = HEAD_COLS // tn
    n_tail = TAIL_COLS // tn
    n_m = n_tok // tm
    sub = tn // off
    proj_j = lambda j: jnp.where(j < n_head, j, jnp.where(j == n_head, n_head - 1, j - 1))
    proj_m = lambda j, m: jnp.where(j == n_head, n_m - 1, m)
    dt_m = lambda j, m: jnp.where(j < n_head, 0, jnp.where(j == n_head, m, n_m - 1))
    wide_j = lambda j: jnp.where(j <= n_head, jnp.minimum(j, n_head), j - 1)
    narrow_j = lambda j: jnp.where(j <= n_head, DT_COL // off, j * sub)
    kern = functools.partial(_inproj_kernel, n_head=n_head, tm=tm)
    return pl.pallas_call(
        kern,
        grid=(n_head + 1 + n_tail, n_m),
        in_specs=[
            _single((n_tok, d), lambda j, m: (0, 0)),
            pl.BlockSpec((None, tn, d), lambda j, m: (l, wide_j(j), 0)),
            pl.BlockSpec((None, off, d), lambda j, m: (l, narrow_j(j), 0)),
        ],
        out_specs=[
            pl.BlockSpec((tm, tn), lambda j, m: (proj_m(j, m), proj_j(j))),
            pl.BlockSpec((tm, LANES), lambda j, m: (dt_m(j, m), 0)),
        ],
        out_shape=[
            jax.ShapeDtypeStruct((n_tok, HEAD_COLS + TAIL_COLS), F32),
            jax.ShapeDtypeStruct((n_tok, LANES), F32),
        ],
        scratch_shapes=[pltpu.VMEM((tn, d), BF16)],
        compiler_params=_cparams(2),
        name="inproj",
    )(u, w_in_t, w_in_t)


def _dwconv_rows(x, w, left):
    n_rows = x.shape[0]
    row = lax.broadcasted_iota(jnp.int32, x.shape, 0)
    out = None
    for k in range(w.shape[0]):
        off = k - left
        if off == 0:
            term = x
        else:
            shifted = pltpu.roll(x, (-off) % n_rows, axis=0)
            valid = jnp.logical_and(row + off >= 0, row + off < n_rows)
            term = jnp.where(valid, shifted, 0.0)
        term = term * w[k:k + 1, :]
        out = term if out is None else out + term
    return out


def _ssd_kernel(*refs, seq_len, has_h0, n_alias, emit_state):
    (z_ref, xr_ref, br_ref, cr_ref, dt_ref, cwx_ref, cwb_ref, cwc_ref, cbx_ref, cbb_ref, cbc_ref,
     dtb_ref, alog_ref, dsk_ref, gn_ref, sel_ref, e2_ref, tri_ref, trit_ref) = refs[:19]
    pos = 19
    h0_ref = None
    if has_h0:
        h0_ref = refs[pos]
        pos += 1
    pos += n_alias
    yn_ref = refs[pos]
    pos += 1
    hfin_ref = None
    if emit_state:
        hfin_ref = refs[pos]
        pos += 1
    xa_sc, ba_sc, ca_sc, dts_sc, a_sc, ht_sc, yg_sc, y_sc, ssq_sc = refs[pos:pos + 9]

    gi = pl.program_id(1)
    q = SSD_CHUNK
    nc = seq_len // q
    left = (SSD_CONV - 1) // 2
    hpg = HEADS_PER_GROUP
    gw = GROUP_WIDTH

    xa = _silu(_dwconv_rows(xr_ref[...], cwx_ref[...], left) + cbx_ref[...])
    xa_sc[...] = xa
    ba_sc[...] = _silu(_dwconv_rows(br_ref[...], cwb_ref[...], left) + cbb_ref[...])
    ca_sc[...] = _silu(_dwconv_rows(cr_ref[...], cwc_ref[...], left) + cbc_ref[...])
    yg_sc[...] = xa * dsk_ref[...]

    draw = dt_ref[...] + dtb_ref[...]
    dt_all = jnp.maximum(draw, 0.0) + jnp.log1p(jnp.exp(-jnp.abs(draw)))
    dts = _dot_r01(dt_all, sel_ref[...])
    dts_sc[...] = dts
    a_sc[...] = dts * (-jnp.exp(alog_ref[...]))

    for d in range(2):
        if has_h0:
            for p in range(hpg // 2):
                blk = h0_ref[d, 2 * p:2 * p + 2].reshape(2 * SSD_HEAD_DIM, SSD_STATE)
                ht_sc[d, :, p * LANES:(p + 1) * LANES] = blk.T
        else:
            ht_sc[d] = jnp.zeros((SSD_STATE, gw), F32)

    ri = lax.broadcasted_iota(jnp.int32, (q, q), 0)
    ci = lax.broadcasted_iota(jnp.int32, (q, q), 1)
    lane_lo = lax.broadcasted_iota(jnp.int32, (q, LANES), 1) < SSD_HEAD_DIM
    neg_inf = jnp.float32(-jnp.inf)

    def chunk_dir(c, d):
        rows = pl.ds(pl.multiple_of(c * q, q), q)
        a_c = a_sc[rows, :]
        if d == 0:
            cs = _dot_l01(tri_ref[...], a_c)
            tot = cs[q - 1:q, :]
            mask = ri >= ci
        else:
            cs = _dot_l01(trit_ref[...], a_c)
            tot = cs[0:1, :]
            mask = ri <= ci
        cst = cs.T
        stack = jnp.concatenate([
            dts_sc[rows, :].astype(BF16),
            jnp.exp(cs).astype(BF16),
            jnp.exp(tot - cs).astype(BF16),
            jnp.broadcast_to(jnp.exp(tot), (BF16_ROWS, LANES)).astype(BF16)], axis=0)
        ex = jnp.dot(stack, e2_ref[:, d * gw:(d + 1) * gw], preferred_element_type=F32)
        dtx, ecs_x, dte_x, ea_x = ex[0:q], ex[q:2 * q], ex[2 * q:3 * q], ex[3 * q:3 * q + 1]
        b_c = ba_sc[rows, :]
        c_bf = ca_sc[rows, :].astype(BF16)
        xdt = xa_sc[rows, :] * dtx
        g = lax.dot_general(c_bf, b_c.astype(BF16), (((1,), (1,)), ((), ())),
                            preferred_element_type=F32)
        ht = ht_sc[d]
        y_off = jnp.dot(c_bf, ht.astype(BF16), preferred_element_type=F32) * ecs_x
        pieces = []
        for p in range(hpg // 2):
            ms = []
            for jj in range(2):
                r = d * hpg + 2 * p + jj
                diff = cs[:, r:r + 1] - cst[r:r + 1, :]
                ms.append((g * jnp.exp(jnp.where(mask, diff, neg_inf))).astype(BF16))
            xp = xdt[:, p * LANES:(p + 1) * LANES]
            rhs = jnp.concatenate([jnp.where(lane_lo, xp, 0.0), jnp.where(lane_lo, 0.0, xp)], axis=0)
            pieces.append(jnp.dot(jnp.concatenate(ms, axis=1), rhs.astype(BF16),
                                  preferred_element_type=F32))
        s_t = jnp.dot(b_c.T.astype(BF16), (xdt * dte_x).astype(BF16), preferred_element_type=F32)
        ht_sc[d] = ht * ea_x + s_t
        yg_sc[rows, :] += jnp.concatenate(pieces, axis=1) + y_off

    def body(c, carry):
        chunk_dir(c, 0)
        chunk_dir(nc - 1 - c, 1)
        return carry

    lax.fori_loop(0, nc, body, 0, unroll=2)

    yz = yg_sc[...] * _silu(z_ref[...])
    y_sc[gi] = yz
    part = jnp.broadcast_to(jnp.sum(yz * yz, axis=-1, keepdims=True), ssq_sc.shape)

    @pl.when(gi == 0)
    def _():
        ssq_sc[...] = part

    @pl.when(gi > 0)
    def _():
        ssq_sc[...] += part

    @pl.when(gi == SSD_GROUPS - 1)
    def _():
        inv = lax.rsqrt(ssq_sc[:, 0:1] * (1.0 / SSD_INNER) + NORM_EPS)
        for gg in range(SSD_GROUPS):
            cols = slice(gg * gw, (gg + 1) * gw)
            yn_ref[:, cols] = (y_sc[gg] * inv * gn_ref[:, cols]).astype(BF16)

    if emit_state:
        for d in range(2):
            for p in range(hpg // 2):
                blk = ht_sc[d, :, p * LANES:(p + 1) * LANES].T
                hfin_ref[d, 2 * p:2 * p + 2] = blk.reshape(2, SSD_HEAD_DIM, SSD_STATE)


def _ssd_constants():
    hpg = HEADS_PER_GROUP
    sel = np.zeros((SSD_GROUPS, LANES, LANES), np.float32)
    for g in range(SSD_GROUPS):
        for d in range(2):
            for j in range(hpg):
                sel[g, d * SSD_HEADS + hpg * g + j, d * hpg + j] = 1.0
    e2 = np.zeros((LANES, 2 * GROUP_WIDTH), np.float32)
    for d in range(2):
        for j in range(hpg):
            lo = d * GROUP_WIDTH + j * SSD_HEAD_DIM
            e2[d * hpg + j, lo:lo + SSD_HEAD_DIM] = 1.0
    tri = np.tril(np.ones((SSD_CHUNK, SSD_CHUNK), np.float32))
    return (jnp.asarray(sel, BF16), jnp.asarray(e2, BF16), jnp.asarray(tri, BF16),
            jnp.asarray(tri.T, BF16))


def _ssd(head, dt, conv_w, conv_b3, dtb, alog_g, dsk_g, norm_g3, consts, l, *, row0, n_seq, seq_len,
         h0=None, prev_y=None, prev_state=None, state_shape=None):
    n_tok = head.shape[0]
    sel, e2, tri, trit = consts
    assert row0 % seq_len == 0 and seq_len % (2 * SSD_CHUNK) == 0
    blk0 = row0 // seq_len
    gw = GROUP_WIDTH
    has_h0 = h0 is not None
    emit_state = state_shape is not None
    xcol = SSD_INNER // gw
    bcol = (SSD_INNER + SSD_INNER) // LANES
    ccol = bcol + SSD_GROUPS
    cwb = SSD_INNER // LANES
    cwc = cwb + SSD_GROUPS
    in_specs = [
        pl.BlockSpec((seq_len, gw), lambda b, g: (blk0 + b, g)),
        pl.BlockSpec((seq_len, gw), lambda b, g: (blk0 + b, xcol + g)),
        pl.BlockSpec((seq_len, LANES), lambda b, g: (blk0 + b, bcol + g)),
        pl.BlockSpec((seq_len, LANES), lambda b, g: (blk0 + b, ccol + g)),
        pl.BlockSpec((seq_len, LANES), lambda b, g: (blk0 + b, 0)),
        pl.BlockSpec((None, SSD_CONV, gw), lambda b, g: (l, 0, g)),
        pl.BlockSpec((None, SSD_CONV, LANES), lambda b, g: (l, 0, cwb + g)),
        pl.BlockSpec((None, SSD_CONV, LANES), lambda b, g: (l, 0, cwc + g)),
        pl.BlockSpec((None, 1, gw), lambda b, g: (l, 0, g)),
        pl.BlockSpec((None, 1, LANES), lambda b, g: (l, 0, cwb + g)),
        pl.BlockSpec((None, 1, LANES), lambda b, g: (l, 0, cwc + g)),
        pl.BlockSpec((None, 1, LANES), lambda b, g: (l, 0, 0)),
        pl.BlockSpec((None, None, 1, LANES), lambda b, g: (l, g, 0, 0)),
        pl.BlockSpec((None, None, 1, gw), lambda b, g: (l, g, 0, 0)),
        pl.BlockSpec((None, 1, SSD_INNER), lambda b, g: (l, 0, 0)),
        pl.BlockSpec((None, LANES, LANES), lambda b, g: (g, 0, 0)),
        pl.BlockSpec((LANES, 2 * gw), lambda b, g: (0, 0)),
        pl.BlockSpec((SSD_CHUNK, SSD_CHUNK), lambda b, g: (0, 0)),
        pl.BlockSpec((SSD_CHUNK, SSD_CHUNK), lambda b, g: (0, 0)),
    ]
    args = [head, head, head, head, dt, conv_w, conv_w, conv_w, conv_b3, conv_b3, conv_b3,
            dtb, alog_g, dsk_g, norm_g3, sel, e2, tri, trit]
    if has_h0:
        in_specs.append(pl.BlockSpec((None, None, 2, HEADS_PER_GROUP, SSD_HEAD_DIM, SSD_STATE),
                                     lambda b, g: (b, l, 0, g, 0, 0)))
        args.append(h0)
    aliases = {}
    if prev_y is not None:
        in_specs.append(pl.BlockSpec(memory_space=pl.ANY))
        aliases[len(args)] = 0
        args.append(prev_y)
    if prev_state is not None:
        assert emit_state
        in_specs.append(pl.BlockSpec(memory_space=pl.ANY))
        aliases[len(args)] = 1
        args.append(prev_state)
    out_specs = [pl.BlockSpec((seq_len, SSD_INNER), lambda b, g: (blk0 + b, 0))]
    out_shape = [jax.ShapeDtypeStruct((n_tok, SSD_INNER), BF16)]
    if emit_state:
        out_specs.append(pl.BlockSpec((None, None, 2, HEADS_PER_GROUP, SSD_HEAD_DIM, SSD_STATE),
                                      lambda b, g: (b, l, 0, g, 0, 0)))
        out_shape.append(jax.ShapeDtypeStruct(state_shape, F32))
    kern = functools.partial(_ssd_kernel, seq_len=seq_len, has_h0=has_h0, n_alias=len(aliases),
                             emit_state=emit_state)
    return pl.pallas_call(
        kern,
        grid=(n_seq, SSD_GROUPS),
        in_specs=in_specs,
        out_specs=out_specs,
        out_shape=out_shape,
        input_output_aliases=aliases,
        scratch_shapes=[
            pltpu.VMEM((seq_len, gw), F32),
            pltpu.VMEM((seq_len, LANES), F32),
            pltpu.VMEM((seq_len, LANES), F32),
            pltpu.VMEM((seq_len, LANES), F32),
            pltpu.VMEM((seq_len, LANES), F32),
            pltpu.VMEM((2, SSD_STATE, gw), F32),
            pltpu.VMEM((seq_len, gw), F32),
            pltpu.VMEM((SSD_GROUPS, seq_len, gw), F32),
            pltpu.VMEM((seq_len, LANES), F32),
        ],
        compiler_params=_cparams(2),
        name="ssd_lat" if has_h0 else "ssd_ctx",
    )(*args)


def _scft_kernel(*refs, n_alias, n_grp):
    b_ref, c_ref, x_ref, f_ref, cw_ref, cl_ref, sl_ref, cc_ref, sc_ref = refs[:9]
    pos = 9 + n_alias
    ysc_ref, yft_ref = refs[pos], refs[pos + 1]
    v = c_ref[...] * x_ref[...]
    ysc_ref[...] = (b_ref[...] * _dwconv_rows(v, cw_ref[...], (SC_CONV - 1) // 2)).astype(BF16)
    tw = FT_GROUP_DIM
    for gg in range(n_grp):
        cols = slice(gg * tw, (gg + 1) * tw)
        u = f_ref[:, cols].astype(BF16)
        p = jnp.dot(u, cc_ref[...], preferred_element_type=F32).astype(BF16)
        s = jnp.dot(u, sc_ref[...], preferred_element_type=F32).astype(BF16)
        yft_ref[:, cols] = (jnp.dot(cl_ref[...], p, preferred_element_type=F32)
                            - jnp.dot(sl_ref[...], s, preferred_element_type=F32)).astype(BF16)


def _dft_mats(n):
    k = np.arange(n)
    ang = 2.0 * np.pi * ((k[:, None] * k[None, :]) % n) / n
    scale = 1.0 / math.sqrt(n)
    return jnp.asarray(np.cos(ang) * scale, BF16), jnp.asarray(np.sin(ang) * scale, BF16)


def _scft(tail, sc_conv_w, l, *, row0, n_seq, seq_len, n_grp, prev=None):
    n_tok = tail.shape[0]
    tw = n_grp * FT_GROUP_DIM
    assert row0 % seq_len == 0 and SC_WIDTH % tw == 0 and HEAD_COLS % tw == 0
    blk0 = row0 // seq_len
    nt = SC_WIDTH // tw
    c0 = HEAD_COLS // tw
    cl, sl = _dft_mats(seq_len)
    cc, sc = _dft_mats(FT_GROUP_DIM)
    in_specs = [
        pl.BlockSpec((seq_len, tw), lambda b, j: (blk0 + b, c0 + j)),
        pl.BlockSpec((seq_len, tw), lambda b, j: (blk0 + b, c0 + nt + j)),
        pl.BlockSpec((seq_len, tw), lambda b, j: (blk0 + b, c0 + 2 * nt + j)),
        pl.BlockSpec((seq_len, tw), lambda b, j: (blk0 + b, c0 + 3 * nt + j)),
        pl.BlockSpec((None, SC_CONV, tw), lambda b, j: (l, 0, j)),
        pl.BlockSpec((seq_len, seq_len), lambda b, j: (0, 0)),
        pl.BlockSpec((seq_len, seq_len), lambda b, j: (0, 0)),
        pl.BlockSpec((FT_GROUP_DIM, FT_GROUP_DIM), lambda b, j: (0, 0)),
        pl.BlockSpec((FT_GROUP_DIM, FT_GROUP_DIM), lambda b, j: (0, 0)),
    ]
    args = [tail, tail, tail, tail, sc_conv_w, cl, sl, cc, sc]
    aliases = {}
    if prev is not None:
        in_specs += [pl.BlockSpec(memory_space=pl.ANY), pl.BlockSpec(memory_space=pl.ANY)]
        aliases = {len(args): 0, len(args) + 1: 1}
        args += list(prev)
    kern = functools.partial(_scft_kernel, n_alias=len(aliases), n_grp=n_grp)
    return pl.pallas_call(
        kern,
        grid=(n_seq, nt),
        in_specs=in_specs,
        out_specs=[
            pl.BlockSpec((seq_len, tw), lambda b, j: (blk0 + b, j)),
            pl.BlockSpec((seq_len, tw), lambda b, j: (blk0 + b, j)),
        ],
        out_shape=[
            jax.ShapeDtypeStruct((n_tok, SC_WIDTH), BF16),
            jax.ShapeDtypeStruct((n_tok, FT_WIDTH), BF16),
        ],
        input_output_aliases=aliases,
        compiler_params=_cparams(2),
        name="scft_lat" if prev is not None else "scft_ctx",
    )(*args)


def _mixout_kernel(a0_ref, a1_ref, a2_ref, g0_ref, g1_ref, g2_ref, w0_ref, w1_ref, w2_ref, wo_ref,
                   x_ref, mod_ref, ng_ref, o_ref, un_ref, *, nk):
    k = pl.program_id(1)

    @pl.when(k == 0)
    def _():
        o_ref[...] = jnp.zeros_like(o_ref)

    m = jax.nn.sigmoid(g0_ref[...]) * jnp.dot(a0_ref[...], w0_ref[...].astype(BF16),
                                               preferred_element_type=F32)
    m += jax.nn.sigmoid(g1_ref[...]) * jnp.dot(a1_ref[...], w1_ref[...].astype(BF16),
                                                preferred_element_type=F32)
    m += jax.nn.sigmoid(g2_ref[...]) * jnp.dot(a2_ref[...], w2_ref[...].astype(BF16),
                                                preferred_element_type=F32)
    o_ref[...] += jnp.dot(m.astype(BF16), wo_ref[...].astype(BF16), preferred_element_type=F32)

    @pl.when(k == nk - 1)
    def _():
        y = _rms(o_ref[...], ng_ref[3:4, :])
        x_new = x_ref[...] + mod_ref[5:6, :] * y
        o_ref[...] = x_new
        un_ref[...] = _norm_mod(x_new, ng_ref[4:5, :], mod_ref[6:7, :], mod_ref[7:8, :]).astype(BF16)


def _mixout(yssd, ysc, yft, tail, w_br_ssd, w_br_sc, w_br_ft, w_out, x, mod4, norm_g, l, rows,
            tm=1024, tk=256):
    n_tok, d = x.shape
    nk = d // tk
    gate0 = (HEAD_COLS + 3 * SC_WIDTH + FT_WIDTH) // tk
    per = d // tk
    grp = lambda i: rows.group_of_tile(i, tm)
    wk = lambda i, k: k
    kern = functools.partial(_mixout_kernel, nk=nk)
    return pl.pallas_call(
        kern,
        grid=(n_tok // tm, nk),
        in_specs=[
            _single((tm, SSD_INNER), lambda i, k: (i, 0)),
            _single((tm, SC_WIDTH), lambda i, k: (i, 0)),
            _single((tm, FT_WIDTH), lambda i, k: (i, 0)),
            pl.BlockSpec((tm, tk), lambda i, k: (i, gate0 + k)),
            pl.BlockSpec((tm, tk), lambda i, k: (i, gate0 + per + k)),
            pl.BlockSpec((tm, tk), lambda i, k: (i, gate0 + 2 * per + k)),
            pl.BlockSpec((None, SSD_INNER, tk), lambda i, k: (l, 0, wk(i, k))),
            pl.BlockSpec((None, SC_WIDTH, tk), lambda i, k: (l, 0, wk(i, k))),
            pl.BlockSpec((None, FT_WIDTH, tk), lambda i, k: (l, 0, wk(i, k))),
            pl.BlockSpec((None, tk, d), lambda i, k: (l, wk(i, k), 0)),
            _single((tm, d), lambda i, k: (i, 0)),
            pl.BlockSpec((None, None, N_MOD, d), lambda i, k: (l, grp(i), 0, 0)),
            pl.BlockSpec((None, 6, d), lambda i, k: (l, 0, 0)),
        ],
        out_specs=[_single((tm, d), lambda i, k: (i, 0)), _single((tm, d), lambda i, k: (i, 0))],
        out_shape=[jax.ShapeDtypeStruct((n_tok, d), F32), jax.ShapeDtypeStruct((n_tok, d), BF16)],
        compiler_params=_cparams(2),
        name="mixout",
    )(yssd, ysc, yft, tail, tail, tail, w_br_ssd, w_br_sc, w_br_ft, w_out, x, mod4, norm_g)


def _grid_pos_emb(n_tok):
    rows = n_tok // GRID_W
    t = np.arange(rows * GRID_W)
    r = (t // GRID_W).astype(np.float32)[:, None]
    col = (t % GRID_W).astype(np.float32)[:, None]
    nf = D_MODEL // 4
    omega = (1.0 / (np.float32(POS_BASE) ** (np.arange(nf, dtype=np.float32) / np.float32(nf)))).astype(np.float32)
    ro = (r * omega).astype(np.float32).astype(np.float64)
    co = (col * omega).astype(np.float32).astype(np.float64)
    return np.concatenate([np.sin(ro), np.cos(ro), np.sin(co), np.cos(co)], axis=-1).astype(np.float32)


def _group_lanes(p, hpg=HEADS_PER_GROUP):
    depth = p.shape[0]
    t = p.reshape(depth, 2, SSD_GROUPS, hpg).transpose(0, 2, 1, 3).reshape(depth, SSD_GROUPS, 1, 2 * hpg)
    return jnp.pad(t, ((0, 0), (0, 0), (0, 0), (0, LANES - 2 * hpg)))


def kernel(x_prompt, x_sample, state_ssd, c, c_ctx, ada_w, ada_b, norm_g, ffn1_wgu, ffn1_wd, w_in,
           ssd_conv_w, ssd_conv_b, ssd_dt_bias, ssd_a_log, ssd_d, ssd_norm_g, sc_conv_w,
           w_br_ssd, w_br_sc, w_br_ft, w_out, ffn2_wgu, ffn2_wd):
    n_ctx, ctx_len, d = x_prompt.shape
    n_lat, lat_len, _ = x_sample.shape
    depth = ada_w.shape[0]
    rows = _Rows(n_ctx * ctx_len, n_lat, lat_len)

    c8 = jnp.concatenate([c_ctx[None, :], c, jnp.zeros((8 - 1 - n_lat, d), F32)], axis=0)
    mod4 = _modulation(c8, ada_w, ada_b)[:, :1 + n_lat].reshape(depth, 1 + n_lat, N_MOD, d)

    x, u = _embed(x_prompt.reshape(n_ctx * ctx_len, d), x_sample.reshape(n_lat * lat_len, d),
                  jnp.asarray(_grid_pos_emb(lat_len)), mod4, norm_g, rows)

    consts = _ssd_constants()
    w_in_t = jnp.swapaxes(w_in, 1, 2)
    conv_b3 = ssd_conv_b.reshape(depth, 1, SSD_XBC)
    dtb = jnp.pad(ssd_dt_bias.reshape(depth, 1, DT_WIDTH), ((0, 0), (0, 0), (0, LANES - DT_WIDTH)))
    alog_g = _group_lanes(ssd_a_log)
    dsk_g = jnp.repeat(ssd_d, SSD_HEAD_DIM, axis=-1).reshape(depth, SSD_GROUPS, 1, GROUP_WIDTH)
    ssd_norm_g3 = ssd_norm_g.reshape(depth, 1, SSD_INNER)
    state_shape = (n_ctx, depth, 2, SSD_HEADS, SSD_HEAD_DIM, SSD_STATE)

    states = None
    for l in range(depth):
        x, u = _ffn(x, u, mod4, norm_g, ffn1_wgu, ffn1_wd, l, 0, rows, "same")
        proj, dt = _inproj(u, w_in_t, l)
        tail = proj
        ssd_args = (proj, dt, ssd_conv_w, conv_b3, dtb, alog_g, dsk_g, ssd_norm_g3, consts, l)
        yssd, states = _ssd(*ssd_args, row0=0, n_seq=n_ctx, seq_len=ctx_len,
                            prev_state=states, state_shape=state_shape)
        (yssd,) = _ssd(*ssd_args, row0=rows.n_ctx_tok, n_seq=n_lat, seq_len=lat_len,
                       h0=state_ssd, prev_y=yssd)
        ysc, yft = _scft(tail, sc_conv_w, l, row0=0, n_seq=n_ctx, seq_len=ctx_len, n_grp=FT_GROUPS)
        ysc, yft = _scft(tail, sc_conv_w, l, row0=rows.n_ctx_tok, n_seq=n_lat, seq_len=lat_len,
                         n_grp=1, prev=(ysc, yft))
        x, u = _mixout(yssd, ysc, yft, tail, w_br_ssd, w_br_sc, w_br_ft, w_out, x, mod4, norm_g, l, rows)
        if l + 1 < depth:
            x, u = _ffn(x, u, mod4, norm_g, ffn2_wgu, ffn2_wd, l, 1, rows, "layer")
        else:
            (x,) = _ffn(x, u, mod4, norm_g, ffn2_wgu, ffn2_wd, l, 1, rows, None)

    y_prompt = x[:rows.n_ctx_tok].reshape(n_ctx, ctx_len, d)
    y_sample = x[rows.n_ctx_tok:].reshape(n_lat, lat_len, d)
    return (y_prompt, y_sample, states)
```

```python
import functools
import math

import numpy as np
import jax
import jax.numpy as jnp
from jax import lax
from jax.experimental import pallas as pl
from jax.experimental.pallas import tpu as pltpu

F32 = jnp.float32
BF16 = jnp.bfloat16

D_MODEL = 2048
DEPTH = 4
GRID_W = 64
POS_BASE = 10000.0
NORM_EPS = 1e-6
SSD_HEADS = 32
SSD_HEAD_DIM = 64
SSD_INNER = SSD_HEADS * SSD_HEAD_DIM
SSD_GROUPS = 4
SSD_STATE = 128
SSD_CONV = 4
SSD_CHUNK = 128
SSD_XBC = SSD_INNER + 2 * SSD_GROUPS * SSD_STATE
SC_WIDTH = 1024
SC_CONV = 3
FT_WIDTH = 1024
FT_GROUPS = 4
FT_GROUP_DIM = FT_WIDTH // FT_GROUPS
D_FF = 5504
N_BRANCH = 3
N_MOD = 9
HEADS_PER_GROUP = SSD_HEADS // SSD_GROUPS
GROUP_WIDTH = HEADS_PER_GROUP * SSD_HEAD_DIM
HEAD_COLS = SSD_INNER + SSD_XBC
DT_COL = HEAD_COLS
DT_WIDTH = 2 * SSD_HEADS
TAIL_COL = HEAD_COLS + DT_WIDTH
TAIL_COLS = 3 * SC_WIDTH + FT_WIDTH + N_BRANCH * D_MODEL
IN_COLS = TAIL_COL + TAIL_COLS

LANES = 128
BF16_ROWS = 16
VMEM_LIMIT_BYTES = 60 * 1024 * 1024


def _cparams(n_axes):
    return pltpu.CompilerParams(dimension_semantics=("arbitrary",) * n_axes,
                                vmem_limit_bytes=VMEM_LIMIT_BYTES)


def _silu(x):
    return x * jax.nn.sigmoid(x)


def _rms(x, g):
    ms = jnp.mean(x * x, axis=-1, keepdims=True)
    return x * lax.rsqrt(ms + NORM_EPS) * g


def _norm_mod(x, g, shift, scale):
    return _rms(x, g) * (1.0 + scale) + shift


def _split_bf16(x, n):
    parts = []
    r = x
    for i in range(n):
        p = r.astype(BF16)
        parts.append(p)
        if i + 1 < n:
            r = r - p.astype(F32)
    return parts


def _dot_r01(x, m01, n=3):
    acc = None
    for p in _split_bf16(x, n):
        t = jnp.dot(p, m01, preferred_element_type=F32)
        acc = t if acc is None else acc + t
    return acc


def _dot_l01(m01, x, n=3):
    acc = None
    for p in _split_bf16(x, n):
        t = jnp.dot(m01, p, preferred_element_type=F32)
        acc = t if acc is None else acc + t
    return acc


def _mod_kernel(c_ref, w_ref, b_ref, o_ref):
    s = _silu(c_ref[...]).astype(BF16)
    o_ref[...] = jnp.dot(s, w_ref[...].astype(BF16), preferred_element_type=F32) + b_ref[...]


def _modulation(c8, ada_w, ada_b, tn=1024):
    depth, d, n = ada_w.shape
    return pl.pallas_call(
        _mod_kernel,
        grid=(depth, n // tn),
        in_specs=[
            pl.BlockSpec((8, d), lambda l, j: (0, 0)),
            pl.BlockSpec((None, d, tn), lambda l, j: (l, 0, j)),
            pl.BlockSpec((None, 1, tn), lambda l, j: (l, 0, j)),
        ],
        out_specs=pl.BlockSpec((None, 8, tn), lambda l, j: (l, 0, j)),
        out_shape=jax.ShapeDtypeStruct((depth, 8, n), F32),
        compiler_params=_cparams(2),
        name="adaln_mod",
    )(c8, ada_w, ada_b.reshape(depth, 1, n))


class _Rows:
    def __init__(self, n_ctx_tok, n_lat, lat_len):
        self.n_ctx_tok = n_ctx_tok
        self.n_lat = n_lat
        self.lat_len = lat_len
        self.n_tok = n_ctx_tok + n_lat * lat_len

    def group_of_tile(self, i, tm):
        assert self.n_ctx_tok % tm == 0 and self.lat_len % tm == 0
        n_ctx_tiles = self.n_ctx_tok // tm
        per_lat = self.lat_len // tm
        return jnp.where(i < n_ctx_tiles, 0, 1 + (i - n_ctx_tiles) // per_lat)


def _single(block_shape, index_map):
    return pl.BlockSpec(block_shape, index_map, pipeline_mode=pl.Buffered(1))


FFN_TF = 512
FFN_SPLIT = 2


def _ffn_kernel(*refs, nk, mod_row, ng_row, next_kind):
    x_ref, u_ref, mod_ref, ng_ref, wg_ref, wu_ref, wd_ref = refs[:7]
    pos = 7
    if next_kind == "layer":
        modn_ref, ngn_ref = refs[pos:pos + 2]
        pos += 2
    o_ref = refs[pos]
    pos += 1
    if next_kind is not None:
        un_ref = refs[pos]
        pos += 1
    wup_sc, wdn_sc = refs[pos:pos + 2]
    k = pl.program_id(1)
    tf = FFN_TF

    @pl.when(k == 0)
    def _():
        o_ref[...] = jnp.zeros_like(o_ref)

    ts = tf // FFN_SPLIT
    for s in range(FFN_SPLIT):
        wup_sc[:, 2 * s * ts:(2 * s + 1) * ts] = wg_ref[:, s * ts:(s + 1) * ts].astype(BF16)
        wup_sc[:, (2 * s + 1) * ts:(2 * s + 2) * ts] = wu_ref[:, s * ts:(s + 1) * ts].astype(BF16)
    wdn_sc[...] = wd_ref[...].astype(BF16)

    overlap = nk * tf - D_FF
    assert overlap <= ts
    for s in range(FFN_SPLIT):
        h = jnp.dot(u_ref[...], wup_sc[:, 2 * s * ts:(2 * s + 2) * ts], preferred_element_type=F32)
        a = _silu(h[:, :ts]) * h[:, ts:]
        if s == 0:
            lane = lax.broadcasted_iota(jnp.int32, a.shape, 1)
            a = jnp.where(jnp.logical_and(k == nk - 1, lane < overlap), 0.0, a)
        o_ref[...] += jnp.dot(a.astype(BF16), wdn_sc[s * ts:(s + 1) * ts, :], preferred_element_type=F32)

    @pl.when(k == nk - 1)
    def _():
        y = _rms(o_ref[...], ng_ref[ng_row + 1:ng_row + 2, :])
        x_new = x_ref[...] + 0.5 * mod_ref[mod_row + 2:mod_row + 3, :] * y
        o_ref[...] = x_new
        if next_kind == "same":
            un = _norm_mod(x_new, ng_ref[ng_row + 2:ng_row + 3, :],
                           mod_ref[mod_row + 3:mod_row + 4, :], mod_ref[mod_row + 4:mod_row + 5, :])
            un_ref[...] = un.astype(BF16)
        elif next_kind == "layer":
            un = _norm_mod(x_new, ngn_ref[0:1, :], modn_ref[0:1, :], modn_ref[1:2, :])
            un_ref[...] = un.astype(BF16)


def _ffn(x, u, mod4, norm_g, wgu, wd, l, which, rows, next_kind, tm=1024):
    n_tok, d = x.shape
    tf = FFN_TF
    nk = pl.cdiv(D_FF, tf)
    mod_row = 0 if which == 0 else 6
    ng_row = 0 if which == 0 else 4
    grp = lambda i: rows.group_of_tile(i, tm)
    assert tf % LANES == 0 and D_FF % LANES == 0
    col = lambda k, base=0: LANES * (base // LANES + jnp.minimum(k * (tf // LANES), (D_FF - tf) // LANES))
    kern = functools.partial(_ffn_kernel, nk=nk, mod_row=mod_row, ng_row=ng_row, next_kind=next_kind)
    in_specs = [
        _single((tm, d), lambda i, k: (i, 0)),
        _single((tm, d), lambda i, k: (i, 0)),
        pl.BlockSpec((None, None, N_MOD, d), lambda i, k: (l, grp(i), 0, 0)),
        pl.BlockSpec((None, 6, d), lambda i, k: (l, 0, 0)),
        pl.BlockSpec((pl.squeezed, pl.Element(d), pl.Element(tf)), lambda i, k: (l, 0, col(k))),
        pl.BlockSpec((pl.squeezed, pl.Element(d), pl.Element(tf)), lambda i, k: (l, 0, col(k, D_FF))),
        pl.BlockSpec((pl.squeezed, pl.Element(tf), pl.Element(d)), lambda i, k: (l, col(k), 0)),
    ]
    args = [x, u, mod4, norm_g, wgu, wgu, wd]
    if next_kind == "layer":
        in_specs += [pl.BlockSpec((None, None, N_MOD, d), lambda i, k: (l + 1, grp(i), 0, 0)),
                     pl.BlockSpec((None, 6, d), lambda i, k: (l + 1, 0, 0))]
        args += [mod4, norm_g]
    out_specs = [_single((tm, d), lambda i, k: (i, 0))]
    out_shape = [jax.ShapeDtypeStruct((n_tok, d), F32)]
    if next_kind is not None:
        out_specs.append(_single((tm, d), lambda i, k: (i, 0)))
        out_shape.append(jax.ShapeDtypeStruct((n_tok, d), BF16))
    return pl.pallas_call(
        kern,
        grid=(n_tok // tm, nk),
        in_specs=in_specs,
        out_specs=out_specs,
        out_shape=out_shape,
        scratch_shapes=[
            pltpu.VMEM((d, 2 * tf), BF16),
            pltpu.VMEM((tf, d), BF16),
        ],
        compiler_params=_cparams(2),
        name="ffn",
    )(*args)


def _embed_kernel(xp_ref, xs_ref, pe_ref, mod_ref, ng_ref, x_ref, u_ref, *, n_ctx_tiles):
    i = pl.program_id(0)

    @pl.when(i < n_ctx_tiles)
    def _():
        x_ref[...] = xp_ref[...]

    @pl.when(i >= n_ctx_tiles)
    def _():
        x_ref[...] = xs_ref[...] + pe_ref[...]

    u_ref[...] = _norm_mod(x_ref[...], ng_ref[0:1, :], mod_ref[0:1, :], mod_ref[1:2, :]).astype(BF16)


def _embed(xp, xs, pe, mod4, norm_g, rows, tm=256):
    d = xp.shape[1]
    assert rows.lat_len % tm == 0 and rows.n_ctx_tok % tm == 0
    n_ctx_tiles = rows.n_ctx_tok // tm
    pe_tiles = rows.lat_len // tm
    grp = lambda i: rows.group_of_tile(i, tm)
    kern = functools.partial(_embed_kernel, n_ctx_tiles=n_ctx_tiles)
    return pl.pallas_call(
        kern,
        grid=(rows.n_tok // tm,),
        in_specs=[
            pl.BlockSpec((tm, d), lambda i: (jnp.minimum(i, n_ctx_tiles - 1), 0)),
            pl.BlockSpec((tm, d), lambda i: (jnp.maximum(i - n_ctx_tiles, 0), 0)),
            pl.BlockSpec((tm, d), lambda i: (jnp.maximum(i - n_ctx_tiles, 0) % pe_tiles, 0)),
            pl.BlockSpec((None, None, N_MOD, d), lambda i: (0, grp(i), 0, 0)),
            pl.BlockSpec((None, 6, d), lambda i: (0, 0, 0)),
        ],
        out_specs=[pl.BlockSpec((tm, d), lambda i: (i, 0)), pl.BlockSpec((tm, d), lambda i: (i, 0))],
        out_shape=[jax.ShapeDtypeStruct((rows.n_tok, d), F32), jax.ShapeDtypeStruct((rows.n_tok, d), BF16)],
        compiler_params=_cparams(1),
        name="embed",
    )(xp, xs, pe, mod4, norm_g)


def _dot_nt(a, b):
    return lax.dot_general(a, b, (((1,), (1,)), ((), ())), preferred_element_type=F32)


def _inproj_kernel(u_ref, wa_ref, wb_ref, op_ref, odt_ref, w_sc, *, n_head, tm):
    j = pl.program_id(0)
    m = pl.program_id(1)
    tn = w_sc.shape[0]
    off = DT_WIDTH

    @pl.when(jnp.logical_and(m == 0, j < n_head))
    def _():
        w_sc[...] = wa_ref[...].astype(BF16)

    @pl.when(jnp.logical_and(m == 0, j == n_head))
    def _():
        w_sc[:off, :] = wb_ref[...].astype(BF16)

    @pl.when(jnp.logical_and(m == 0, j > n_head))
    def _():
        w_sc[:tn - off, :] = wa_ref[off:, :].astype(BF16)
        w_sc[tn - off:, :] = wb_ref[...].astype(BF16)

    u = u_ref[pl.ds(pl.multiple_of(m * tm, tm), tm), :]

    @pl.when(j != n_head)
    def _():
        op_ref[...] = _dot_nt(u, w_sc[...])

    @pl.when(j == n_head)
    def _():
        odt_ref[...] = _dot_nt(u, w_sc[:LANES, :])


def _inproj(u, w_in_t, l, tm=2048, tn=512):
    n_tok, d = u.shape
    off = DT_WIDTH
    assert TAIL_COL % tn == off and HEAD_COLS % tn == 0 and TAIL_COLS % tn == 0 and tn % off == 0
    n_head = HEAD_COLS // tn
    n_tail = TAIL_COLS // tn
    n_m = n_tok // tm
    sub = tn // off
    proj_j = lambda j: jnp.where(j < n_head, j, jnp.where(j == n_head, n_head - 1, j - 1))
    proj_m = lambda j, m: jnp.where(j == n_head, n_m - 1, m)
    dt_m = lambda j, m: jnp.where(j < n_head, 0, jnp.where(j == n_head, m, n_m - 1))
    wide_j = lambda j: jnp.where(j <= n_head, jnp.minimum(j, n_head), j - 1)
    narrow_j = lambda j: jnp.where(j <= n_head, DT_COL // off, j * sub)
    kern = functools.partial(_inproj_kernel, n_head=n_head, tm=tm)
    return pl.pallas_call(
        kern,
        grid=(n_head + 1 + n_tail, n_m),
        in_specs=[
            _single((n_tok, d), lambda j, m: (0, 0)),
            pl.BlockSpec((None, tn, d), lambda j, m: (l, wide_j(j), 0)),
            pl.BlockSpec((None, off, d), lambda j, m: (l, narrow_j(j), 0)),
        ],
        out_specs=[
            pl.BlockSpec((tm, tn), lambda j, m: (proj_m(j, m), proj_j(j))),
            pl.BlockSpec((tm, LANES), lambda j, m: (dt_m(j, m), 0)),
        ],
        out_shape=[
            jax.ShapeDtypeStruct((n_tok, HEAD_COLS + TAIL_COLS), F32),
            jax.ShapeDtypeStruct((n_tok, LANES), F32),
        ],
        scratch_shapes=[pltpu.VMEM((tn, d), BF16)],
        compiler_params=_cparams(2),
        name="inproj",
    )(u, w_in_t, w_in_t)


def _dwconv_rows(x, w, left):
    n_rows = x.shape[0]
    row = lax.broadcasted_iota(jnp.int32, x.shape, 0)
    out = None
    for k in range(w.shape[0]):
        off = k - left
        if off == 0:
            term = x
        else:
            shifted = pltpu.roll(x, (-off) % n_rows, axis=0)
            valid = (row < n_rows - off) if off > 0 else (row >= -off)
            term = jnp.where(valid, shifted, 0.0)
        term = term * w[k:k + 1, :]
        out = term if out is None else out + term
    return out


def _ssd_kernel(*refs, seq_len, has_h0, n_alias, emit_state):
    (z_ref, xr_ref, br_ref, cr_ref, dt_ref, cwx_ref, cwb_ref, cwc_ref, cbx_ref, cbb_ref, cbc_ref,
     dtb_ref, alog_ref, dsk_ref, gn_ref, sel_ref, e2_ref, tri_ref, trit_ref) = refs[:19]
    pos = 19
    h0_ref = None
    if has_h0:
        h0_ref = refs[pos]
        pos += 1
    pos += n_alias
    yn_ref = refs[pos]
    pos += 1
    hfin_ref = None
    if emit_state:
        hfin_ref = refs[pos]
        pos += 1
    xa_sc, ba_sc, ca_sc, dts_sc, a_sc, ht_sc, yg_sc, y_sc, ssq_sc = refs[pos:pos + 9]

    gi = pl.program_id(1)
    q = SSD_CHUNK
    nc = seq_len // q
    left = (SSD_CONV - 1) // 2
    hpg = HEADS_PER_GROUP
    gw = GROUP_WIDTH

    xa = _silu(_dwconv_rows(xr_ref[...], cwx_ref[...], left) + cbx_ref[...])
    xa_sc[...] = xa
    ba_sc[...] = _silu(_dwconv_rows(br_ref[...], cwb_ref[...], left) + cbb_ref[...])
    ca_sc[...] = _silu(_dwconv_rows(cr_ref[...], cwc_ref[...], left) + cbc_ref[...])
    yg_sc[...] = xa * dsk_ref[...]

    draw = dt_ref[...] + dtb_ref[...]
    dt_all = jnp.maximum(draw, 0.0) + jnp.log1p(jnp.exp(-jnp.abs(draw)))
    dts = _dot_r01(dt_all, sel_ref[...])
    dts_sc[...] = dts
    a_sc[...] = dts * (-jnp.exp(alog_ref[...]))

    for d in range(2):
        if has_h0:
            for p in range(hpg // 2):
                blk = h0_ref[d, 2 * p:2 * p + 2].reshape(2 * SSD_HEAD_DIM, SSD_STATE)
                ht_sc[d, :, p * LANES:(p + 1) * LANES] = blk.T
        else:
            ht_sc[d] = jnp.zeros((SSD_STATE, gw), F32)

    ri = lax.broadcasted_iota(jnp.int32, (q, q), 0)
    ci = lax.broadcasted_iota(jnp.int32, (q, q), 1)
    lane_lo = lax.broadcasted_iota(jnp.int32, (q, LANES), 1) < SSD_HEAD_DIM
    neg_inf = jnp.float32(-jnp.inf)

    def chunk_dir(c, d):
        rows = pl.ds(pl.multiple_of(c * q, q), q)
        a_c = a_sc[rows, :]
        if d == 0:
            cs = _dot_l01(tri_ref[...], a_c)
            tot = cs[q - 1:q, :]
            mask = ri >= ci
        else:
            cs = _dot_l01(trit_ref[...], a_c)
            tot = cs[0:1, :]
            mask = ri <= ci
        cst = cs.T
        stack = jnp.concatenate([
            dts_sc[rows, :].astype(BF16),
            jnp.exp(cs).astype(BF16),
            jnp.exp(tot - cs).astype(BF16),
            jnp.broadcast_to(jnp.exp(tot), (BF16_ROWS, LANES)).astype(BF16)], axis=0)
        ex = jnp.dot(stack, e2_ref[:, d * gw:(d + 1) * gw], preferred_element_type=F32)
        dtx, ecs_x, dte_x, ea_x = ex[0:q], ex[q:2 * q], ex[2 * q:3 * q], ex[3 * q:3 * q + 1]
        b_c = ba_sc[rows, :]
        c_bf = ca_sc[rows, :].astype(BF16)
        xdt = xa_sc[rows, :] * dtx
        g = lax.dot_general(c_bf, b_c.astype(BF16), (((1,), (1,)), ((), ())),
                            preferred_element_type=F32)
        ht = ht_sc[d]
        y_off = jnp.dot(c_bf, ht.astype(BF16), preferred_element_type=F32) * ecs_x
        pieces = []
        for p in range(hpg // 2):
            ms = []
            for jj in range(2):
                r = d * hpg + 2 * p + jj
                diff = cs[:, r:r + 1] - cst[r:r + 1, :]
                ms.append((g * jnp.exp(jnp.where(mask, diff, neg_inf))).astype(BF16))
            xp = xdt[:, p * LANES:(p + 1) * LANES]
            rhs = jnp.concatenate([jnp.where(lane_lo, xp, 0.0), jnp.where(lane_lo, 0.0, xp)], axis=0)
            pieces.append(jnp.dot(jnp.concatenate(ms, axis=1), rhs.astype(BF16),
                                  preferred_element_type=F32))
        s_t = jnp.dot(b_c.T.astype(BF16), (xdt * dte_x).astype(BF16), preferred_element_type=F32)
        ht_sc[d] = ht * ea_x + s_t
        yg_sc[rows, :] += jnp.concatenate(pieces, axis=1) + y_off

    def body(c, carry):
        chunk_dir(c, 0)
        chunk_dir(nc - 1 - c, 1)
        return carry

    lax.fori_loop(0, nc, body, 0, unroll=2)

    yz = yg_sc[...] * _silu(z_ref[...])
    y_sc[gi] = yz
    part = jnp.broadcast_to(jnp.sum(yz * yz, axis=-1, keepdims=True), ssq_sc.shape)

    @pl.when(gi == 0)
    def _():
        ssq_sc[...] = part

    @pl.when(gi > 0)
    def _():
        ssq_sc[...] += part

    @pl.when(gi == SSD_GROUPS - 1)
    def _():
        inv = lax.rsqrt(ssq_sc[:, 0:1] * (1.0 / SSD_INNER) + NORM_EPS)
        for gg in range(SSD_GROUPS):
            cols = slice(gg * gw, (gg + 1) * gw)
            yn_ref[:, cols] = (y_sc[gg] * inv * gn_ref[:, cols]).astype(BF16)

    if emit_state:
        for d in range(2):
            for p in range(hpg // 2):
                blk = ht_sc[d, :, p * LANES:(p + 1) * LANES].T
                hfin_ref[d, 2 * p:2 * p + 2] = blk.reshape(2, SSD_HEAD_DIM, SSD_STATE)


def _ssd_constants():
    hpg = HEADS_PER_GROUP
    sel = np.zeros((SSD_GROUPS, LANES, LANES), np.float32)
    for g in range(SSD_GROUPS):
        for d in range(2):
            for j in range(hpg):
                sel[g, d * SSD_HEADS + hpg * g + j, d * hpg + j] = 1.0
    e2 = np.zeros((LANES, 2 * GROUP_WIDTH), np.float32)
    for d in range(2):
        for j in range(hpg):
            lo = d * GROUP_WIDTH + j * SSD_HEAD_DIM
            e2[d * hpg + j, lo:lo + SSD_HEAD_DIM] = 1.0
    tri = np.tril(np.ones((SSD_CHUNK, SSD_CHUNK), np.float32))
    return (jnp.asarray(sel, BF16), jnp.asarray(e2, BF16), jnp.asarray(tri, BF16),
            jnp.asarray(tri.T, BF16))


def _ssd(head, dt, conv_w, conv_b3, dtb, alog_g, dsk_g, norm_g3, consts, l, *, row0, n_seq, seq_len,
         h0=None, prev_y=None, prev_state=None, state_shape=None):
    n_tok = head.shape[0]
    sel, e2, tri, trit = consts
    assert row0 % seq_len == 0 and seq_len % (2 * SSD_CHUNK) == 0
    blk0 = row0 // seq_len
    gw = GROUP_WIDTH
    has_h0 = h0 is not None
    emit_state = state_shape is not None
    xcol = SSD_INNER // gw
    bcol = (SSD_INNER + SSD_INNER) // LANES
    ccol = bcol + SSD_GROUPS
    cwb = SSD_INNER // LANES
    cwc = cwb + SSD_GROUPS
    in_specs = [
        pl.BlockSpec((seq_len, gw), lambda b, g: (blk0 + b, g)),
        pl.BlockSpec((seq_len, gw), lambda b, g: (blk0 + b, xcol + g)),
        pl.BlockSpec((seq_len, LANES), lambda b, g: (blk0 + b, bcol + g)),
        pl.BlockSpec((seq_len, LANES), lambda b, g: (blk0 + b, ccol + g)),
        pl.BlockSpec((seq_len, LANES), lambda b, g: (blk0 + b, 0)),
        pl.BlockSpec((None, SSD_CONV, gw), lambda b, g: (l, 0, g)),
        pl.BlockSpec((None, SSD_CONV, LANES), lambda b, g: (l, 0, cwb + g)),
        pl.BlockSpec((None, SSD_CONV, LANES), lambda b, g: (l, 0, cwc + g)),
        pl.BlockSpec((None, 1, gw), lambda b, g: (l, 0, g)),
        pl.BlockSpec((None, 1, LANES), lambda b, g: (l, 0, cwb + g)),
        pl.BlockSpec((None, 1, LANES), lambda b, g: (l, 0, cwc + g)),
        pl.BlockSpec((None, 1, LANES), lambda b, g: (l, 0, 0)),
        pl.BlockSpec((None, None, 1, LANES), lambda b, g: (l, g, 0, 0)),
        pl.BlockSpec((None, None, 1, gw), lambda b, g: (l, g, 0, 0)),
        pl.BlockSpec((None, 1, SSD_INNER), lambda b, g: (l, 0, 0)),
        pl.BlockSpec((None, LANES, LANES), lambda b, g: (g, 0, 0)),
        pl.BlockSpec((LANES, 2 * gw), lambda b, g: (0, 0)),
        pl.BlockSpec((SSD_CHUNK, SSD_CHUNK), lambda b, g: (0, 0)),
        pl.BlockSpec((SSD_CHUNK, SSD_CHUNK), lambda b, g: (0, 0)),
    ]
    args = [head, head, head, head, dt, conv_w, conv_w, conv_w, conv_b3, conv_b3, conv_b3,
            dtb, alog_g, dsk_g, norm_g3, sel, e2, tri, trit]
    if has_h0:
        in_specs.append(pl.BlockSpec((None, None, 2, HEADS_PER_GROUP, SSD_HEAD_DIM, SSD_STATE),
                                     lambda b, g: (b, l, 0, g, 0, 0)))
        args.append(h0)
    aliases = {}
    if prev_y is not None:
        in_specs.append(pl.BlockSpec(memory_space=pl.ANY))
        aliases[len(args)] = 0
        args.append(prev_y)
    if prev_state is not None:
        assert emit_state
        in_specs.append(pl.BlockSpec(memory_space=pl.ANY))
        aliases[len(args)] = 1
        args.append(prev_state)
    out_specs = [pl.BlockSpec((seq_len, SSD_INNER), lambda b, g: (blk0 + b, 0))]
    out_shape = [jax.ShapeDtypeStruct((n_tok, SSD_INNER), BF16)]
    if emit_state:
        out_specs.append(pl.BlockSpec((None, None, 2, HEADS_PER_GROUP, SSD_HEAD_DIM, SSD_STATE),
                                      lambda b, g: (b, l, 0, g, 0, 0)))
        out_shape.append(jax.ShapeDtypeStruct(state_shape, F32))
    kern = functools.partial(_ssd_kernel, seq_len=seq_len, has_h0=has_h0, n_alias=len(aliases),
                             emit_state=emit_state)
    return pl.pallas_call(
        kern,
        grid=(n_seq, SSD_GROUPS),
        in_specs=in_specs,
        out_specs=out_specs,
        out_shape=out_shape,
        input_output_aliases=aliases,
        scratch_shapes=[
            pltpu.VMEM((seq_len, gw), F32),
            pltpu.VMEM((seq_len, LANES), F32),
            pltpu.VMEM((seq_len, LANES), F32),
            pltpu.VMEM((seq_len, LANES), F32),
            pltpu.VMEM((seq_len, LANES), F32),
            pltpu.VMEM((2, SSD_STATE, gw), F32),
            pltpu.VMEM((seq_len, gw), F32),
            pltpu.VMEM((SSD_GROUPS, seq_len, gw), F32),
            pltpu.VMEM((seq_len, LANES), F32),
        ],
        compiler_params=_cparams(2),
        name="ssd_lat" if has_h0 else "ssd_ctx",
    )(*args)


def _scft_kernel(*refs, n_alias, n_grp):
    b_ref, c_ref, x_ref, f_ref, cw_ref, cl_ref, sl_ref, cc_ref, sc_ref = refs[:9]
    pos = 9 + n_alias
    ysc_ref, yft_ref = refs[pos], refs[pos + 1]
    v = c_ref[...] * x_ref[...]
    ysc_ref[...] = (b_ref[...] * _dwconv_rows(v, cw_ref[...], (SC_CONV - 1) // 2)).astype(BF16)
    tw = FT_GROUP_DIM
    for gg in range(n_grp):
        cols = slice(gg * tw, (gg + 1) * tw)
        u = f_ref[:, cols].astype(BF16)
        p = jnp.dot(u, cc_ref[...], preferred_element_type=F32).astype(BF16)
        s = jnp.dot(u, sc_ref[...], preferred_element_type=F32).astype(BF16)
        yft_ref[:, cols] = (jnp.dot(cl_ref[...], p, preferred_element_type=F32)
                            - jnp.dot(sl_ref[...], s, preferred_element_type=F32)).astype(BF16)


def _dft_mats(n):
    k = np.arange(n)
    ang = 2.0 * np.pi * ((k[:, None] * k[None, :]) % n) / n
    scale = 1.0 / math.sqrt(n)
    return jnp.asarray(np.cos(ang) * scale, BF16), jnp.asarray(np.sin(ang) * scale, BF16)


def _scft(tail, sc_conv_w, l, *, row0, n_seq, seq_len, n_grp, prev=None):
    n_tok = tail.shape[0]
    tw = n_grp * FT_GROUP_DIM
    assert row0 % seq_len == 0 and SC_WIDTH % tw == 0 and HEAD_COLS % tw == 0
    blk0 = row0 // seq_len
    nt = SC_WIDTH // tw
    c0 = HEAD_COLS // tw
    cl, sl = _dft_mats(seq_len)
    cc, sc = _dft_mats(FT_GROUP_DIM)
    in_specs = [
        pl.BlockSpec((seq_len, tw), lambda b, j: (blk0 + b, c0 + j)),
        pl.BlockSpec((seq_len, tw), lambda b, j: (blk0 + b, c0 + nt + j)),
        pl.BlockSpec((seq_len, tw), lambda b, j: (blk0 + b, c0 + 2 * nt + j)),
        pl.BlockSpec((seq_len, tw), lambda b, j: (blk0 + b, c0 + 3 * nt + j)),
        pl.BlockSpec((None, SC_CONV, tw), lambda b, j: (l, 0, j)),
        pl.BlockSpec((seq_len, seq_len), lambda b, j: (0, 0)),
        pl.BlockSpec((seq_len, seq_len), lambda b, j: (0, 0)),
        pl.BlockSpec((FT_GROUP_DIM, FT_GROUP_DIM), lambda b, j: (0, 0)),
        pl.BlockSpec((FT_GROUP_DIM, FT_GROUP_DIM), lambda b, j: (0, 0)),
    ]
    args = [tail, tail, tail, tail, sc_conv_w, cl, sl, cc, sc]
    aliases = {}
    if prev is not None:
        in_specs += [pl.BlockSpec(memory_space=pl.ANY), pl.BlockSpec(memory_space=pl.ANY)]
        aliases = {len(args): 0, len(args) + 1: 1}
        args += list(prev)
    kern = functools.partial(_scft_kernel, n_alias=len(aliases), n_grp=n_grp)
    return pl.pallas_call(
        kern,
        grid=(n_seq, nt),
        in_specs=in_specs,
        out_specs=[
            pl.BlockSpec((seq_len, tw), lambda b, j: (blk0 + b, j)),
            pl.BlockSpec((seq_len, tw), lambda b, j: (blk0 + b, j)),
        ],
        out_shape=[
            jax.ShapeDtypeStruct((n_tok, SC_WIDTH), BF16),
            jax.ShapeDtypeStruct((n_tok, FT_WIDTH), BF16),
        ],
        input_output_aliases=aliases,
        compiler_params=_cparams(2),
        name="scft_lat" if prev is not None else "scft_ctx",
    )(*args)


def _mixout_kernel(a0_ref, a1_ref, a2_ref, g0_ref, g1_ref, g2_ref, w0_ref, w1_ref, w2_ref, wo_ref,
                   x_ref, mod_ref, ng_ref, o_ref, un_ref, *, nk):
    k = pl.program_id(1)

    @pl.when(k == 0)
    def _():
        o_ref[...] = jnp.zeros_like(o_ref)

    m = jax.nn.sigmoid(g0_ref[...]) * jnp.dot(a0_ref[...], w0_ref[...].astype(BF16),
                                               preferred_element_type=F32)
    m += jax.nn.sigmoid(g1_ref[...]) * jnp.dot(a1_ref[...], w1_ref[...].astype(BF16),
                                                preferred_element_type=F32)
    m += jax.nn.sigmoid(g2_ref[...]) * jnp.dot(a2_ref[...], w2_ref[...].astype(BF16),
                                                preferred_element_type=F32)
    o_ref[...] += jnp.dot(m.astype(BF16), wo_ref[...].astype(BF16), preferred_element_type=F32)

    @pl.when(k == nk - 1)
    def _():
        y = _rms(o_ref[...], ng_ref[3:4, :])
        x_new = x_ref[...] + mod_ref[5:6, :] * y
        o_ref[...] = x_new
        un_ref[...] = _norm_mod(x_new, ng_ref[4:5, :], mod_ref[6:7, :], mod_ref[7:8, :]).astype(BF16)


def _mixout(yssd, ysc, yft, tail, w_br_ssd, w_br_sc, w_br_ft, w_out, x, mod4, norm_g, l, rows,
            tm=1024, tk=256):
    n_tok, d = x.shape
    nk = d // tk
    gate0 = (HEAD_COLS + 3 * SC_WIDTH + FT_WIDTH) // tk
    per = d // tk
    grp = lambda i: rows.group_of_tile(i, tm)
    wk = lambda i, k: k
    kern = functools.partial(_mixout_kernel, nk=nk)
    return pl.pallas_call(
        kern,
        grid=(n_tok // tm, nk),
        in_specs=[
            _single((tm, SSD_INNER), lambda i, k: (i, 0)),
            _single((tm, SC_WIDTH), lambda i, k: (i, 0)),
            _single((tm, FT_WIDTH), lambda i, k: (i, 0)),
            pl.BlockSpec((tm, tk), lambda i, k: (i, gate0 + k)),
            pl.BlockSpec((tm, tk), lambda i, k: (i, gate0 + per + k)),
            pl.BlockSpec((tm, tk), lambda i, k: (i, gate0 + 2 * per + k)),
            pl.BlockSpec((None, SSD_INNER, tk), lambda i, k: (l, 0, wk(i, k))),
            pl.BlockSpec((None, SC_WIDTH, tk), lambda i, k: (l, 0, wk(i, k))),
            pl.BlockSpec((None, FT_WIDTH, tk), lambda i, k: (l, 0, wk(i, k))),
            pl.BlockSpec((None, tk, d), lambda i, k: (l, wk(i, k), 0)),
            _single((tm, d), lambda i, k: (i, 0)),
            pl.BlockSpec((None, None, N_MOD, d), lambda i, k: (l, grp(i), 0, 0)),
            pl.BlockSpec((None, 6, d), lambda i, k: (l, 0, 0)),
        ],
        out_specs=[_single((tm, d), lambda i, k: (i, 0)), _single((tm, d), lambda i, k: (i, 0))],
        out_shape=[jax.ShapeDtypeStruct((n_tok, d), F32), jax.ShapeDtypeStruct((n_tok, d), BF16)],
        compiler_params=_cparams(2),
        name="mixout",
    )(yssd, ysc, yft, tail, tail, tail, w_br_ssd, w_br_sc, w_br_ft, w_out, x, mod4, norm_g)


def _grid_pos_emb(n_tok):
    rows = n_tok // GRID_W
    t = np.arange(rows * GRID_W)
    r = (t // GRID_W).astype(np.float32)[:, None]
    col = (t % GRID_W).astype(np.float32)[:, None]
    nf = D_MODEL // 4
    omega = (1.0 / (np.float32(POS_BASE) ** (np.arange(nf, dtype=np.float32) / np.float32(nf)))).astype(np.float32)
    ro = (r * omega).astype(np.float32).astype(np.float64)
    co = (col * omega).astype(np.float32).astype(np.float64)
    return np.concatenate([np.sin(ro), np.cos(ro), np.sin(co), np.cos(co)], axis=-1).astype(np.float32)


def _group_lanes(p, hpg=HEADS_PER_GROUP):
    depth = p.shape[0]
    t = p.reshape(depth, 2, SSD_GROUPS, hpg).transpose(0, 2, 1, 3).reshape(depth, SSD_GROUPS, 1, 2 * hpg)
    return jnp.pad(t, ((0, 0), (0, 0), (0, 0), (0, LANES - 2 * hpg)))


def kernel(x_prompt, x_sample, state_ssd, c, c_ctx, ada_w, ada_b, norm_g, ffn1_wgu, ffn1_wd, w_in,
           ssd_conv_w, ssd_conv_b, ssd_dt_bias, ssd_a_log, ssd_d, ssd_norm_g, sc_conv_w,
           w_br_ssd, w_br_sc, w_br_ft, w_out, ffn2_wgu, ffn2_wd):
    n_ctx, ctx_len, d = x_prompt.shape
    n_lat, lat_len, _ = x_sample.shape
    depth = ada_w.shape[0]
    rows = _Rows(n_ctx * ctx_len, n_lat, lat_len)

    c8 = jnp.concatenate([c_ctx[None, :], c, jnp.zeros((8 - 1 - n_lat, d), F32)], axis=0)
    mod4 = _modulation(c8, ada_w, ada_b)[:, :1 + n_lat].reshape(depth, 1 + n_lat, N_MOD, d)

    x, u = _embed(x_prompt.reshape(n_ctx * ctx_len, d), x_sample.reshape(n_lat * lat_len, d),
                  jnp.asarray(_grid_pos_emb(lat_len)), mod4, norm_g, rows)

    consts = _ssd_constants()
    w_in_t = jnp.swapaxes(w_in, 1, 2)
    conv_b3 = ssd_conv_b.reshape(depth, 1, SSD_XBC)
    dtb = jnp.pad(ssd_dt_bias.reshape(depth, 1, DT_WIDTH), ((0, 0), (0, 0), (0, LANES - DT_WIDTH)))
    alog_g = _group_lanes(ssd_a_log)
    dsk_g = jnp.repeat(ssd_d, SSD_HEAD_DIM, axis=-1).reshape(depth, SSD_GROUPS, 1, GROUP_WIDTH)
    ssd_norm_g3 = ssd_norm_g.reshape(depth, 1, SSD_INNER)
    state_shape = (n_ctx, depth, 2, SSD_HEADS, SSD_HEAD_DIM, SSD_STATE)

    states = None
    for l in range(depth):
        x, u = _ffn(x, u, mod4, norm_g, ffn1_wgu, ffn1_wd, l, 0, rows, "same")
        proj, dt = _inproj(u, w_in_t, l)
        tail = proj
        ssd_args = (proj, dt, ssd_conv_w, conv_b3, dtb, alog_g, dsk_g, ssd_norm_g3, consts, l)
        yssd, states = _ssd(*ssd_args, row0=0, n_seq=n_ctx, seq_len=ctx_len,
                            prev_state=states, state_shape=state_shape)
        (yssd,) = _ssd(*ssd_args, row0=rows.n_ctx_tok, n_seq=n_lat, seq_len=lat_len,
                       h0=state_ssd, prev_y=yssd)
        ysc, yft = _scft(tail, sc_conv_w, l, row0=0, n_seq=n_ctx, seq_len=ctx_len, n_grp=FT_GROUPS)
        ysc, yft = _scft(tail, sc_conv_w, l, row0=rows.n_ctx_tok, n_seq=n_lat, seq_len=lat_len,
                         n_grp=1, prev=(ysc, yft))
        x, u = _mixout(yssd, ysc, yft, tail, w_br_ssd, w_br_sc, w_br_ft, w_out, x, mod4, norm_g, l, rows)
        if l + 1 < depth:
            x, u = _ffn(x, u, mod4, norm_g, ffn2_wgu, ffn2_wd, l, 1, rows, "layer")
        else:
            (x,) = _ffn(x, u, mod4, norm_g, ffn2_wgu, ffn2_wd, l, 1, rows, None)

    y_prompt = x[:rows.n_ctx_tok].reshape(n_ctx, ctx_len, d)
    y_sample = x[rows.n_ctx_tok:].reshape(n_lat, lat_len, d)
    return (y_prompt, y_sample, states)
```

```python
import functools
import math

import numpy as np
import jax
import jax.numpy as jnp
from jax import lax
from jax.experimental import pallas as pl
from jax.experimental.pallas import tpu as pltpu

F32 = jnp.float32
BF16 = jnp.bfloat16

D_MODEL = 2048
DEPTH = 4
GRID_W = 64
POS_BASE = 10000.0
NORM_EPS = 1e-6
SSD_HEADS = 32
SSD_HEAD_DIM = 64
SSD_INNER = SSD_HEADS * SSD_HEAD_DIM
SSD_GROUPS = 4
SSD_STATE = 128
SSD_CONV = 4
SSD_CHUNK = 128
SSD_XBC = SSD_INNER + 2 * SSD_GROUPS * SSD_STATE
SC_WIDTH = 1024
SC_CONV = 3
FT_WIDTH = 1024
FT_GROUPS = 4
FT_GROUP_DIM = FT_WIDTH // FT_GROUPS
D_FF = 5504
N_BRANCH = 3
N_MOD = 9
HEADS_PER_GROUP = SSD_HEADS // SSD_GROUPS
GROUP_WIDTH = HEADS_PER_GROUP * SSD_HEAD_DIM
HEAD_COLS = SSD_INNER + SSD_XBC
DT_COL = HEAD_COLS
DT_WIDTH = 2 * SSD_HEADS
TAIL_COL = HEAD_COLS + DT_WIDTH
TAIL_COLS = 3 * SC_WIDTH + FT_WIDTH + N_BRANCH * D_MODEL
IN_COLS = TAIL_COL + TAIL_COLS

LANES = 128
BF16_ROWS = 16
VMEM_LIMIT_BYTES = 60 * 1024 * 1024


def _cparams(n_axes):
    return pltpu.CompilerParams(dimension_semantics=("arbitrary",) * n_axes,
                                vmem_limit_bytes=VMEM_LIMIT_BYTES)


def _silu(x):
    return x * jax.nn.sigmoid(x)


def _rms(x, g):
    ms = jnp.mean(x * x, axis=-1, keepdims=True)
    return x * lax.rsqrt(ms + NORM_EPS) * g


def _norm_mod(x, g, shift, scale):
    return _rms(x, g) * (1.0 + scale) + shift


def _split_bf16(x, n):
    parts = []
    r = x
    for i in range(n):
        p = r.astype(BF16)
        parts.append(p)
        if i + 1 < n:
            r = r - p.astype(F32)
    return parts


def _dot_r01(x, m01, n=3):
    acc = None
    for p in _split_bf16(x, n):
        t = jnp.dot(p, m01, preferred_element_type=F32)
        acc = t if acc is None else acc + t
    return acc


def _dot_l01(m01, x, n=3):
    acc = None
    for p in _split_bf16(x, n):
        t = jnp.dot(m01, p, preferred_element_type=F32)
        acc = t if acc is None else acc + t
    return acc


def _mod_kernel(c_ref, w_ref, b_ref, o_ref):
    s = _silu(c_ref[...]).astype(BF16)
    o_ref[...] = jnp.dot(s, w_ref[...].astype(BF16), preferred_element_type=F32) + b_ref[...]


def _modulation(c8, ada_w, ada_b, tn=1024):
    depth, d, n = ada_w.shape
    return pl.pallas_call(
        _mod_kernel,
        grid=(depth, n // tn),
        in_specs=[
            pl.BlockSpec((8, d), lambda l, j: (0, 0)),
            pl.BlockSpec((None, d, tn), lambda l, j: (l, 0, j)),
            pl.BlockSpec((None, 1, tn), lambda l, j: (l, 0, j)),
        ],
        out_specs=pl.BlockSpec((None, 8, tn), lambda l, j: (l, 0, j)),
        out_shape=jax.ShapeDtypeStruct((depth, 8, n), F32),
        compiler_params=_cparams(2),
        name="adaln_mod",
    )(c8, ada_w, ada_b.reshape(depth, 1, n))


class _Rows:
    def __init__(self, n_ctx_tok, n_lat, lat_len):
        self.n_ctx_tok = n_ctx_tok
        self.n_lat = n_lat
        self.lat_len = lat_len
        self.n_tok = n_ctx_tok + n_lat * lat_len

    def group_of_tile(self, i, tm):
        assert self.n_ctx_tok % tm == 0 and self.lat_len % tm == 0
        n_ctx_tiles = self.n_ctx_tok // tm
        per_lat = self.lat_len // tm
        return jnp.where(i < n_ctx_tiles, 0, 1 + (i - n_ctx_tiles) // per_lat)


def _single(block_shape, index_map):
    return pl.BlockSpec(block_shape, index_map, pipeline_mode=pl.Buffered(1))


FFN_TF = 512
FFN_SPLIT = 2
EPILOGUE_ROWS = 256


def _ffn_kernel(*refs, nk, mod_row, ng_row, next_kind):
    x_ref, u_ref, mod_ref, ng_ref, wg_ref, wu_ref, wd_ref = refs[:7]
    pos = 7
    if next_kind == "layer":
        modn_ref, ngn_ref = refs[pos:pos + 2]
        pos += 2
    o_ref = refs[pos]
    pos += 1
    if next_kind is not None:
        un_ref = refs[pos]
        pos += 1
    acc_sc, wup_sc, wdn_sc = refs[pos:pos + 3]
    k = pl.program_id(1)
    tf = FFN_TF
    te = o_ref.shape[0]

    @pl.when(k == 0)
    def _():
        acc_sc[...] = jnp.zeros_like(acc_sc)

    @pl.when(k < nk)
    def _():
        ts = tf // FFN_SPLIT
        for s in range(FFN_SPLIT):
            wup_sc[:, 2 * s * ts:(2 * s + 1) * ts] = wg_ref[:, s * ts:(s + 1) * ts].astype(BF16)
            wup_sc[:, (2 * s + 1) * ts:(2 * s + 2) * ts] = wu_ref[:, s * ts:(s + 1) * ts].astype(BF16)
        wdn_sc[...] = wd_ref[...].astype(BF16)

        overlap = nk * tf - D_FF
        assert overlap <= ts
        for s in range(FFN_SPLIT):
            h = jnp.dot(u_ref[...], wup_sc[:, 2 * s * ts:(2 * s + 2) * ts], preferred_element_type=F32)
            a = _silu(h[:, :ts]) * h[:, ts:]
            if s == 0:
                lane = lax.broadcasted_iota(jnp.int32, a.shape, 1)
                a = jnp.where(jnp.logical_and(k == nk - 1, lane < overlap), 0.0, a)
            acc_sc[...] += jnp.dot(a.astype(BF16), wdn_sc[s * ts:(s + 1) * ts, :], preferred_element_type=F32)

    @pl.when(k >= nk)
    def _():
        r0 = pl.multiple_of((k - nk) * te, te)
        y = _rms(acc_sc[pl.ds(r0, te), :], ng_ref[ng_row + 1:ng_row + 2, :])
        x_new = x_ref[...] + 0.5 * mod_ref[mod_row + 2:mod_row + 3, :] * y
        o_ref[...] = x_new
        if next_kind == "same":
            un = _norm_mod(x_new, ng_ref[ng_row + 2:ng_row + 3, :],
                           mod_ref[mod_row + 3:mod_row + 4, :], mod_ref[mod_row + 4:mod_row + 5, :])
            un_ref[...] = un.astype(BF16)
        elif next_kind == "layer":
            un = _norm_mod(x_new, ngn_ref[0:1, :], modn_ref[0:1, :], modn_ref[1:2, :])
            un_ref[...] = un.astype(BF16)


def _ffn(x, u, mod4, norm_g, wgu, wd, l, which, rows, next_kind, tm=1024):
    n_tok, d = x.shape
    tf = FFN_TF
    nk = pl.cdiv(D_FF, tf)
    mod_row = 0 if which == 0 else 6
    ng_row = 0 if which == 0 else 4
    grp = lambda i: rows.group_of_tile(i, tm)
    assert tf % LANES == 0 and D_FF % LANES == 0
    col = lambda k, base=0: LANES * (base // LANES + jnp.minimum(k * (tf // LANES), (D_FF - tf) // LANES))
    kern = functools.partial(_ffn_kernel, nk=nk, mod_row=mod_row, ng_row=ng_row, next_kind=next_kind)
    te = EPILOGUE_ROWS
    ne = tm // te
    erow = lambda i, k: (ne * i + jnp.clip(k - nk, 0, ne - 1), 0)
    in_specs = [
        pl.BlockSpec((te, d), erow),
        pl.BlockSpec((tm, d), lambda i, k: (i, 0)),
        pl.BlockSpec((None, None, N_MOD, d), lambda i, k: (l, grp(i), 0, 0)),
        pl.BlockSpec((None, 6, d), lambda i, k: (l, 0, 0)),
        pl.BlockSpec((pl.squeezed, pl.Element(d), pl.Element(tf)), lambda i, k: (l, 0, col(k))),
        pl.BlockSpec((pl.squeezed, pl.Element(d), pl.Element(tf)), lambda i, k: (l, 0, col(k, D_FF))),
        pl.BlockSpec((pl.squeezed, pl.Element(tf), pl.Element(d)), lambda i, k: (l, col(k), 0)),
    ]
    args = [x, u, mod4, norm_g, wgu, wgu, wd]
    if next_kind == "layer":
        in_specs += [pl.BlockSpec((None, None, N_MOD, d), lambda i, k: (l + 1, grp(i), 0, 0)),
                     pl.BlockSpec((None, 6, d), lambda i, k: (l + 1, 0, 0))]
        args += [mod4, norm_g]
    out_specs = [pl.BlockSpec((te, d), erow)]
    out_shape = [jax.ShapeDtypeStruct((n_tok, d), F32)]
    if next_kind is not None:
        out_specs.append(pl.BlockSpec((te, d), erow))
        out_shape.append(jax.ShapeDtypeStruct((n_tok, d), BF16))
    return pl.pallas_call(
        kern,
        grid=(n_tok // tm, nk + ne),
        in_specs=in_specs,
        out_specs=out_specs,
        out_shape=out_shape,
        scratch_shapes=[
            pltpu.VMEM((tm, d), F32),
            pltpu.VMEM((d, 2 * tf), BF16),
            pltpu.VMEM((tf, d), BF16),
        ],
        compiler_params=_cparams(2),
        name="ffn",
    )(*args)


def _embed_kernel(xp_ref, xs_ref, pe_ref, mod_ref, ng_ref, x_ref, u_ref, *, n_ctx_tiles):
    i = pl.program_id(0)

    @pl.when(i < n_ctx_tiles)
    def _():
        x_ref[...] = xp_ref[...]

    @pl.when(i >= n_ctx_tiles)
    def _():
        x_ref[...] = xs_ref[...] + pe_ref[...]

    u_ref[...] = _norm_mod(x_ref[...], ng_ref[0:1, :], mod_ref[0:1, :], mod_ref[1:2, :]).astype(BF16)


def _embed(xp, xs, pe, mod4, norm_g, rows, tm=256):
    d = xp.shape[1]
    assert rows.lat_len % tm == 0 and rows.n_ctx_tok % tm == 0
    n_ctx_tiles = rows.n_ctx_tok // tm
    pe_tiles = rows.lat_len // tm
    grp = lambda i: rows.group_of_tile(i, tm)
    kern = functools.partial(_embed_kernel, n_ctx_tiles=n_ctx_tiles)
    return pl.pallas_call(
        kern,
        grid=(rows.n_tok // tm,),
        in_specs=[
            pl.BlockSpec((tm, d), lambda i: (jnp.minimum(i, n_ctx_tiles - 1), 0)),
            pl.BlockSpec((tm, d), lambda i: (jnp.maximum(i - n_ctx_tiles, 0), 0)),
            pl.BlockSpec((tm, d), lambda i: (jnp.maximum(i - n_ctx_tiles, 0) % pe_tiles, 0)),
            pl.BlockSpec((None, None, N_MOD, d), lambda i: (0, grp(i), 0, 0)),
            pl.BlockSpec((None, 6, d), lambda i: (0, 0, 0)),
        ],
        out_specs=[pl.BlockSpec((tm, d), lambda i: (i, 0)), pl.BlockSpec((tm, d), lambda i: (i, 0))],
        out_shape=[jax.ShapeDtypeStruct((rows.n_tok, d), F32), jax.ShapeDtypeStruct((rows.n_tok, d), BF16)],
        compiler_params=_cparams(1),
        name="embed",
    )(xp, xs, pe, mod4, norm_g)


def _dot_nt(a, b):
    return lax.dot_general(a, b, (((1,), (1,)), ((), ())), preferred_element_type=F32)


def _inproj_kernel(u_ref, wa_ref, wb_ref, op_ref, odt_ref, w_sc, *, n_head, tm):
    j = pl.program_id(0)
    m = pl.program_id(1)
    tn = w_sc.shape[0]
    off = DT_WIDTH

    @pl.when(jnp.logical_and(m == 0, j < n_head))
    def _():
        w_sc[...] = wa_ref[...].astype(BF16)

    @pl.when(jnp.logical_and(m == 0, j == n_head))
    def _():
        w_sc[:off, :] = wb_ref[...].astype(BF16)

    @pl.when(jnp.logical_and(m == 0, j > n_head))
    def _():
        w_sc[:tn - off, :] = wa_ref[off:, :].astype(BF16)
        w_sc[tn - off:, :] = wb_ref[...].astype(BF16)

    u = u_ref[pl.ds(pl.multiple_of(m * tm, tm), tm), :]

    @pl.when(j != n_head)
    def _():
        op_ref[...] = _dot_nt(u, w_sc[...])

    @pl.when(j == n_head)
    def _():
        odt_ref[...] = _dot_nt(u, w_sc[:LANES, :])


def _inproj(u, w_in_t, l, tm=2048, tn=512):
    n_tok, d = u.shape
    off = DT_WIDTH
    assert TAIL_COL % tn == off and HEAD_COLS % tn == 0 and TAIL_COLS % tn == 0 and tn % off == 0
    n_head = HEAD_COLS // tn
    n_tail = TAIL_COLS // tn
    n_m = n_tok // tm
    sub = tn // off
    proj_j = lambda j: jnp.where(j < n_head, j, jnp.where(j == n_head, n_head - 1, j - 1))
    proj_m = lambda j, m: jnp.where(j == n_head, n_m - 1, m)
    dt_m = lambda j, m: jnp.where(j < n_head, 0, jnp.where(j == n_head, m, n_m - 1))
    wide_j = lambda j: jnp.where(j <= n_head, jnp.minimum(j, n_head), j - 1)
    narrow_j = lambda j: jnp.where(j <= n_head, DT_COL // off, j * sub)
    kern = functools.partial(_inproj_kernel, n_head=n_head, tm=tm)
    return pl.pallas_call(
        kern,
        grid=(n_head + 1 + n_tail, n_m),
        in_specs=[
            _single((n_tok, d), lambda j, m: (0, 0)),
            pl.BlockSpec((None, tn, d), lambda j, m: (l, wide_j(j), 0)),
            pl.BlockSpec((None, off, d), lambda j, m: (l, narrow_j(j), 0)),
        ],
        out_specs=[
            pl.BlockSpec((tm, tn), lambda j, m: (proj_m(j, m), proj_j(j))),
            pl.BlockSpec((tm, LANES), lambda j, m: (dt_m(j, m), 0)),
        ],
        out_shape=[
            jax.ShapeDtypeStruct((n_tok, HEAD_COLS + TAIL_COLS), F32),
            jax.ShapeDtypeStruct((n_tok, LANES), F32),
        ],
        scratch_shapes=[pltpu.VMEM((tn, d), BF16)],
        compiler_params=_cparams(2),
        name="inproj",
    )(u, w_in_t, w_in_t)


def _dwconv_rows(x, w, left):
    n_rows = x.shape[0]
    row = lax.broadcasted_iota(jnp.int32, x.shape, 0)
    out = None
    for k in range(w.shape[0]):
        off = k - left
        if off == 0:
            term = x
        else:
            shifted = pltpu.roll(x, (-off) % n_rows, axis=0)
            valid = (row < n_rows - off) if off > 0 else (row >= -off)
            term = jnp.where(valid, shifted, 0.0)
        term = term * w[k:k + 1, :]
        out = term if out is None else out + term
    return out


def _ssd_kernel(*refs, seq_len, has_h0, n_alias, emit_state):
    (z_ref, xr_ref, br_ref, cr_ref, dt_ref, cwx_ref, cwb_ref, cwc_ref, cbx_ref, cbb_ref, cbc_ref,
     dtb_ref, alog_ref, dsk_ref, gn_ref, sel_ref, e2_ref, tri_ref, trit_ref) = refs[:19]
    pos = 19
    h0_ref = None
    if has_h0:
        h0_ref = refs[pos]
        pos += 1
    pos += n_alias
    yn_ref = refs[pos]
    pos += 1
    hfin_ref = None
    if emit_state:
        hfin_ref = refs[pos]
        pos += 1
    xa_sc, ba_sc, ca_sc, dts_sc, a_sc, ht_sc, yg_sc, y_sc, ssq_sc = refs[pos:pos + 9]

    gi = pl.program_id(1)
    q = SSD_CHUNK
    nc = seq_len // q
    left = (SSD_CONV - 1) // 2
    hpg = HEADS_PER_GROUP
    gw = GROUP_WIDTH

    xa = _silu(_dwconv_rows(xr_ref[...], cwx_ref[...], left) + cbx_ref[...])
    xa_sc[...] = xa
    ba_sc[...] = _silu(_dwconv_rows(br_ref[...], cwb_ref[...], left) + cbb_ref[...])
    ca_sc[...] = _silu(_dwconv_rows(cr_ref[...], cwc_ref[...], left) + cbc_ref[...])
    yg_sc[...] = xa * dsk_ref[...]

    draw = dt_ref[...] + dtb_ref[...]
    dt_all = jnp.maximum(draw, 0.0) + jnp.log1p(jnp.exp(-jnp.abs(draw)))
    dts = _dot_r01(dt_all, sel_ref[...])
    dts_sc[...] = dts
    a_sc[...] = dts * (-jnp.exp(alog_ref[...]))

    for d in range(2):
        if has_h0:
            for p in range(hpg // 2):
                blk = h0_ref[d, 2 * p:2 * p + 2].reshape(2 * SSD_HEAD_DIM, SSD_STATE)
                ht_sc[d, :, p * LANES:(p + 1) * LANES] = blk.T
        else:
            ht_sc[d] = jnp.zeros((SSD_STATE, gw), F32)

    ri = lax.broadcasted_iota(jnp.int32, (q, q), 0)
    ci = lax.broadcasted_iota(jnp.int32, (q, q), 1)
    lane_lo = lax.broadcasted_iota(jnp.int32, (q, LANES), 1) < SSD_HEAD_DIM
    neg_inf = jnp.float32(-jnp.inf)

    def chunk_dir(c, d):
        rows = pl.ds(pl.multiple_of(c * q, q), q)
        a_c = a_sc[rows, :]
        if d == 0:
            cs = _dot_l01(tri_ref[...], a_c)
            tot = cs[q - 1:q, :]
            mask = ri >= ci
        else:
            cs = _dot_l01(trit_ref[...], a_c)
            tot = cs[0:1, :]
            mask = ri <= ci
        cst = cs.T
        stack = jnp.concatenate([
            dts_sc[rows, :].astype(BF16),
            jnp.exp(cs).astype(BF16),
            jnp.exp(tot - cs).astype(BF16),
            jnp.broadcast_to(jnp.exp(tot), (BF16_ROWS, LANES)).astype(BF16)], axis=0)
        ex = jnp.dot(stack, e2_ref[:, d * gw:(d + 1) * gw], preferred_element_type=F32)
        dtx, ecs_x, dte_x, ea_x = ex[0:q], ex[q:2 * q], ex[2 * q:3 * q], ex[3 * q:3 * q + 1]
        b_c = ba_sc[rows, :]
        c_bf = ca_sc[rows, :].astype(BF16)
        xdt = xa_sc[rows, :] * dtx
        g = lax.dot_general(c_bf, b_c.astype(BF16), (((1,), (1,)), ((), ())),
                            preferred_element_type=F32)
        ht = ht_sc[d]
        y_off = jnp.dot(c_bf, ht.astype(BF16), preferred_element_type=F32) * ecs_x
        pieces = []
        for p in range(hpg // 2):
            ms = []
            for jj in range(2):
                r = d * hpg + 2 * p + jj
                diff = cs[:, r:r + 1] - cst[r:r + 1, :]
                ms.append((g * jnp.exp(jnp.where(mask, diff, neg_inf))).astype(BF16))
            xp = xdt[:, p * LANES:(p + 1) * LANES]
            rhs = jnp.concatenate([jnp.where(lane_lo, xp, 0.0), jnp.where(lane_lo, 0.0, xp)], axis=0)
            pieces.append(jnp.dot(jnp.concatenate(ms, axis=1), rhs.astype(BF16),
                                  preferred_element_type=F32))
        s_t = jnp.dot(b_c.T.astype(BF16), (xdt * dte_x).astype(BF16), preferred_element_type=F32)
        ht_sc[d] = ht * ea_x + s_t
        yg_sc[rows, :] += jnp.concatenate(pieces, axis=1) + y_off

    def body(c, carry):
        chunk_dir(c, 0)
        chunk_dir(nc - 1 - c, 1)
        return carry

    lax.fori_loop(0, nc, body, 0, unroll=2)

    yz = yg_sc[...] * _silu(z_ref[...])
    y_sc[gi] = yz
    part = jnp.broadcast_to(jnp.sum(yz * yz, axis=-1, keepdims=True), ssq_sc.shape)

    @pl.when(gi == 0)
    def _():
        ssq_sc[...] = part

    @pl.when(gi > 0)
    def _():
        ssq_sc[...] += part

    @pl.when(gi == SSD_GROUPS - 1)
    def _():
        inv = lax.rsqrt(ssq_sc[:, 0:1] * (1.0 / SSD_INNER) + NORM_EPS)
        for gg in range(SSD_GROUPS):
            cols = slice(gg * gw, (gg + 1) * gw)
            yn_ref[:, cols] = (y_sc[gg] * inv * gn_ref[:, cols]).astype(BF16)

    if emit_state:
        for d in range(2):
            for p in range(hpg // 2):
                blk = ht_sc[d, :, p * LANES:(p + 1) * LANES].T
                hfin_ref[d, 2 * p:2 * p + 2] = blk.reshape(2, SSD_HEAD_DIM, SSD_STATE)


def _ssd_constants():
    hpg = HEADS_PER_GROUP
    sel = np.zeros((SSD_GROUPS, LANES, LANES), np.float32)
    for g in range(SSD_GROUPS):
        for d in range(2):
            for j in range(hpg):
                sel[g, d * SSD_HEADS + hpg * g + j, d * hpg + j] = 1.0
    e2 = np.zeros((LANES, 2 * GROUP_WIDTH), np.float32)
    for d in range(2):
        for j in range(hpg):
            lo = d * GROUP_WIDTH + j * SSD_HEAD_DIM
            e2[d * hpg + j, lo:lo + SSD_HEAD_DIM] = 1.0
    tri = np.tril(np.ones((SSD_CHUNK, SSD_CHUNK), np.float32))
    return (jnp.asarray(sel, BF16), jnp.asarray(e2, BF16), jnp.asarray(tri, BF16),
            jnp.asarray(tri.T, BF16))


def _ssd(head, dt, conv_w, conv_b3, dtb, alog_g, dsk_g, norm_g3, consts, l, *, row0, n_seq, seq_len,
         h0=None, prev_y=None, prev_state=None, state_shape=None):
    n_tok = head.shape[0]
    sel, e2, tri, trit = consts
    assert row0 % seq_len == 0 and seq_len % (2 * SSD_CHUNK) == 0
    blk0 = row0 // seq_len
    gw = GROUP_WIDTH
    has_h0 = h0 is not None
    emit_state = state_shape is not None
    xcol = SSD_INNER // gw
    bcol = (SSD_INNER + SSD_INNER) // LANES
    ccol = bcol + SSD_GROUPS
    cwb = SSD_INNER // LANES
    cwc = cwb + SSD_GROUPS
    in_specs = [
        pl.BlockSpec((seq_len, gw), lambda b, g: (blk0 + b, g)),
        pl.BlockSpec((seq_len, gw), lambda b, g: (blk0 + b, xcol + g)),
        pl.BlockSpec((seq_len, LANES), lambda b, g: (blk0 + b, bcol + g)),
        pl.BlockSpec((seq_len, LANES), lambda b, g: (blk0 + b, ccol + g)),
        pl.BlockSpec((seq_len, LANES), lambda b, g: (blk0 + b, 0)),
        pl.BlockSpec((None, SSD_CONV, gw), lambda b, g: (l, 0, g)),
        pl.BlockSpec((None, SSD_CONV, LANES), lambda b, g: (l, 0, cwb + g)),
        pl.BlockSpec((None, SSD_CONV, LANES), lambda b, g: (l, 0, cwc + g)),
        pl.BlockSpec((None, 1, gw), lambda b, g: (l, 0, g)),
        pl.BlockSpec((None, 1, LANES), lambda b, g: (l, 0, cwb + g)),
        pl.BlockSpec((None, 1, LANES), lambda b, g: (l, 0, cwc + g)),
        pl.BlockSpec((None, 1, LANES), lambda b, g: (l, 0, 0)),
        pl.BlockSpec((None, None, 1, LANES), lambda b, g: (l, g, 0, 0)),
        pl.BlockSpec((None, None, 1, gw), lambda b, g: (l, g, 0, 0)),
        pl.BlockSpec((None, 1, SSD_INNER), lambda b, g: (l, 0, 0)),
        pl.BlockSpec((None, LANES, LANES), lambda b, g: (g, 0, 0)),
        pl.BlockSpec((LANES, 2 * gw), lambda b, g: (0, 0)),
        pl.BlockSpec((SSD_CHUNK, SSD_CHUNK), lambda b, g: (0, 0)),
        pl.BlockSpec((SSD_CHUNK, SSD_CHUNK), lambda b, g: (0, 0)),
    ]
    args = [head, head, head, head, dt, conv_w, conv_w, conv_w, conv_b3, conv_b3, conv_b3,
            dtb, alog_g, dsk_g, norm_g3, sel, e2, tri, trit]
    if has_h0:
        in_specs.append(pl.BlockSpec((None, None, 2, HEADS_PER_GROUP, SSD_HEAD_DIM, SSD_STATE),
                                     lambda b, g: (b, l, 0, g, 0, 0)))
        args.append(h0)
    aliases = {}
    if prev_y is not None:
        in_specs.append(pl.BlockSpec(memory_space=pl.ANY))
        aliases[len(args)] = 0
        args.append(prev_y)
    if prev_state is not None:
        assert emit_state
        in_specs.append(pl.BlockSpec(memory_space=pl.ANY))
        aliases[len(args)] = 1
        args.append(prev_state)
    out_specs = [pl.BlockSpec((seq_len, SSD_INNER), lambda b, g: (blk0 + b, 0))]
    out_shape = [jax.ShapeDtypeStruct((n_tok, SSD_INNER), BF16)]
    if emit_state:
        out_specs.append(pl.BlockSpec((None, None, 2, HEADS_PER_GROUP, SSD_HEAD_DIM, SSD_STATE),
                                      lambda b, g: (b, l, 0, g, 0, 0)))
        out_shape.append(jax.ShapeDtypeStruct(state_shape, F32))
    kern = functools.partial(_ssd_kernel, seq_len=seq_len, has_h0=has_h0, n_alias=len(aliases),
                             emit_state=emit_state)
    return pl.pallas_call(
        kern,
        grid=(n_seq, SSD_GROUPS),
        in_specs=in_specs,
        out_specs=out_specs,
        out_shape=out_shape,
        input_output_aliases=aliases,
        scratch_shapes=[
            pltpu.VMEM((seq_len, gw), F32),
            pltpu.VMEM((seq_len, LANES), F32),
            pltpu.VMEM((seq_len, LANES), F32),
            pltpu.VMEM((seq_len, LANES), F32),
            pltpu.VMEM((seq_len, LANES), F32),
            pltpu.VMEM((2, SSD_STATE, gw), F32),
            pltpu.VMEM((seq_len, gw), F32),
            pltpu.VMEM((SSD_GROUPS, seq_len, gw), F32),
            pltpu.VMEM((seq_len, LANES), F32),
        ],
        compiler_params=_cparams(2),
        name="ssd_lat" if has_h0 else "ssd_ctx",
    )(*args)


def _scft_kernel(*refs, n_alias, n_grp):
    b_ref, c_ref, x_ref, f_ref, cw_ref, cl_ref, sl_ref, cc_ref, sc_ref = refs[:9]
    pos = 9 + n_alias
    ysc_ref, yft_ref = refs[pos], refs[pos + 1]
    v = c_ref[...] * x_ref[...]
    ysc_ref[...] = (b_ref[...] * _dwconv_rows(v, cw_ref[...], (SC_CONV - 1) // 2)).astype(BF16)
    tw = FT_GROUP_DIM
    for gg in range(n_grp):
        cols = slice(gg * tw, (gg + 1) * tw)
        u = f_ref[:, cols].astype(BF16)
        p = jnp.dot(u, cc_ref[...], preferred_element_type=F32).astype(BF16)
        s = jnp.dot(u, sc_ref[...], preferred_element_type=F32).astype(BF16)
        yft_ref[:, cols] = (jnp.dot(cl_ref[...], p, preferred_element_type=F32)
                            - jnp.dot(sl_ref[...], s, preferred_element_type=F32)).astype(BF16)


def _dft_mats(n):
    k = np.arange(n)
    ang = 2.0 * np.pi * ((k[:, None] * k[None, :]) % n) / n
    scale = 1.0 / math.sqrt(n)
    return jnp.asarray(np.cos(ang) * scale, BF16), jnp.asarray(np.sin(ang) * scale, BF16)


def _scft(tail, sc_conv_w, l, *, row0, n_seq, seq_len, n_grp, prev=None):
    n_tok = tail.shape[0]
    tw = n_grp * FT_GROUP_DIM
    assert row0 % seq_len == 0 and SC_WIDTH % tw == 0 and HEAD_COLS % tw == 0
    blk0 = row0 // seq_len
    nt = SC_WIDTH // tw
    c0 = HEAD_COLS // tw
    cl, sl = _dft_mats(seq_len)
    cc, sc = _dft_mats(FT_GROUP_DIM)
    in_specs = [
        pl.BlockSpec((seq_len, tw), lambda b, j: (blk0 + b, c0 + j)),
        pl.BlockSpec((seq_len, tw), lambda b, j: (blk0 + b, c0 + nt + j)),
        pl.BlockSpec((seq_len, tw), lambda b, j: (blk0 + b, c0 + 2 * nt + j)),
        pl.BlockSpec((seq_len, tw), lambda b, j: (blk0 + b, c0 + 3 * nt + j)),
        pl.BlockSpec((None, SC_CONV, tw), lambda b, j: (l, 0, j)),
        pl.BlockSpec((seq_len, seq_len), lambda b, j: (0, 0)),
        pl.BlockSpec((seq_len, seq_len), lambda b, j: (0, 0)),
        pl.BlockSpec((FT_GROUP_DIM, FT_GROUP_DIM), lambda b, j: (0, 0)),
        pl.BlockSpec((FT_GROUP_DIM, FT_GROUP_DIM), lambda b, j: (0, 0)),
    ]
    args = [tail, tail, tail, tail, sc_conv_w, cl, sl, cc, sc]
    aliases = {}
    if prev is not None:
        in_specs += [pl.BlockSpec(memory_space=pl.ANY), pl.BlockSpec(memory_space=pl.ANY)]
        aliases = {len(args): 0, len(args) + 1: 1}
        args += list(prev)
    kern = functools.partial(_scft_kernel, n_alias=len(aliases), n_grp=n_grp)
    return pl.pallas_call(
        kern,
        grid=(n_seq, nt),
        in_specs=in_specs,
        out_specs=[
            pl.BlockSpec((seq_len, tw), lambda b, j: (blk0 + b, j)),
            pl.BlockSpec((seq_len, tw), lambda b, j: (blk0 + b, j)),
        ],
        out_shape=[
            jax.ShapeDtypeStruct((n_tok, SC_WIDTH), BF16),
            jax.ShapeDtypeStruct((n_tok, FT_WIDTH), BF16),
        ],
        input_output_aliases=aliases,
        compiler_params=_cparams(2),
        name="scft_lat" if prev is not None else "scft_ctx",
    )(*args)


def _mixout_kernel(a0_ref, a1_ref, a2_ref, g0_ref, g1_ref, g2_ref, w0_ref, w1_ref, w2_ref, wo_ref,
                   x_ref, mod_ref, ng_ref, o_ref, un_ref, acc_sc, *, nk):
    k = pl.program_id(1)
    te = o_ref.shape[0]

    @pl.when(k == 0)
    def _():
        acc_sc[...] = jnp.zeros_like(acc_sc)

    @pl.when(k < nk)
    def _():
        m = jax.nn.sigmoid(g0_ref[...]) * jnp.dot(a0_ref[...], w0_ref[...].astype(BF16),
                                                   preferred_element_type=F32)
        m += jax.nn.sigmoid(g1_ref[...]) * jnp.dot(a1_ref[...], w1_ref[...].astype(BF16),
                                                    preferred_element_type=F32)
        m += jax.nn.sigmoid(g2_ref[...]) * jnp.dot(a2_ref[...], w2_ref[...].astype(BF16),
                                                    preferred_element_type=F32)
        acc_sc[...] += jnp.dot(m.astype(BF16), wo_ref[...].astype(BF16), preferred_element_type=F32)

    @pl.when(k >= nk)
    def _():
        r0 = pl.multiple_of((k - nk) * te, te)
        y = _rms(acc_sc[pl.ds(r0, te), :], ng_ref[3:4, :])
        x_new = x_ref[...] + mod_ref[5:6, :] * y
        o_ref[...] = x_new
        un_ref[...] = _norm_mod(x_new, ng_ref[4:5, :], mod_ref[6:7, :], mod_ref[7:8, :]).astype(BF16)


def _mixout(yssd, ysc, yft, tail, w_br_ssd, w_br_sc, w_br_ft, w_out, x, mod4, norm_g, l, rows,
            tm=1024, tk=256):
    n_tok, d = x.shape
    nk = d // tk
    gate0 = (HEAD_COLS + 3 * SC_WIDTH + FT_WIDTH) // tk
    per = d // tk
    grp = lambda i: rows.group_of_tile(i, tm)
    kk = lambda k: jnp.minimum(k, nk - 1)
    te = EPILOGUE_ROWS
    ne = tm // te
    erow = lambda i, k: (ne * i + jnp.clip(k - nk, 0, ne - 1), 0)
    kern = functools.partial(_mixout_kernel, nk=nk)
    return pl.pallas_call(
        kern,
        grid=(n_tok // tm, nk + ne),
        in_specs=[
            pl.BlockSpec((tm, SSD_INNER), lambda i, k: (i, 0)),
            pl.BlockSpec((tm, SC_WIDTH), lambda i, k: (i, 0)),
            pl.BlockSpec((tm, FT_WIDTH), lambda i, k: (i, 0)),
            pl.BlockSpec((tm, tk), lambda i, k: (i, gate0 + kk(k))),
            pl.BlockSpec((tm, tk), lambda i, k: (i, gate0 + per + kk(k))),
            pl.BlockSpec((tm, tk), lambda i, k: (i, gate0 + 2 * per + kk(k))),
            pl.BlockSpec((None, SSD_INNER, tk), lambda i, k: (l, 0, kk(k))),
            pl.BlockSpec((None, SC_WIDTH, tk), lambda i, k: (l, 0, kk(k))),
            pl.BlockSpec((None, FT_WIDTH, tk), lambda i, k: (l, 0, kk(k))),
            pl.BlockSpec((None, tk, d), lambda i, k: (l, kk(k), 0)),
            pl.BlockSpec((te, d), erow),
            pl.BlockSpec((None, None, N_MOD, d), lambda i, k: (l, grp(i), 0, 0)),
            pl.BlockSpec((None, 6, d), lambda i, k: (l, 0, 0)),
        ],
        out_specs=[pl.BlockSpec((te, d), erow), pl.BlockSpec((te, d), erow)],
        out_shape=[jax.ShapeDtypeStruct((n_tok, d), F32), jax.ShapeDtypeStruct((n_tok, d), BF16)],
        scratch_shapes=[pltpu.VMEM((tm, d), F32)],
        compiler_params=_cparams(2),
        name="mixout",
    )(yssd, ysc, yft, tail, tail, tail, w_br_ssd, w_br_sc, w_br_ft, w_out, x, mod4, norm_g)


def _grid_pos_emb(n_tok):
    rows = n_tok // GRID_W
    t = np.arange(rows * GRID_W)
    r = (t // GRID_W).astype(np.float32)[:, None]
    col = (t % GRID_W).astype(np.float32)[:, None]
    nf = D_MODEL // 4
    omega = (1.0 / (np.float32(POS_BASE) ** (np.arange(nf, dtype=np.float32) / np.float32(nf)))).astype(np.float32)
    ro = (r * omega).astype(np.float32).astype(np.float64)
    co = (col * omega).astype(np.float32).astype(np.float64)
    return np.concatenate([np.sin(ro), np.cos(ro), np.sin(co), np.cos(co)], axis=-1).astype(np.float32)


def _group_lanes(p, hpg=HEADS_PER_GROUP):
    depth = p.shape[0]
    t = p.reshape(depth, 2, SSD_GROUPS, hpg).transpose(0, 2, 1, 3).reshape(depth, SSD_GROUPS, 1, 2 * hpg)
    return jnp.pad(t, ((0, 0), (0, 0), (0, 0), (0, LANES - 2 * hpg)))


def kernel(x_prompt, x_sample, state_ssd, c, c_ctx, ada_w, ada_b, norm_g, ffn1_wgu, ffn1_wd, w_in,
           ssd_conv_w, ssd_conv_b, ssd_dt_bias, ssd_a_log, ssd_d, ssd_norm_g, sc_conv_w,
           w_br_ssd, w_br_sc, w_br_ft, w_out, ffn2_wgu, ffn2_wd):
    n_ctx, ctx_len, d = x_prompt.shape
    n_lat, lat_len, _ = x_sample.shape
    depth = ada_w.shape[0]
    rows = _Rows(n_ctx * ctx_len, n_lat, lat_len)

    c8 = jnp.concatenate([c_ctx[None, :], c, jnp.zeros((8 - 1 - n_lat, d), F32)], axis=0)
    mod4 = _modulation(c8, ada_w, ada_b)[:, :1 + n_lat].reshape(depth, 1 + n_lat, N_MOD, d)

    x, u = _embed(x_prompt.reshape(n_ctx * ctx_len, d), x_sample.reshape(n_lat * lat_len, d),
                  jnp.asarray(_grid_pos_emb(lat_len)), mod4, norm_g, rows)

    consts = _ssd_constants()
    w_in_t = jnp.swapaxes(w_in, 1, 2)
    conv_b3 = ssd_conv_b.reshape(depth, 1, SSD_XBC)
    dtb = jnp.pad(ssd_dt_bias.reshape(depth, 1, DT_WIDTH), ((0, 0), (0, 0), (0, LANES - DT_WIDTH)))
    alog_g = _group_lanes(ssd_a_log)
    dsk_g = jnp.repeat(ssd_d, SSD_HEAD_DIM, axis=-1).reshape(depth, SSD_GROUPS, 1, GROUP_WIDTH)
    ssd_norm_g3 = ssd_norm_g.reshape(depth, 1, SSD_INNER)
    state_shape = (n_ctx, depth, 2, SSD_HEADS, SSD_HEAD_DIM, SSD_STATE)

    states = None
    for l in range(depth):
        x, u = _ffn(x, u, mod4, norm_g, ffn1_wgu, ffn1_wd, l, 0, rows, "same")
        proj, dt = _inproj(u, w_in_t, l)
        tail = proj
        ssd_args = (proj, dt, ssd_conv_w, conv_b3, dtb, alog_g, dsk_g, ssd_norm_g3, consts, l)
        yssd, states = _ssd(*ssd_args, row0=0, n_seq=n_ctx, seq_len=ctx_len,
                            prev_state=states, state_shape=state_shape)
        (yssd,) = _ssd(*ssd_args, row0=rows.n_ctx_tok, n_seq=n_lat, seq_len=lat_len,
                       h0=state_ssd, prev_y=yssd)
        ysc, yft = _scft(tail, sc_conv_w, l, row0=0, n_seq=n_ctx, seq_len=ctx_len, n_grp=FT_GROUPS)
        ysc, yft = _scft(tail, sc_conv_w, l, row0=rows.n_ctx_tok, n_seq=n_lat, seq_len=lat_len,
                         n_grp=1, prev=(ysc, yft))
        x, u = _mixout(yssd, ysc, yft, tail, w_br_ssd, w_br_sc, w_br_ft, w_out, x, mod4, norm_g, l, rows)
        if l + 1 < depth:
            x, u = _ffn(x, u, mod4, norm_g, ffn2_wgu, ffn2_wd, l, 1, rows, "layer")
        else:
            (x,) = _ffn(x, u, mod4, norm_g, ffn2_wgu, ffn2_wd, l, 1, rows, None)

    y_prompt = x[:rows.n_ctx_tok].reshape(n_ctx, ctx_len, d)
    y_sample = x[rows.n_ctx_tok:].reshape(n_lat, lat_len, d)
    return (y_prompt, y_sample, states)
```

```python
import functools
import math

import numpy as np
import jax
import jax.numpy as jnp
from jax import lax
from jax.experimental import pallas as pl
from jax.experimental.pallas import tpu as pltpu

F32 = jnp.float32
BF16 = jnp.bfloat16

D_MODEL = 2048
DEPTH = 4
GRID_W = 64
POS_BASE = 10000.0
NORM_EPS = 1e-6
SSD_HEADS = 32
SSD_HEAD_DIM = 64
SSD_INNER = SSD_HEADS * SSD_HEAD_DIM
SSD_GROUPS = 4
SSD_STATE = 128
SSD_CONV = 4
SSD_CHUNK = 128
SSD_XBC = SSD_INNER + 2 * SSD_GROUPS * SSD_STATE
SC_WIDTH = 1024
SC_CONV = 3
FT_WIDTH = 1024
FT_GROUPS = 4
FT_GROUP_DIM = FT_WIDTH // FT_GROUPS
D_FF = 5504
N_BRANCH = 3
N_MOD = 9
HEADS_PER_GROUP = SSD_HEADS // SSD_GROUPS
GROUP_WIDTH = HEADS_PER_GROUP * SSD_HEAD_DIM
HEAD_COLS = SSD_INNER + SSD_XBC
DT_COL = HEAD_COLS
DT_WIDTH = 2 * SSD_HEADS
TAIL_COL = HEAD_COLS + DT_WIDTH
TAIL_COLS = 3 * SC_WIDTH + FT_WIDTH + N_BRANCH * D_MODEL
IN_COLS = TAIL_COL + TAIL_COLS

LANES = 128
BF16_ROWS = 16
VMEM_LIMIT_BYTES = 60 * 1024 * 1024


def _cparams(n_axes):
    return pltpu.CompilerParams(dimension_semantics=("arbitrary",) * n_axes,
                                vmem_limit_bytes=VMEM_LIMIT_BYTES)


def _silu(x):
    return x * jax.nn.sigmoid(x)


def _rms(x, g):
    ms = jnp.mean(x * x, axis=-1, keepdims=True)
    return x * lax.rsqrt(ms + NORM_EPS) * g


def _norm_mod(x, g, shift, scale):
    return _rms(x, g) * (1.0 + scale) + shift


def _split_bf16(x, n):
    parts = []
    r = x
    for i in range(n):
        p = r.astype(BF16)
        parts.append(p)
        if i + 1 < n:
            r = r - p.astype(F32)
    return parts


def _dot_r01(x, m01, n=3):
    acc = None
    for p in _split_bf16(x, n):
        t = jnp.dot(p, m01, preferred_element_type=F32)
        acc = t if acc is None else acc + t
    return acc


def _dot_l01(m01, x, n=3):
    acc = None
    for p in _split_bf16(x, n):
        t = jnp.dot(m01, p, preferred_element_type=F32)
        acc = t if acc is None else acc + t
    return acc


def _mod_kernel(c_ref, w_ref, b_ref, o_ref):
    s = _silu(c_ref[...]).astype(BF16)
    o_ref[...] = jnp.dot(s, w_ref[...].astype(BF16), preferred_element_type=F32) + b_ref[...]


def _modulation(c8, ada_w, ada_b, tn=1024):
    depth, d, n = ada_w.shape
    return pl.pallas_call(
        _mod_kernel,
        grid=(depth, n // tn),
        in_specs=[
            pl.BlockSpec((8, d), lambda l, j: (0, 0)),
            pl.BlockSpec((None, d, tn), lambda l, j: (l, 0, j)),
            pl.BlockSpec((None, 1, tn), lambda l, j: (l, 0, j)),
        ],
        out_specs=pl.BlockSpec((None, 8, tn), lambda l, j: (l, 0, j)),
        out_shape=jax.ShapeDtypeStruct((depth, 8, n), F32),
        compiler_params=_cparams(2),
        name="adaln_mod",
    )(c8, ada_w, ada_b.reshape(depth, 1, n))


class _Rows:
    def __init__(self, n_ctx_tok, n_lat, lat_len):
        self.n_ctx_tok = n_ctx_tok
        self.n_lat = n_lat
        self.lat_len = lat_len
        self.n_tok = n_ctx_tok + n_lat * lat_len

    def group_of_tile(self, i, tm):
        assert self.n_ctx_tok % tm == 0 and self.lat_len % tm == 0
        n_ctx_tiles = self.n_ctx_tok // tm
        per_lat = self.lat_len // tm
        return jnp.where(i < n_ctx_tiles, 0, 1 + (i - n_ctx_tiles) // per_lat)


def _single(block_shape, index_map):
    return pl.BlockSpec(block_shape, index_map, pipeline_mode=pl.Buffered(1))


FFN_TF = 512
FFN_SPLIT = 2
EPILOGUE_ROWS = 256


def _ffn_kernel(*refs, nk, mod_row, ng_row, next_kind, n_ctx_tiles):
    x_ref, u_ref, mod_ref, ng_ref, wg_ref, wu_ref, wd_ref = refs[:7]
    pos = 7
    if next_kind == "layer":
        modn_ref, ngn_ref = refs[pos:pos + 2]
        pos += 2
    o_ref = refs[pos]
    pos += 1
    if next_kind == "final":
        olat_ref = refs[pos]
    else:
        un_ref = refs[pos]
    pos += 1
    acc_sc, wup_sc, wdn_sc = refs[pos:pos + 3]
    k = pl.program_id(1)
    tf = FFN_TF
    te = o_ref.shape[0]

    @pl.when(k == 0)
    def _():
        acc_sc[...] = jnp.zeros_like(acc_sc)

    @pl.when(k < nk)
    def _():
        ts = tf // FFN_SPLIT
        for s in range(FFN_SPLIT):
            wup_sc[:, 2 * s * ts:(2 * s + 1) * ts] = wg_ref[:, s * ts:(s + 1) * ts].astype(BF16)
            wup_sc[:, (2 * s + 1) * ts:(2 * s + 2) * ts] = wu_ref[:, s * ts:(s + 1) * ts].astype(BF16)
        wdn_sc[...] = wd_ref[...].astype(BF16)

        overlap = nk * tf - D_FF
        assert overlap <= ts
        for s in range(FFN_SPLIT):
            h = jnp.dot(u_ref[...], wup_sc[:, 2 * s * ts:(2 * s + 2) * ts], preferred_element_type=F32)
            a = _silu(h[:, :ts]) * h[:, ts:]
            if s == 0:
                lane = lax.broadcasted_iota(jnp.int32, a.shape, 1)
                a = jnp.where(jnp.logical_and(k == nk - 1, lane < overlap), 0.0, a)
            acc_sc[...] += jnp.dot(a.astype(BF16), wdn_sc[s * ts:(s + 1) * ts, :], preferred_element_type=F32)

    @pl.when(k >= nk)
    def _():
        r0 = pl.multiple_of((k - nk) * te, te)
        y = _rms(acc_sc[pl.ds(r0, te), :], ng_ref[ng_row + 1:ng_row + 2, :])
        x_new = x_ref[...] + 0.5 * mod_ref[mod_row + 2:mod_row + 3, :] * y
        if next_kind == "final":
            is_ctx = pl.program_id(0) < n_ctx_tiles

            @pl.when(is_ctx)
            def _():
                o_ref[...] = x_new

            @pl.when(jnp.logical_not(is_ctx))
            def _():
                olat_ref[...] = x_new
        else:
            o_ref[...] = x_new
        if next_kind == "same":
            un = _norm_mod(x_new, ng_ref[ng_row + 2:ng_row + 3, :],
                           mod_ref[mod_row + 3:mod_row + 4, :], mod_ref[mod_row + 4:mod_row + 5, :])
            un_ref[...] = un.astype(BF16)
        elif next_kind == "layer":
            un = _norm_mod(x_new, ngn_ref[0:1, :], modn_ref[0:1, :], modn_ref[1:2, :])
            un_ref[...] = un.astype(BF16)


def _ffn(x, u, mod4, norm_g, wgu, wd, l, which, rows, next_kind, tm=1024):
    n_tok, d = x.shape
    tf = FFN_TF
    nk = pl.cdiv(D_FF, tf)
    mod_row = 0 if which == 0 else 6
    ng_row = 0 if which == 0 else 4
    grp = lambda i: rows.group_of_tile(i, tm)
    assert tf % LANES == 0 and D_FF % LANES == 0
    col = lambda k, base=0: LANES * (base // LANES + jnp.minimum(k * (tf // LANES), (D_FF - tf) // LANES))
    assert next_kind in ("same", "layer", "final")
    kern = functools.partial(_ffn_kernel, nk=nk, mod_row=mod_row, ng_row=ng_row, next_kind=next_kind,
                             n_ctx_tiles=rows.n_ctx_tok // tm)
    te = EPILOGUE_ROWS
    ne = tm // te
    erow = lambda i, k: (ne * i + jnp.clip(k - nk, 0, ne - 1), 0)
    in_specs = [
        pl.BlockSpec((te, d), erow),
        pl.BlockSpec((tm, d), lambda i, k: (i, 0)),
        pl.BlockSpec((None, None, N_MOD, d), lambda i, k: (l, grp(i), 0, 0)),
        pl.BlockSpec((None, 6, d), lambda i, k: (l, 0, 0)),
        pl.BlockSpec((pl.squeezed, pl.Element(d), pl.Element(tf)), lambda i, k: (l, 0, col(k))),
        pl.BlockSpec((pl.squeezed, pl.Element(d), pl.Element(tf)), lambda i, k: (l, 0, col(k, D_FF))),
        pl.BlockSpec((pl.squeezed, pl.Element(tf), pl.Element(d)), lambda i, k: (l, col(k), 0)),
    ]
    args = [x, u, mod4, norm_g, wgu, wgu, wd]
    if next_kind == "layer":
        in_specs += [pl.BlockSpec((None, None, N_MOD, d), lambda i, k: (l + 1, grp(i), 0, 0)),
                     pl.BlockSpec((None, 6, d), lambda i, k: (l + 1, 0, 0))]
        args += [mod4, norm_g]
    if next_kind == "final":
        n_ctx_blocks = rows.n_ctx_tok // te
        eblk = lambda i, k: ne * i + jnp.clip(k - nk, 0, ne - 1)
        out_specs = [pl.BlockSpec((te, d), lambda i, k: (jnp.minimum(eblk(i, k), n_ctx_blocks - 1), 0)),
                     pl.BlockSpec((te, d), lambda i, k: (jnp.maximum(eblk(i, k) - n_ctx_blocks, 0), 0))]
        out_shape = [jax.ShapeDtypeStruct((rows.n_ctx_tok, d), F32),
                     jax.ShapeDtypeStruct((n_tok - rows.n_ctx_tok, d), F32)]
    else:
        out_specs = [pl.BlockSpec((te, d), erow), pl.BlockSpec((te, d), erow)]
        out_shape = [jax.ShapeDtypeStruct((n_tok, d), F32), jax.ShapeDtypeStruct((n_tok, d), BF16)]
    return pl.pallas_call(
        kern,
        grid=(n_tok // tm, nk + ne),
        in_specs=in_specs,
        out_specs=out_specs,
        out_shape=out_shape,
        scratch_shapes=[
            pltpu.VMEM((tm, d), F32),
            pltpu.VMEM((d, 2 * tf), BF16),
            pltpu.VMEM((tf, d), BF16),
        ],
        compiler_params=_cparams(2),
        name="ffn",
    )(*args)


def _embed_kernel(xp_ref, xs_ref, pe_ref, mod_ref, ng_ref, x_ref, u_ref, *, n_ctx_tiles):
    i = pl.program_id(0)

    @pl.when(i < n_ctx_tiles)
    def _():
        x_ref[...] = xp_ref[...]

    @pl.when(i >= n_ctx_tiles)
    def _():
        x_ref[...] = xs_ref[...] + pe_ref[...]

    u_ref[...] = _norm_mod(x_ref[...], ng_ref[0:1, :], mod_ref[0:1, :], mod_ref[1:2, :]).astype(BF16)


def _embed(xp, xs, pe, mod4, norm_g, rows, tm=256):
    d = xp.shape[1]
    assert rows.lat_len % tm == 0 and rows.n_ctx_tok % tm == 0
    n_ctx_tiles = rows.n_ctx_tok // tm
    pe_tiles = rows.lat_len // tm
    grp = lambda i: rows.group_of_tile(i, tm)
    kern = functools.partial(_embed_kernel, n_ctx_tiles=n_ctx_tiles)
    return pl.pallas_call(
        kern,
        grid=(rows.n_tok // tm,),
        in_specs=[
            pl.BlockSpec((tm, d), lambda i: (jnp.minimum(i, n_ctx_tiles - 1), 0)),
            pl.BlockSpec((tm, d), lambda i: (jnp.maximum(i - n_ctx_tiles, 0), 0)),
            pl.BlockSpec((tm, d), lambda i: (jnp.maximum(i - n_ctx_tiles, 0) % pe_tiles, 0)),
            pl.BlockSpec((None, None, N_MOD, d), lambda i: (0, grp(i), 0, 0)),
            pl.BlockSpec((None, 6, d), lambda i: (0, 0, 0)),
        ],
        out_specs=[pl.BlockSpec((tm, d), lambda i: (i, 0)), pl.BlockSpec((tm, d), lambda i: (i, 0))],
        out_shape=[jax.ShapeDtypeStruct((rows.n_tok, d), F32), jax.ShapeDtypeStruct((rows.n_tok, d), BF16)],
        compiler_params=_cparams(1),
        name="embed",
    )(xp, xs, pe, mod4, norm_g)


def _dot_nt(a, b):
    return lax.dot_general(a, b, (((1,), (1,)), ((), ())), preferred_element_type=F32)


def _inproj_kernel(u_ref, wa_ref, wb_ref, op_ref, odt_ref, w_sc, *, n_head, tm):
    j = pl.program_id(0)
    m = pl.program_id(1)
    tn = w_sc.shape[0]
    off = DT_WIDTH

    @pl.when(jnp.logical_and(m == 0, j < n_head))
    def _():
        w_sc[...] = wa_ref[...].astype(BF16)

    @pl.when(jnp.logical_and(m == 0, j == n_head))
    def _():
        w_sc[:off, :] = wb_ref[...].astype(BF16)

    @pl.when(jnp.logical_and(m == 0, j > n_head))
    def _():
        w_sc[:tn - off, :] = wa_ref[off:, :].astype(BF16)
        w_sc[tn - off:, :] = wb_ref[...].astype(BF16)

    u = u_ref[pl.ds(pl.multiple_of(m * tm, tm), tm), :]

    @pl.when(j != n_head)
    def _():
        op_ref[...] = _dot_nt(u, w_sc[...])

    @pl.when(j == n_head)
    def _():
        odt_ref[...] = _dot_nt(u, w_sc[:LANES, :])


def _inproj(u, w_in_t, l, tm=2048, tn=512):
    n_tok, d = u.shape
    off = DT_WIDTH
    assert TAIL_COL % tn == off and HEAD_COLS % tn == 0 and TAIL_COLS % tn == 0 and tn % off == 0
    n_head = HEAD_COLS // tn
    n_tail = TAIL_COLS // tn
    n_m = n_tok // tm
    sub = tn // off
    proj_j = lambda j: jnp.where(j < n_head, j, jnp.where(j == n_head, n_head - 1, j - 1))
    proj_m = lambda j, m: jnp.where(j == n_head, n_m - 1, m)
    dt_m = lambda j, m: jnp.where(j < n_head, 0, jnp.where(j == n_head, m, n_m - 1))
    wide_j = lambda j: jnp.where(j <= n_head, jnp.minimum(j, n_head), j - 1)
    narrow_j = lambda j: jnp.where(j <= n_head, DT_COL // off, j * sub)
    kern = functools.partial(_inproj_kernel, n_head=n_head, tm=tm)
    return pl.pallas_call(
        kern,
        grid=(n_head + 1 + n_tail, n_m),
        in_specs=[
            _single((n_tok, d), lambda j, m: (0, 0)),
            pl.BlockSpec((None, tn, d), lambda j, m: (l, wide_j(j), 0)),
            pl.BlockSpec((None, off, d), lambda j, m: (l, narrow_j(j), 0)),
        ],
        out_specs=[
            pl.BlockSpec((tm, tn), lambda j, m: (proj_m(j, m), proj_j(j))),
            pl.BlockSpec((tm, LANES), lambda j, m: (dt_m(j, m), 0)),
        ],
        out_shape=[
            jax.ShapeDtypeStruct((n_tok, HEAD_COLS + TAIL_COLS), F32),
            jax.ShapeDtypeStruct((n_tok, LANES), F32),
        ],
        scratch_shapes=[pltpu.VMEM((tn, d), BF16)],
        compiler_params=_cparams(2),
        name="inproj",
    )(u, w_in_t, w_in_t)


def _dwconv_rows(x, w, left):
    n_rows = x.shape[0]
    row = lax.broadcasted_iota(jnp.int32, x.shape, 0)
    out = None
    for k in range(w.shape[0]):
        off = k - left
        if off == 0:
            term = x
        else:
            shifted = pltpu.roll(x, (-off) % n_rows, axis=0)
            valid = (row < n_rows - off) if off > 0 else (row >= -off)
            term = jnp.where(valid, shifted, 0.0)
        term = term * w[k:k + 1, :]
        out = term if out is None else out + term
    return out


def _ssd_kernel(*refs, seq_len, has_h0, n_alias, emit_state):
    (z_ref, xr_ref, br_ref, cr_ref, dt_ref, cwx_ref, cwb_ref, cwc_ref, cbx_ref, cbb_ref, cbc_ref,
     dtb_ref, alog_ref, dsk_ref, gn_ref, sel_ref, e2_ref, tri_ref, trit_ref) = refs[:19]
    pos = 19
    h0_ref = None
    if has_h0:
        h0_ref = refs[pos]
        pos += 1
    pos += n_alias
    yn_ref = refs[pos]
    pos += 1
    hfin_ref = None
    if emit_state:
        hfin_ref = refs[pos]
        pos += 1
    xa_sc, ba_sc, ca_sc, dts_sc, a_sc, ht_sc, yg_sc, y_sc, ssq_sc = refs[pos:pos + 9]

    gi = pl.program_id(1)
    q = SSD_CHUNK
    nc = seq_len // q
    left = (SSD_CONV - 1) // 2
    hpg = HEADS_PER_GROUP
    gw = GROUP_WIDTH

    xa = _silu(_dwconv_rows(xr_ref[...], cwx_ref[...], left) + cbx_ref[...])
    xa_sc[...] = xa
    ba_sc[...] = _silu(_dwconv_rows(br_ref[...], cwb_ref[...], left) + cbb_ref[...])
    ca_sc[...] = _silu(_dwconv_rows(cr_ref[...], cwc_ref[...], left) + cbc_ref[...])
    yg_sc[...] = xa * dsk_ref[...]

    draw = dt_ref[...] + dtb_ref[...]
    dt_all = jnp.maximum(draw, 0.0) + jnp.log1p(jnp.exp(-jnp.abs(draw)))
    dts = _dot_r01(dt_all, sel_ref[...])
    dts_sc[...] = dts
    a_sc[...] = dts * (-jnp.exp(alog_ref[...]))

    for d in range(2):
        if has_h0:
            for p in range(hpg // 2):
                blk = h0_ref[d, 2 * p:2 * p + 2].reshape(2 * SSD_HEAD_DIM, SSD_STATE)
                ht_sc[d, :, p * LANES:(p + 1) * LANES] = blk.T
        else:
            ht_sc[d] = jnp.zeros((SSD_STATE, gw), F32)

    ri = lax.broadcasted_iota(jnp.int32, (q, q), 0)
    ci = lax.broadcasted_iota(jnp.int32, (q, q), 1)
    lane_lo = lax.broadcasted_iota(jnp.int32, (q, LANES), 1) < SSD_HEAD_DIM
    neg_inf = jnp.float32(-jnp.inf)

    def chunk_dir(c, d):
        rows = pl.ds(pl.multiple_of(c * q, q), q)
        a_c = a_sc[rows, :]
        if d == 0:
            cs = _dot_l01(tri_ref[...], a_c)
            tot = cs[q - 1:q, :]
            mask = ri >= ci
        else:
            cs = _dot_l01(trit_ref[...], a_c)
            tot = cs[0:1, :]
            mask = ri <= ci
        cst = cs.T
        stack = jnp.concatenate([
            dts_sc[rows, :].astype(BF16),
            jnp.exp(cs).astype(BF16),
            jnp.exp(tot - cs).astype(BF16),
            jnp.broadcast_to(jnp.exp(tot), (BF16_ROWS, LANES)).astype(BF16)], axis=0)
        ex = jnp.dot(stack, e2_ref[:, d * gw:(d + 1) * gw], preferred_element_type=F32)
        dtx, ecs_x, dte_x, ea_x = ex[0:q], ex[q:2 * q], ex[2 * q:3 * q], ex[3 * q:3 * q + 1]
        b_c = ba_sc[rows, :]
        c_bf = ca_sc[rows, :].astype(BF16)
        xdt = xa_sc[rows, :] * dtx
        g = lax.dot_general(c_bf, b_c.astype(BF16), (((1,), (1,)), ((), ())),
                            preferred_element_type=F32)
        ht = ht_sc[d]
        y_off = jnp.dot(c_bf, ht.astype(BF16), preferred_element_type=F32) * ecs_x
        pieces = []
        for p in range(hpg // 2):
            ms = []
            for jj in range(2):
                r = d * hpg + 2 * p + jj
                diff = cs[:, r:r + 1] - cst[r:r + 1, :]
                ms.append((g * jnp.exp(jnp.where(mask, diff, neg_inf))).astype(BF16))
            xp = xdt[:, p * LANES:(p + 1) * LANES]
            rhs = jnp.concatenate([jnp.where(lane_lo, xp, 0.0), jnp.where(lane_lo, 0.0, xp)], axis=0)
            pieces.append(jnp.dot(jnp.concatenate(ms, axis=1), rhs.astype(BF16),
                                  preferred_element_type=F32))
        s_t = jnp.dot(b_c.T.astype(BF16), (xdt * dte_x).astype(BF16), preferred_element_type=F32)
        ht_sc[d] = ht * ea_x + s_t
        yg_sc[rows, :] += jnp.concatenate(pieces, axis=1) + y_off

    def body(c, carry):
        chunk_dir(c, 0)
        chunk_dir(nc - 1 - c, 1)
        return carry

    lax.fori_loop(0, nc, body, 0, unroll=2)

    yz = yg_sc[...] * _silu(z_ref[...])
    y_sc[gi] = yz
    part = jnp.broadcast_to(jnp.sum(yz * yz, axis=-1, keepdims=True), ssq_sc.shape)

    @pl.when(gi == 0)
    def _():
        ssq_sc[...] = part

    @pl.when(gi > 0)
    def _():
        ssq_sc[...] += part

    @pl.when(gi == SSD_GROUPS - 1)
    def _():
        inv = lax.rsqrt(ssq_sc[:, 0:1] * (1.0 / SSD_INNER) + NORM_EPS)
        for gg in range(SSD_GROUPS):
            cols = slice(gg * gw, (gg + 1) * gw)
            yn_ref[:, cols] = (y_sc[gg] * inv * gn_ref[:, cols]).astype(BF16)

    if emit_state:
        for d in range(2):
            for p in range(hpg // 2):
                blk = ht_sc[d, :, p * LANES:(p + 1) * LANES].T
                hfin_ref[d, 2 * p:2 * p + 2] = blk.reshape(2, SSD_HEAD_DIM, SSD_STATE)


def _ssd_constants():
    hpg = HEADS_PER_GROUP
    sel = np.zeros((SSD_GROUPS, LANES, LANES), np.float32)
    for g in range(SSD_GROUPS):
        for d in range(2):
            for j in range(hpg):
                sel[g, d * SSD_HEADS + hpg * g + j, d * hpg + j] = 1.0
    e2 = np.zeros((LANES, 2 * GROUP_WIDTH), np.float32)
    for d in range(2):
        for j in range(hpg):
            lo = d * GROUP_WIDTH + j * SSD_HEAD_DIM
            e2[d * hpg + j, lo:lo + SSD_HEAD_DIM] = 1.0
    tri = np.tril(np.ones((SSD_CHUNK, SSD_CHUNK), np.float32))
    return (jnp.asarray(sel, BF16), jnp.asarray(e2, BF16), jnp.asarray(tri, BF16),
            jnp.asarray(tri.T, BF16))


def _ssd(head, dt, conv_w, conv_b3, dtb, alog_g, dsk_g, norm_g3, consts, l, *, row0, n_seq, seq_len,
         h0=None, prev_y=None, prev_state=None, state_shape=None):
    n_tok = head.shape[0]
    sel, e2, tri, trit = consts
    assert row0 % seq_len == 0 and seq_len % (2 * SSD_CHUNK) == 0
    blk0 = row0 // seq_len
    gw = GROUP_WIDTH
    has_h0 = h0 is not None
    emit_state = state_shape is not None
    xcol = SSD_INNER // gw
    bcol = (SSD_INNER + SSD_INNER) // LANES
    ccol = bcol + SSD_GROUPS
    cwb = SSD_INNER // LANES
    cwc = cwb + SSD_GROUPS
    in_specs = [
        pl.BlockSpec((seq_len, gw), lambda b, g: (blk0 + b, g)),
        pl.BlockSpec((seq_len, gw), lambda b, g: (blk0 + b, xcol + g)),
        pl.BlockSpec((seq_len, LANES), lambda b, g: (blk0 + b, bcol + g)),
        pl.BlockSpec((seq_len, LANES), lambda b, g: (blk0 + b, ccol + g)),
        pl.BlockSpec((seq_len, LANES), lambda b, g: (blk0 + b, 0)),
        pl.BlockSpec((None, SSD_CONV, gw), lambda b, g: (l, 0, g)),
        pl.BlockSpec((None, SSD_CONV, LANES), lambda b, g: (l, 0, cwb + g)),
        pl.BlockSpec((None, SSD_CONV, LANES), lambda b, g: (l, 0, cwc + g)),
        pl.BlockSpec((None, 1, gw), lambda b, g: (l, 0, g)),
        pl.BlockSpec((None, 1, LANES), lambda b, g: (l, 0, cwb + g)),
        pl.BlockSpec((None, 1, LANES), lambda b, g: (l, 0, cwc + g)),
        pl.BlockSpec((None, 1, LANES), lambda b, g: (l, 0, 0)),
        pl.BlockSpec((None, None, 1, LANES), lambda b, g: (l, g, 0, 0)),
        pl.BlockSpec((None, None, 1, gw), lambda b, g: (l, g, 0, 0)),
        pl.BlockSpec((None, 1, SSD_INNER), lambda b, g: (l, 0, 0)),
        pl.BlockSpec((None, LANES, LANES), lambda b, g: (g, 0, 0)),
        pl.BlockSpec((LANES, 2 * gw), lambda b, g: (0, 0)),
        pl.BlockSpec((SSD_CHUNK, SSD_CHUNK), lambda b, g: (0, 0)),
        pl.BlockSpec((SSD_CHUNK, SSD_CHUNK), lambda b, g: (0, 0)),
    ]
    args = [head, head, head, head, dt, conv_w, conv_w, conv_w, conv_b3, conv_b3, conv_b3,
            dtb, alog_g, dsk_g, norm_g3, sel, e2, tri, trit]
    if has_h0:
        in_specs.append(pl.BlockSpec((None, None, 2, HEADS_PER_GROUP, SSD_HEAD_DIM, SSD_STATE),
                                     lambda b, g: (b, l, 0, g, 0, 0)))
        args.append(h0)
    aliases = {}
    if prev_y is not None:
        in_specs.append(pl.BlockSpec(memory_space=pl.ANY))
        aliases[len(args)] = 0
        args.append(prev_y)
    if prev_state is not None:
        assert emit_state
        in_specs.append(pl.BlockSpec(memory_space=pl.ANY))
        aliases[len(args)] = 1
        args.append(prev_state)
    out_specs = [pl.BlockSpec((seq_len, SSD_INNER), lambda b, g: (blk0 + b, 0))]
    out_shape = [jax.ShapeDtypeStruct((n_tok, SSD_INNER), BF16)]
    if emit_state:
        out_specs.append(pl.BlockSpec((None, None, 2, HEADS_PER_GROUP, SSD_HEAD_DIM, SSD_STATE),
                                      lambda b, g: (b, l, 0, g, 0, 0)))
        out_shape.append(jax.ShapeDtypeStruct(state_shape, F32))
    kern = functools.partial(_ssd_kernel, seq_len=seq_len, has_h0=has_h0, n_alias=len(aliases),
                             emit_state=emit_state)
    return pl.pallas_call(
        kern,
        grid=(n_seq, SSD_GROUPS),
        in_specs=in_specs,
        out_specs=out_specs,
        out_shape=out_shape,
        input_output_aliases=aliases,
        scratch_shapes=[
            pltpu.VMEM((seq_len, gw), F32),
            pltpu.VMEM((seq_len, LANES), F32),
            pltpu.VMEM((seq_len, LANES), F32),
            pltpu.VMEM((seq_len, LANES), F32),
            pltpu.VMEM((seq_len, LANES), F32),
            pltpu.VMEM((2, SSD_STATE, gw), F32),
            pltpu.VMEM((seq_len, gw), F32),
            pltpu.VMEM((SSD_GROUPS, seq_len, gw), F32),
            pltpu.VMEM((seq_len, LANES), F32),
        ],
        compiler_params=_cparams(2),
        name="ssd_lat" if has_h0 else "ssd_ctx",
    )(*args)


def _scft_kernel(*refs, n_alias, n_grp):
    b_ref, c_ref, x_ref, f_ref, cw_ref, cl_ref, sl_ref, cc_ref, sc_ref = refs[:9]
    pos = 9 + n_alias
    ysc_ref, yft_ref = refs[pos], refs[pos + 1]
    v = c_ref[...] * x_ref[...]
    ysc_ref[...] = (b_ref[...] * _dwconv_rows(v, cw_ref[...], (SC_CONV - 1) // 2)).astype(BF16)
    tw = FT_GROUP_DIM
    for gg in range(n_grp):
        cols = slice(gg * tw, (gg + 1) * tw)
        u = f_ref[:, cols].astype(BF16)
        p = jnp.dot(u, cc_ref[...], preferred_element_type=F32).astype(BF16)
        s = jnp.dot(u, sc_ref[...], preferred_element_type=F32).astype(BF16)
        yft_ref[:, cols] = (jnp.dot(cl_ref[...], p, preferred_element_type=F32)
                            - jnp.dot(sl_ref[...], s, preferred_element_type=F32)).astype(BF16)


def _dft_mats(n):
    k = np.arange(n)
    ang = 2.0 * np.pi * ((k[:, None] * k[None, :]) % n) / n
    scale = 1.0 / math.sqrt(n)
    return jnp.asarray(np.cos(ang) * scale, BF16), jnp.asarray(np.sin(ang) * scale, BF16)


def _scft(tail, sc_conv_w, l, *, row0, n_seq, seq_len, n_grp, prev=None):
    n_tok = tail.shape[0]
    tw = n_grp * FT_GROUP_DIM
    assert row0 % seq_len == 0 and SC_WIDTH % tw == 0 and HEAD_COLS % tw == 0
    blk0 = row0 // seq_len
    nt = SC_WIDTH // tw
    c0 = HEAD_COLS // tw
    cl, sl = _dft_mats(seq_len)
    cc, sc = _dft_mats(FT_GROUP_DIM)
    in_specs = [
        pl.BlockSpec((seq_len, tw), lambda b, j: (blk0 + b, c0 + j)),
        pl.BlockSpec((seq_len, tw), lambda b, j: (blk0 + b, c0 + nt + j)),
        pl.BlockSpec((seq_len, tw), lambda b, j: (blk0 + b, c0 + 2 * nt + j)),
        pl.BlockSpec((seq_len, tw), lambda b, j: (blk0 + b, c0 + 3 * nt + j)),
        pl.BlockSpec((None, SC_CONV, tw), lambda b, j: (l, 0, j)),
        pl.BlockSpec((seq_len, seq_len), lambda b, j: (0, 0)),
        pl.BlockSpec((seq_len, seq_len), lambda b, j: (0, 0)),
        pl.BlockSpec((FT_GROUP_DIM, FT_GROUP_DIM), lambda b, j: (0, 0)),
        pl.BlockSpec((FT_GROUP_DIM, FT_GROUP_DIM), lambda b, j: (0, 0)),
    ]
    args = [tail, tail, tail, tail, sc_conv_w, cl, sl, cc, sc]
    aliases = {}
    if prev is not None:
        in_specs += [pl.BlockSpec(memory_space=pl.ANY), pl.BlockSpec(memory_space=pl.ANY)]
        aliases = {len(args): 0, len(args) + 1: 1}
        args += list(prev)
    kern = functools.partial(_scft_kernel, n_alias=len(aliases), n_grp=n_grp)
    return pl.pallas_call(
        kern,
        grid=(n_seq, nt),
        in_specs=in_specs,
        out_specs=[
            pl.BlockSpec((seq_len, tw), lambda b, j: (blk0 + b, j)),
            pl.BlockSpec((seq_len, tw), lambda b, j: (blk0 + b, j)),
        ],
        out_shape=[
            jax.ShapeDtypeStruct((n_tok, SC_WIDTH), BF16),
            jax.ShapeDtypeStruct((n_tok, FT_WIDTH), BF16),
        ],
        input_output_aliases=aliases,
        compiler_params=_cparams(2),
        name="scft_lat" if prev is not None else "scft_ctx",
    )(*args)


def _mixout_kernel(a0_ref, a1_ref, a2_ref, g0_ref, g1_ref, g2_ref, w0_ref, w1_ref, w2_ref, wo_ref,
                   x_ref, mod_ref, ng_ref, o_ref, un_ref, acc_sc, m_sc, *, nk):
    k = pl.program_id(1)
    te = o_ref.shape[0]

    def merge():
        m = jax.nn.sigmoid(g0_ref[...]) * jnp.dot(a0_ref[...], w0_ref[...].astype(BF16),
                                                   preferred_element_type=F32)
        m += jax.nn.sigmoid(g1_ref[...]) * jnp.dot(a1_ref[...], w1_ref[...].astype(BF16),
                                                    preferred_element_type=F32)
        m += jax.nn.sigmoid(g2_ref[...]) * jnp.dot(a2_ref[...], w2_ref[...].astype(BF16),
                                                    preferred_element_type=F32)
        m_sc[k % 2] = m.astype(BF16)

    def project():
        return jnp.dot(m_sc[(k + 1) % 2], wo_ref[...].astype(BF16), preferred_element_type=F32)

    @pl.when(k == 0)
    def _():
        merge()

    @pl.when(k == 1)
    def _():
        acc_sc[...] = project()
        merge()

    @pl.when(jnp.logical_and(k > 1, k < nk))
    def _():
        acc_sc[...] += project()
        merge()

    @pl.when(k == nk)
    def _():
        acc_sc[...] += project()

    @pl.when(k >= nk)
    def _():
        r0 = pl.multiple_of((k - nk) * te, te)
        y = _rms(acc_sc[pl.ds(r0, te), :], ng_ref[3:4, :])
        x_new = x_ref[...] + mod_ref[5:6, :] * y
        o_ref[...] = x_new
        un_ref[...] = _norm_mod(x_new, ng_ref[4:5, :], mod_ref[6:7, :], mod_ref[7:8, :]).astype(BF16)


def _mixout(yssd, ysc, yft, tail, w_br_ssd, w_br_sc, w_br_ft, w_out, x, mod4, norm_g, l, rows,
            tm=1024, tk=256):
    n_tok, d = x.shape
    nk = d // tk
    gate0 = (HEAD_COLS + 3 * SC_WIDTH + FT_WIDTH) // tk
    per = d // tk
    grp = lambda i: rows.group_of_tile(i, tm)
    kk = lambda k: jnp.minimum(k, nk - 1)
    te = EPILOGUE_ROWS
    ne = tm // te
    erow = lambda i, k: (ne * i + jnp.clip(k - nk, 0, ne - 1), 0)
    kern = functools.partial(_mixout_kernel, nk=nk)
    return pl.pallas_call(
        kern,
        grid=(n_tok // tm, nk + ne),
        in_specs=[
            pl.BlockSpec((tm, SSD_INNER), lambda i, k: (i, 0)),
            pl.BlockSpec((tm, SC_WIDTH), lambda i, k: (i, 0)),
            pl.BlockSpec((tm, FT_WIDTH), lambda i, k: (i, 0)),
            pl.BlockSpec((tm, tk), lambda i, k: (i, gate0 + kk(k))),
            pl.BlockSpec((tm, tk), lambda i, k: (i, gate0 + per + kk(k))),
            pl.BlockSpec((tm, tk), lambda i, k: (i, gate0 + 2 * per + kk(k))),
            pl.BlockSpec((None, SSD_INNER, tk), lambda i, k: (l, 0, kk(k))),
            pl.BlockSpec((None, SC_WIDTH, tk), lambda i, k: (l, 0, kk(k))),
            pl.BlockSpec((None, FT_WIDTH, tk), lambda i, k: (l, 0, kk(k))),
            pl.BlockSpec((None, tk, d), lambda i, k: (l, jnp.clip(k - 1, 0, nk - 1), 0)),
            pl.BlockSpec((te, d), erow),
            pl.BlockSpec((None, None, N_MOD, d), lambda i, k: (l, grp(i), 0, 0)),
            pl.BlockSpec((None, 6, d), lambda i, k: (l, 0, 0)),
        ],
        out_specs=[pl.BlockSpec((te, d), erow), pl.BlockSpec((te, d), erow)],
        out_shape=[jax.ShapeDtypeStruct((n_tok, d), F32), jax.ShapeDtypeStruct((n_tok, d), BF16)],
        scratch_shapes=[pltpu.VMEM((tm, d), F32), pltpu.VMEM((2, tm, tk), BF16)],
        compiler_params=_cparams(2),
        name="mixout",
    )(yssd, ysc, yft, tail, tail, tail, w_br_ssd, w_br_sc, w_br_ft, w_out, x, mod4, norm_g)


def _grid_pos_emb(n_tok):
    rows = n_tok // GRID_W
    t = np.arange(rows * GRID_W)
    r = (t // GRID_W).astype(np.float32)[:, None]
    col = (t % GRID_W).astype(np.float32)[:, None]
    nf = D_MODEL // 4
    omega = (1.0 / (np.float32(POS_BASE) ** (np.arange(nf, dtype=np.float32) / np.float32(nf)))).astype(np.float32)
    ro = (r * omega).astype(np.float32).astype(np.float64)
    co = (col * omega).astype(np.float32).astype(np.float64)
    return np.concatenate([np.sin(ro), np.cos(ro), np.sin(co), np.cos(co)], axis=-1).astype(np.float32)


def _group_lanes(p, hpg=HEADS_PER_GROUP):
    depth = p.shape[0]
    t = p.reshape(depth, 2, SSD_GROUPS, hpg).transpose(0, 2, 1, 3).reshape(depth, SSD_GROUPS, 1, 2 * hpg)
    return jnp.pad(t, ((0, 0), (0, 0), (0, 0), (0, LANES - 2 * hpg)))


def kernel(x_prompt, x_sample, state_ssd, c, c_ctx, ada_w, ada_b, norm_g, ffn1_wgu, ffn1_wd, w_in,
           ssd_conv_w, ssd_conv_b, ssd_dt_bias, ssd_a_log, ssd_d, ssd_norm_g, sc_conv_w,
           w_br_ssd, w_br_sc, w_br_ft, w_out, ffn2_wgu, ffn2_wd):
    n_ctx, ctx_len, d = x_prompt.shape
    n_lat, lat_len, _ = x_sample.shape
    depth = ada_w.shape[0]
    rows = _Rows(n_ctx * ctx_len, n_lat, lat_len)

    c8 = jnp.concatenate([c_ctx[None, :], c, jnp.zeros((8 - 1 - n_lat, d), F32)], axis=0)
    mod4 = _modulation(c8, ada_w, ada_b)[:, :1 + n_lat].reshape(depth, 1 + n_lat, N_MOD, d)

    x, u = _embed(x_prompt.reshape(n_ctx * ctx_len, d), x_sample.reshape(n_lat * lat_len, d),
                  jnp.asarray(_grid_pos_emb(lat_len)), mod4, norm_g, rows)

    consts = _ssd_constants()
    w_in_t = jnp.swapaxes(w_in, 1, 2)
    conv_b3 = ssd_conv_b.reshape(depth, 1, SSD_XBC)
    dtb = jnp.pad(ssd_dt_bias.reshape(depth, 1, DT_WIDTH), ((0, 0), (0, 0), (0, LANES - DT_WIDTH)))
    alog_g = _group_lanes(ssd_a_log)
    dsk_g = jnp.repeat(ssd_d, SSD_HEAD_DIM, axis=-1).reshape(depth, SSD_GROUPS, 1, GROUP_WIDTH)
    ssd_norm_g3 = ssd_norm_g.reshape(depth, 1, SSD_INNER)
    state_shape = (n_ctx, depth, 2, SSD_HEADS, SSD_HEAD_DIM, SSD_STATE)

    states = None
    for l in range(depth):
        x, u = _ffn(x, u, mod4, norm_g, ffn1_wgu, ffn1_wd, l, 0, rows, "same")
        proj, dt = _inproj(u, w_in_t, l)
        tail = proj
        ssd_args = (proj, dt, ssd_conv_w, conv_b3, dtb, alog_g, dsk_g, ssd_norm_g3, consts, l)
        yssd, states = _ssd(*ssd_args, row0=0, n_seq=n_ctx, seq_len=ctx_len,
                            prev_state=states, state_shape=state_shape)
        (yssd,) = _ssd(*ssd_args, row0=rows.n_ctx_tok, n_seq=n_lat, seq_len=lat_len,
                       h0=state_ssd, prev_y=yssd)
        ysc, yft = _scft(tail, sc_conv_w, l, row0=0, n_seq=n_ctx, seq_len=ctx_len, n_grp=FT_GROUPS)
        ysc, yft = _scft(tail, sc_conv_w, l, row0=rows.n_ctx_tok, n_seq=n_lat, seq_len=lat_len,
                         n_grp=1, prev=(ysc, yft))
        x, u = _mixout(yssd, ysc, yft, tail, w_br_ssd, w_br_sc, w_br_ft, w_out, x, mod4, norm_g, l, rows)
        if l + 1 < depth:
            x, u = _ffn(x, u, mod4, norm_g, ffn2_wgu, ffn2_wd, l, 1, rows, "layer")
        else:
            y_ctx, y_lat = _ffn(x, u, mod4, norm_g, ffn2_wgu, ffn2_wd, l, 1, rows, "final")

    return (y_ctx.reshape(n_ctx, ctx_len, d), y_lat.reshape(n_lat, lat_len, d), states)
```

```python
import functools
import math

import numpy as np
import jax
import jax.numpy as jnp
from jax import lax
from jax.experimental import pallas as pl
from jax.experimental.pallas import tpu as pltpu

F32 = jnp.float32
BF16 = jnp.bfloat16

D_MODEL = 2048
DEPTH = 4
GRID_W = 64
POS_BASE = 10000.0
NORM_EPS = 1e-6
SSD_HEADS = 32
SSD_HEAD_DIM = 64
SSD_INNER = SSD_HEADS * SSD_HEAD_DIM
SSD_GROUPS = 4
SSD_STATE = 128
SSD_CONV = 4
SSD_CHUNK = 128
SSD_XBC = SSD_INNER + 2 * SSD_GROUPS * SSD_STATE
SC_WIDTH = 1024
SC_CONV = 3
FT_WIDTH = 1024
FT_GROUPS = 4
FT_GROUP_DIM = FT_WIDTH // FT_GROUPS
D_FF = 5504
N_BRANCH = 3
N_MOD = 9
HEADS_PER_GROUP = SSD_HEADS // SSD_GROUPS
GROUP_WIDTH = HEADS_PER_GROUP * SSD_HEAD_DIM
HEAD_COLS = SSD_INNER + SSD_XBC
DT_COL = HEAD_COLS
DT_WIDTH = 2 * SSD_HEADS
TAIL_COL = HEAD_COLS + DT_WIDTH
TAIL_COLS = 3 * SC_WIDTH + FT_WIDTH + N_BRANCH * D_MODEL
IN_COLS = TAIL_COL + TAIL_COLS

LANES = 128
BF16_ROWS = 16
VMEM_LIMIT_BYTES = 60 * 1024 * 1024


def _cparams(n_axes):
    return pltpu.CompilerParams(dimension_semantics=("arbitrary",) * n_axes,
                                vmem_limit_bytes=VMEM_LIMIT_BYTES)


def _silu(x):
    return x * jax.nn.sigmoid(x)


def _rms(x, g):
    ms = jnp.mean(x * x, axis=-1, keepdims=True)
    return x * lax.rsqrt(ms + NORM_EPS) * g


def _norm_mod(x, g, shift, scale):
    return _rms(x, g) * (1.0 + scale) + shift


def _split_bf16(x, n):
    parts = []
    r = x
    for i in range(n):
        p = r.astype(BF16)
        parts.append(p)
        if i + 1 < n:
            r = r - p.astype(F32)
    return parts


def _dot_r01(x, m01, n=3):
    acc = None
    for p in _split_bf16(x, n):
        t = jnp.dot(p, m01, preferred_element_type=F32)
        acc = t if acc is None else acc + t
    return acc


def _dot_l01(m01, x, n=3):
    acc = None
    for p in _split_bf16(x, n):
        t = jnp.dot(m01, p, preferred_element_type=F32)
        acc = t if acc is None else acc + t
    return acc


def _mod_kernel(c_ref, w_ref, b_ref, o_ref):
    s = _silu(c_ref[...]).astype(BF16)
    o_ref[...] = jnp.dot(s, w_ref[...].astype(BF16), preferred_element_type=F32) + b_ref[...]


def _modulation(c8, ada_w, ada_b, tn=1024):
    depth, d, n = ada_w.shape
    return pl.pallas_call(
        _mod_kernel,
        grid=(depth, n // tn),
        in_specs=[
            pl.BlockSpec((8, d), lambda l, j: (0, 0)),
            pl.BlockSpec((None, d, tn), lambda l, j: (l, 0, j)),
            pl.BlockSpec((None, 1, tn), lambda l, j: (l, 0, j)),
        ],
        out_specs=pl.BlockSpec((None, 8, tn), lambda l, j: (l, 0, j)),
        out_shape=jax.ShapeDtypeStruct((depth, 8, n), F32),
        compiler_params=_cparams(2),
        name="adaln_mod",
    )(c8, ada_w, ada_b.reshape(depth, 1, n))


class _Rows:
    def __init__(self, n_ctx_tok, n_lat, lat_len):
        self.n_ctx_tok = n_ctx_tok
        self.n_lat = n_lat
        self.lat_len = lat_len
        self.n_tok = n_ctx_tok + n_lat * lat_len

    def group_of_tile(self, i, tm):
        assert self.n_ctx_tok % tm == 0 and self.lat_len % tm == 0
        n_ctx_tiles = self.n_ctx_tok // tm
        per_lat = self.lat_len // tm
        return jnp.where(i < n_ctx_tiles, 0, 1 + (i - n_ctx_tiles) // per_lat)


def _single(block_shape, index_map):
    return pl.BlockSpec(block_shape, index_map, pipeline_mode=pl.Buffered(1))


FFN_TF = 512
FFN_SPLIT = 2
EPILOGUE_ROWS = 256


def _ffn_kernel(*refs, nk, mod_row, ng_row, next_kind, n_ctx_tiles):
    x_ref, u_ref, mod_ref, ng_ref, wg_ref, wu_ref, wd_ref = refs[:7]
    pos = 7
    if next_kind == "layer":
        modn_ref, ngn_ref = refs[pos:pos + 2]
        pos += 2
    o_ref = refs[pos]
    pos += 1
    if next_kind == "final":
        olat_ref = refs[pos]
    else:
        un_ref = refs[pos]
    pos += 1
    acc_sc, wup_sc, wdn_sc = refs[pos:pos + 3]
    k = pl.program_id(1)
    tf = FFN_TF
    te = o_ref.shape[0]

    @pl.when(k == 0)
    def _():
        acc_sc[...] = jnp.zeros_like(acc_sc)

    @pl.when(k < nk)
    def _():
        ts = tf // FFN_SPLIT
        for s in range(FFN_SPLIT):
            wup_sc[:, 2 * s * ts:(2 * s + 1) * ts] = wg_ref[:, s * ts:(s + 1) * ts].astype(BF16)
            wup_sc[:, (2 * s + 1) * ts:(2 * s + 2) * ts] = wu_ref[:, s * ts:(s + 1) * ts].astype(BF16)
        wdn_sc[...] = wd_ref[...].astype(BF16)

        overlap = nk * tf - D_FF
        assert overlap <= ts
        for s in range(FFN_SPLIT):
            h = jnp.dot(u_ref[...], wup_sc[:, 2 * s * ts:(2 * s + 2) * ts], preferred_element_type=F32)
            a = _silu(h[:, :ts]) * h[:, ts:]
            if s == 0:
                lane = lax.broadcasted_iota(jnp.int32, a.shape, 1)
                a = jnp.where(jnp.logical_and(k == nk - 1, lane < overlap), 0.0, a)
            acc_sc[...] += jnp.dot(a.astype(BF16), wdn_sc[s * ts:(s + 1) * ts, :], preferred_element_type=F32)

    @pl.when(k >= nk)
    def _():
        r0 = pl.multiple_of((k - nk) * te, te)
        y = _rms(acc_sc[pl.ds(r0, te), :], ng_ref[ng_row + 1:ng_row + 2, :])
        x_new = x_ref[...] + 0.5 * mod_ref[mod_row + 2:mod_row + 3, :] * y
        if next_kind == "final":
            is_ctx = pl.program_id(0) < n_ctx_tiles

            @pl.when(is_ctx)
            def _():
                o_ref[...] = x_new

            @pl.when(jnp.logical_not(is_ctx))
            def _():
                olat_ref[...] = x_new
        else:
            o_ref[...] = x_new
        if next_kind == "same":
            un = _norm_mod(x_new, ng_ref[ng_row + 2:ng_row + 3, :],
                           mod_ref[mod_row + 3:mod_row + 4, :], mod_ref[mod_row + 4:mod_row + 5, :])
            un_ref[...] = un.astype(BF16)
        elif next_kind == "layer":
            un = _norm_mod(x_new, ngn_ref[0:1, :], modn_ref[0:1, :], modn_ref[1:2, :])
            un_ref[...] = un.astype(BF16)


def _ffn(x, u, mod4, norm_g, wgu, wd, l, which, rows, next_kind, tm=1024):
    n_tok, d = x.shape
    tf = FFN_TF
    nk = pl.cdiv(D_FF, tf)
    mod_row = 0 if which == 0 else 6
    ng_row = 0 if which == 0 else 4
    grp = lambda i: rows.group_of_tile(i, tm)
    assert tf % LANES == 0 and D_FF % LANES == 0
    col = lambda k, base=0: LANES * (base // LANES + jnp.minimum(k * (tf // LANES), (D_FF - tf) // LANES))
    assert next_kind in ("same", "layer", "final")
    kern = functools.partial(_ffn_kernel, nk=nk, mod_row=mod_row, ng_row=ng_row, next_kind=next_kind,
                             n_ctx_tiles=rows.n_ctx_tok // tm)
    te = EPILOGUE_ROWS
    ne = tm // te
    erow = lambda i, k: (ne * i + jnp.clip(k - nk, 0, ne - 1), 0)
    in_specs = [
        pl.BlockSpec((te, d), erow),
        pl.BlockSpec((tm, d), lambda i, k: (i, 0)),
        pl.BlockSpec((None, None, N_MOD, d), lambda i, k: (l, grp(i), 0, 0)),
        pl.BlockSpec((None, 6, d), lambda i, k: (l, 0, 0)),
        pl.BlockSpec((pl.squeezed, pl.Element(d), pl.Element(tf)), lambda i, k: (l, 0, col(k))),
        pl.BlockSpec((pl.squeezed, pl.Element(d), pl.Element(tf)), lambda i, k: (l, 0, col(k, D_FF))),
        pl.BlockSpec((pl.squeezed, pl.Element(tf), pl.Element(d)), lambda i, k: (l, col(k), 0)),
    ]
    args = [x, u, mod4, norm_g, wgu, wgu, wd]
    if next_kind == "layer":
        in_specs += [pl.BlockSpec((None, None, N_MOD, d), lambda i, k: (l + 1, grp(i), 0, 0)),
                     pl.BlockSpec((None, 6, d), lambda i, k: (l + 1, 0, 0))]
        args += [mod4, norm_g]
    if next_kind == "final":
        n_ctx_blocks = rows.n_ctx_tok // te
        eblk = lambda i, k: ne * i + jnp.clip(k - nk, 0, ne - 1)
        out_specs = [pl.BlockSpec((te, d), lambda i, k: (jnp.minimum(eblk(i, k), n_ctx_blocks - 1), 0)),
                     pl.BlockSpec((te, d), lambda i, k: (jnp.maximum(eblk(i, k) - n_ctx_blocks, 0), 0))]
        out_shape = [jax.ShapeDtypeStruct((rows.n_ctx_tok, d), F32),
                     jax.ShapeDtypeStruct((n_tok - rows.n_ctx_tok, d), F32)]
    else:
        out_specs = [pl.BlockSpec((te, d), erow), pl.BlockSpec((te, d), erow)]
        out_shape = [jax.ShapeDtypeStruct((n_tok, d), F32), jax.ShapeDtypeStruct((n_tok, d), BF16)]
    return pl.pallas_call(
        kern,
        grid=(n_tok // tm, nk + ne),
        in_specs=in_specs,
        out_specs=out_specs,
        out_shape=out_shape,
        scratch_shapes=[
            pltpu.VMEM((tm, d), F32),
            pltpu.VMEM((d, 2 * tf), BF16),
            pltpu.VMEM((tf, d), BF16),
        ],
        compiler_params=_cparams(2),
        name="ffn",
    )(*args)


def _embed_kernel(xp_ref, xs_ref, pe_ref, mod_ref, ng_ref, x_ref, u_ref, *, n_ctx_tiles):
    i = pl.program_id(0)

    @pl.when(i < n_ctx_tiles)
    def _():
        x_ref[...] = xp_ref[...]

    @pl.when(i >= n_ctx_tiles)
    def _():
        x_ref[...] = xs_ref[...] + pe_ref[...]

    u_ref[...] = _norm_mod(x_ref[...], ng_ref[0:1, :], mod_ref[0:1, :], mod_ref[1:2, :]).astype(BF16)


def _embed(xp, xs, pe, mod4, norm_g, rows, tm=256):
    d = xp.shape[1]
    assert rows.lat_len % tm == 0 and rows.n_ctx_tok % tm == 0
    n_ctx_tiles = rows.n_ctx_tok // tm
    pe_tiles = rows.lat_len // tm
    grp = lambda i: rows.group_of_tile(i, tm)
    kern = functools.partial(_embed_kernel, n_ctx_tiles=n_ctx_tiles)
    return pl.pallas_call(
        kern,
        grid=(rows.n_tok // tm,),
        in_specs=[
            pl.BlockSpec((tm, d), lambda i: (jnp.minimum(i, n_ctx_tiles - 1), 0)),
            pl.BlockSpec((tm, d), lambda i: (jnp.maximum(i - n_ctx_tiles, 0), 0)),
            pl.BlockSpec((tm, d), lambda i: (jnp.maximum(i - n_ctx_tiles, 0) % pe_tiles, 0)),
            pl.BlockSpec((None, None, N_MOD, d), lambda i: (0, grp(i), 0, 0)),
            pl.BlockSpec((None, 6, d), lambda i: (0, 0, 0)),
        ],
        out_specs=[pl.BlockSpec((tm, d), lambda i: (i, 0)), pl.BlockSpec((tm, d), lambda i: (i, 0))],
        out_shape=[jax.ShapeDtypeStruct((rows.n_tok, d), F32), jax.ShapeDtypeStruct((rows.n_tok, d), BF16)],
        compiler_params=_cparams(1),
        name="embed",
    )(xp, xs, pe, mod4, norm_g)


def _dot_nt(a, b):
    return lax.dot_general(a, b, (((1,), (1,)), ((), ())), preferred_element_type=F32)


def _softplus(x):
    return jnp.maximum(x, 0.0) + jnp.log1p(jnp.exp(-jnp.abs(x)))


def _inproj_kernel(u_ref, wa_ref, wb_ref, dtb_ref, op_ref, odt_ref, w_sc, *, n_head, n_z, n_gate0, tm):
    j = pl.program_id(0)
    m = pl.program_id(1)
    tn = w_sc.shape[0]
    off = DT_WIDTH

    @pl.when(jnp.logical_and(m == 0, j < n_head))
    def _():
        w_sc[...] = wa_ref[...].astype(BF16)

    @pl.when(jnp.logical_and(m == 0, j == n_head))
    def _():
        w_sc[:off, :] = wb_ref[...].astype(BF16)

    @pl.when(jnp.logical_and(m == 0, j > n_head))
    def _():
        w_sc[:tn - off, :] = wa_ref[off:, :].astype(BF16)
        w_sc[tn - off:, :] = wb_ref[...].astype(BF16)

    u = u_ref[pl.ds(pl.multiple_of(m * tm, tm), tm), :]

    @pl.when(j < n_z)
    def _():
        op_ref[...] = _silu(_dot_nt(u, w_sc[...]))

    @pl.when(jnp.logical_and(j >= n_z, jnp.logical_and(j != n_head, j < n_gate0)))
    def _():
        op_ref[...] = _dot_nt(u, w_sc[...])

    @pl.when(j >= n_gate0)
    def _():
        op_ref[...] = jax.nn.sigmoid(_dot_nt(u, w_sc[...]))

    @pl.when(j == n_head)
    def _():
        odt_ref[...] = _softplus(_dot_nt(u, w_sc[:LANES, :]) + dtb_ref[...])


def _inproj(u, w_in_t, dtb, l, tm=2048, tn=512):
    n_tok, d = u.shape
    off = DT_WIDTH
    assert TAIL_COL % tn == off and HEAD_COLS % tn == 0 and TAIL_COLS % tn == 0 and tn % off == 0
    assert SSD_INNER % tn == 0 and (3 * SC_WIDTH + FT_WIDTH) % tn == 0
    n_head = HEAD_COLS // tn
    n_tail = TAIL_COLS // tn
    n_m = n_tok // tm
    sub = tn // off
    proj_j = lambda j: jnp.where(j < n_head, j, jnp.where(j == n_head, n_head - 1, j - 1))
    proj_m = lambda j, m: jnp.where(j == n_head, n_m - 1, m)
    dt_m = lambda j, m: jnp.where(j < n_head, 0, jnp.where(j == n_head, m, n_m - 1))
    wide_j = lambda j: jnp.where(j <= n_head, jnp.minimum(j, n_head), j - 1)
    narrow_j = lambda j: jnp.where(j <= n_head, DT_COL // off, j * sub)
    kern = functools.partial(_inproj_kernel, n_head=n_head, n_z=SSD_INNER // tn,
                             n_gate0=n_head + 1 + (3 * SC_WIDTH + FT_WIDTH) // tn, tm=tm)
    return pl.pallas_call(
        kern,
        grid=(n_head + 1 + n_tail, n_m),
        in_specs=[
            _single((n_tok, d), lambda j, m: (0, 0)),
            pl.BlockSpec((None, tn, d), lambda j, m: (l, wide_j(j), 0)),
            pl.BlockSpec((None, off, d), lambda j, m: (l, narrow_j(j), 0)),
            pl.BlockSpec((None, 1, LANES), lambda j, m: (l, 0, 0)),
        ],
        out_specs=[
            pl.BlockSpec((tm, tn), lambda j, m: (proj_m(j, m), proj_j(j))),
            pl.BlockSpec((tm, LANES), lambda j, m: (dt_m(j, m), 0)),
        ],
        out_shape=[
            jax.ShapeDtypeStruct((n_tok, HEAD_COLS + TAIL_COLS), F32),
            jax.ShapeDtypeStruct((n_tok, LANES), F32),
        ],
        scratch_shapes=[pltpu.VMEM((tn, d), BF16)],
        compiler_params=_cparams(2),
        name="inproj",
    )(u, w_in_t, w_in_t, dtb)


def _dwconv_rows(x, w, left):
    n_rows = x.shape[0]
    row = lax.broadcasted_iota(jnp.int32, x.shape, 0)
    out = None
    for k in range(w.shape[0]):
        off = k - left
        if off == 0:
            term = x
        else:
            shifted = pltpu.roll(x, (-off) % n_rows, axis=0)
            valid = (row < n_rows - off) if off > 0 else (row >= -off)
            term = jnp.where(valid, shifted, 0.0)
        term = term * w[k:k + 1, :]
        out = term if out is None else out + term
    return out


def _ssd_kernel(*refs, seq_len, has_h0, n_alias, emit_state):
    (z_ref, xr_ref, br_ref, cr_ref, dt_ref, cwx_ref, cwb_ref, cwc_ref, cbx_ref, cbb_ref, cbc_ref,
     alog_ref, dsk_ref, gn_ref, sel_ref, e2_ref, tri_ref, trit_ref) = refs[:18]
    pos = 18
    h0_ref = None
    if has_h0:
        h0_ref = refs[pos]
        pos += 1
    pos += n_alias
    yn_ref = refs[pos]
    pos += 1
    hfin_ref = None
    if emit_state:
        hfin_ref = refs[pos]
        pos += 1
    xa_sc, ba_sc, ca_sc, dts_sc, a_sc, ht_sc, yg_sc, y_sc, ssq_sc = refs[pos:pos + 9]

    gi = pl.program_id(1)
    q = SSD_CHUNK
    nc = seq_len // q
    left = (SSD_CONV - 1) // 2
    hpg = HEADS_PER_GROUP
    gw = GROUP_WIDTH

    xa = _silu(_dwconv_rows(xr_ref[...], cwx_ref[...], left) + cbx_ref[...])
    xa_sc[...] = xa
    ba_sc[...] = _silu(_dwconv_rows(br_ref[...], cwb_ref[...], left) + cbb_ref[...])
    ca_sc[...] = _silu(_dwconv_rows(cr_ref[...], cwc_ref[...], left) + cbc_ref[...])
    yg_sc[...] = xa * dsk_ref[...]

    dts = _dot_r01(dt_ref[...], sel_ref[...])
    dts_sc[...] = dts
    a_sc[...] = dts * (-jnp.exp(alog_ref[...]))

    for d in range(2):
        if has_h0:
            for p in range(hpg // 2):
                blk = h0_ref[d, 2 * p:2 * p + 2].reshape(2 * SSD_HEAD_DIM, SSD_STATE)
                ht_sc[d, :, p * LANES:(p + 1) * LANES] = blk.T
        else:
            ht_sc[d] = jnp.zeros((SSD_STATE, gw), F32)

    ri = lax.broadcasted_iota(jnp.int32, (q, q), 0)
    ci = lax.broadcasted_iota(jnp.int32, (q, q), 1)
    lane_lo = lax.broadcasted_iota(jnp.int32, (q, LANES), 1) < SSD_HEAD_DIM
    neg_inf = jnp.float32(-jnp.inf)

    def chunk_dir(c, d):
        rows = pl.ds(pl.multiple_of(c * q, q), q)
        a_c = a_sc[rows, :]
        if d == 0:
            cs = _dot_l01(tri_ref[...], a_c)
            tot = cs[q - 1:q, :]
            mask = ri >= ci
        else:
            cs = _dot_l01(trit_ref[...], a_c)
            tot = cs[0:1, :]
            mask = ri <= ci
        cst = cs.T
        stack = jnp.concatenate([
            dts_sc[rows, :].astype(BF16),
            jnp.exp(cs).astype(BF16),
            jnp.exp(tot - cs).astype(BF16),
            jnp.broadcast_to(jnp.exp(tot), (BF16_ROWS, LANES)).astype(BF16)], axis=0)
        ex = jnp.dot(stack, e2_ref[:, d * gw:(d + 1) * gw], preferred_element_type=F32)
        dtx, ecs_x, dte_x, ea_x = ex[0:q], ex[q:2 * q], ex[2 * q:3 * q], ex[3 * q:3 * q + 1]
        b_c = ba_sc[rows, :]
        c_bf = ca_sc[rows, :].astype(BF16)
        xdt = xa_sc[rows, :] * dtx
        g = lax.dot_general(c_bf, b_c.astype(BF16), (((1,), (1,)), ((), ())),
                            preferred_element_type=F32)
        ht = ht_sc[d]
        y_off = jnp.dot(c_bf, ht.astype(BF16), preferred_element_type=F32) * ecs_x
        pieces = []
        for p in range(hpg // 2):
            ms = []
            for jj in range(2):
                r = d * hpg + 2 * p + jj
                diff = cs[:, r:r + 1] - cst[r:r + 1, :]
                ms.append((g * jnp.exp(jnp.where(mask, diff, neg_inf))).astype(BF16))
            xp = xdt[:, p * LANES:(p + 1) * LANES]
            rhs = jnp.concatenate([jnp.where(lane_lo, xp, 0.0), jnp.where(lane_lo, 0.0, xp)], axis=0)
            pieces.append(jnp.dot(jnp.concatenate(ms, axis=1), rhs.astype(BF16),
                                  preferred_element_type=F32))
        s_t = jnp.dot(b_c.T.astype(BF16), (xdt * dte_x).astype(BF16), preferred_element_type=F32)
        ht_sc[d] = ht * ea_x + s_t
        yg_sc[rows, :] += jnp.concatenate(pieces, axis=1) + y_off

    def body(c, carry):
        chunk_dir(c, 0)
        chunk_dir(nc - 1 - c, 1)
        return carry

    lax.fori_loop(0, nc, body, 0, unroll=2)

    yz = yg_sc[...] * z_ref[...]
    y_sc[gi] = yz
    part = jnp.broadcast_to(jnp.sum(yz * yz, axis=-1, keepdims=True), ssq_sc.shape)

    @pl.when(gi == 0)
    def _():
        ssq_sc[...] = part

    @pl.when(gi > 0)
    def _():
        ssq_sc[...] += part

    @pl.when(gi == SSD_GROUPS - 1)
    def _():
        inv = lax.rsqrt(ssq_sc[:, 0:1] * (1.0 / SSD_INNER) + NORM_EPS)
        for gg in range(SSD_GROUPS):
            cols = slice(gg * gw, (gg + 1) * gw)
            yn_ref[:, cols] = (y_sc[gg] * inv * gn_ref[:, cols]).astype(BF16)

    if emit_state:
        for d in range(2):
            for p in range(hpg // 2):
                blk = ht_sc[d, :, p * LANES:(p + 1) * LANES].T
                hfin_ref[d, 2 * p:2 * p + 2] = blk.reshape(2, SSD_HEAD_DIM, SSD_STATE)


def _ssd_constants():
    hpg = HEADS_PER_GROUP
    sel = np.zeros((SSD_GROUPS, LANES, LANES), np.float32)
    for g in range(SSD_GROUPS):
        for d in range(2):
            for j in range(hpg):
                sel[g, d * SSD_HEADS + hpg * g + j, d * hpg + j] = 1.0
    e2 = np.zeros((LANES, 2 * GROUP_WIDTH), np.float32)
    for d in range(2):
        for j in range(hpg):
            lo = d * GROUP_WIDTH + j * SSD_HEAD_DIM
            e2[d * hpg + j, lo:lo + SSD_HEAD_DIM] = 1.0
    tri = np.tril(np.ones((SSD_CHUNK, SSD_CHUNK), np.float32))
    return (jnp.asarray(sel, BF16), jnp.asarray(e2, BF16), jnp.asarray(tri, BF16),
            jnp.asarray(tri.T, BF16))


def _ssd(head, dt, conv_w, conv_b3, alog_g, dsk_g, norm_g3, consts, l, *, row0, n_seq, seq_len,
         h0=None, prev_y=None, prev_state=None, state_shape=None):
    n_tok = head.shape[0]
    sel, e2, tri, trit = consts
    assert row0 % seq_len == 0 and seq_len % (2 * SSD_CHUNK) == 0
    blk0 = row0 // seq_len
    gw = GROUP_WIDTH
    has_h0 = h0 is not None
    emit_state = state_shape is not None
    xcol = SSD_INNER // gw
    bcol = (SSD_INNER + SSD_INNER) // LANES
    ccol = bcol + SSD_GROUPS
    cwb = SSD_INNER // LANES
    cwc = cwb + SSD_GROUPS
    in_specs = [
        pl.BlockSpec((seq_len, gw), lambda b, g: (blk0 + b, g)),
        pl.BlockSpec((seq_len, gw), lambda b, g: (blk0 + b, xcol + g)),
        pl.BlockSpec((seq_len, LANES), lambda b, g: (blk0 + b, bcol + g)),
        pl.BlockSpec((seq_len, LANES), lambda b, g: (blk0 + b, ccol + g)),
        pl.BlockSpec((seq_len, LANES), lambda b, g: (blk0 + b, 0)),
        pl.BlockSpec((None, SSD_CONV, gw), lambda b, g: (l, 0, g)),
        pl.BlockSpec((None, SSD_CONV, LANES), lambda b, g: (l, 0, cwb + g)),
        pl.BlockSpec((None, SSD_CONV, LANES), lambda b, g: (l, 0, cwc + g)),
        pl.BlockSpec((None, 1, gw), lambda b, g: (l, 0, g)),
        pl.BlockSpec((None, 1, LANES), lambda b, g: (l, 0, cwb + g)),
        pl.BlockSpec((None, 1, LANES), lambda b, g: (l, 0, cwc + g)),
        pl.BlockSpec((None, None, 1, LANES), lambda b, g: (l, g, 0, 0)),
        pl.BlockSpec((None, None, 1, gw), lambda b, g: (l, g, 0, 0)),
        pl.BlockSpec((None, 1, SSD_INNER), lambda b, g: (l, 0, 0)),
        pl.BlockSpec((None, LANES, LANES), lambda b, g: (g, 0, 0)),
        pl.BlockSpec((LANES, 2 * gw), lambda b, g: (0, 0)),
        pl.BlockSpec((SSD_CHUNK, SSD_CHUNK), lambda b, g: (0, 0)),
        pl.BlockSpec((SSD_CHUNK, SSD_CHUNK), lambda b, g: (0, 0)),
    ]
    args = [head, head, head, head, dt, conv_w, conv_w, conv_w, conv_b3, conv_b3, conv_b3,
            alog_g, dsk_g, norm_g3, sel, e2, tri, trit]
    if has_h0:
        in_specs.append(pl.BlockSpec((None, None, 2, HEADS_PER_GROUP, SSD_HEAD_DIM, SSD_STATE),
                                     lambda b, g: (b, l, 0, g, 0, 0)))
        args.append(h0)
    aliases = {}
    if prev_y is not None:
        in_specs.append(pl.BlockSpec(memory_space=pl.ANY))
        aliases[len(args)] = 0
        args.append(prev_y)
    if prev_state is not None:
        assert emit_state
        in_specs.append(pl.BlockSpec(memory_space=pl.ANY))
        aliases[len(args)] = 1
        args.append(prev_state)
    out_specs = [pl.BlockSpec((seq_len, SSD_INNER), lambda b, g: (blk0 + b, 0))]
    out_shape = [jax.ShapeDtypeStruct((n_tok, SSD_INNER), BF16)]
    if emit_state:
        out_specs.append(pl.BlockSpec((None, None, 2, HEADS_PER_GROUP, SSD_HEAD_DIM, SSD_STATE),
                                      lambda b, g: (b, l, 0, g, 0, 0)))
        out_shape.append(jax.ShapeDtypeStruct(state_shape, F32))
    kern = functools.partial(_ssd_kernel, seq_len=seq_len, has_h0=has_h0, n_alias=len(aliases),
                             emit_state=emit_state)
    return pl.pallas_call(
        kern,
        grid=(n_seq, SSD_GROUPS),
        in_specs=in_specs,
        out_specs=out_specs,
        out_shape=out_shape,
        input_output_aliases=aliases,
        scratch_shapes=[
            pltpu.VMEM((seq_len, gw), F32),
            pltpu.VMEM((seq_len, LANES), F32),
            pltpu.VMEM((seq_len, LANES), F32),
            pltpu.VMEM((seq_len, LANES), F32),
            pltpu.VMEM((seq_len, LANES), F32),
            pltpu.VMEM((2, SSD_STATE, gw), F32),
            pltpu.VMEM((seq_len, gw), F32),
            pltpu.VMEM((SSD_GROUPS, seq_len, gw), F32),
            pltpu.VMEM((seq_len, LANES), F32),
        ],
        compiler_params=_cparams(2),
        name="ssd_lat" if has_h0 else "ssd_ctx",
    )(*args)


def _scft_kernel(*refs, n_alias, n_grp):
    b_ref, c_ref, x_ref, f_ref, cw_ref, cl_ref, sl_ref, cc_ref, sc_ref = refs[:9]
    pos = 9 + n_alias
    ysc_ref, yft_ref = refs[pos], refs[pos + 1]
    v = c_ref[...] * x_ref[...]
    ysc_ref[...] = (b_ref[...] * _dwconv_rows(v, cw_ref[...], (SC_CONV - 1) // 2)).astype(BF16)
    tw = FT_GROUP_DIM
    for gg in range(n_grp):
        cols = slice(gg * tw, (gg + 1) * tw)
        u = f_ref[:, cols].astype(BF16)
        p = jnp.dot(u, cc_ref[...], preferred_element_type=F32).astype(BF16)
        s = jnp.dot(u, sc_ref[...], preferred_element_type=F32).astype(BF16)
        yft_ref[:, cols] = (jnp.dot(cl_ref[...], p, preferred_element_type=F32)
                            - jnp.dot(sl_ref[...], s, preferred_element_type=F32)).astype(BF16)


def _dft_mats(n):
    k = np.arange(n)
    ang = 2.0 * np.pi * ((k[:, None] * k[None, :]) % n) / n
    scale = 1.0 / math.sqrt(n)
    return jnp.asarray(np.cos(ang) * scale, BF16), jnp.asarray(np.sin(ang) * scale, BF16)


def _scft(tail, sc_conv_w, l, *, row0, n_seq, seq_len, n_grp, prev=None):
    n_tok = tail.shape[0]
    tw = n_grp * FT_GROUP_DIM
    assert row0 % seq_len == 0 and SC_WIDTH % tw == 0 and HEAD_COLS % tw == 0
    blk0 = row0 // seq_len
    nt = SC_WIDTH // tw
    c0 = HEAD_COLS // tw
    cl, sl = _dft_mats(seq_len)
    cc, sc = _dft_mats(FT_GROUP_DIM)
    in_specs = [
        pl.BlockSpec((seq_len, tw), lambda b, j: (blk0 + b, c0 + j)),
        pl.BlockSpec((seq_len, tw), lambda b, j: (blk0 + b, c0 + nt + j)),
        pl.BlockSpec((seq_len, tw), lambda b, j: (blk0 + b, c0 + 2 * nt + j)),
        pl.BlockSpec((seq_len, tw), lambda b, j: (blk0 + b, c0 + 3 * nt + j)),
        pl.BlockSpec((None, SC_CONV, tw), lambda b, j: (l, 0, j)),
        pl.BlockSpec((seq_len, seq_len), lambda b, j: (0, 0)),
        pl.BlockSpec((seq_len, seq_len), lambda b, j: (0, 0)),
        pl.BlockSpec((FT_GROUP_DIM, FT_GROUP_DIM), lambda b, j: (0, 0)),
        pl.BlockSpec((FT_GROUP_DIM, FT_GROUP_DIM), lambda b, j: (0, 0)),
    ]
    args = [tail, tail, tail, tail, sc_conv_w, cl, sl, cc, sc]
    aliases = {}
    if prev is not None:
        in_specs += [pl.BlockSpec(memory_space=pl.ANY), pl.BlockSpec(memory_space=pl.ANY)]
        aliases = {len(args): 0, len(args) + 1: 1}
        args += list(prev)
    kern = functools.partial(_scft_kernel, n_alias=len(aliases), n_grp=n_grp)
    return pl.pallas_call(
        kern,
        grid=(n_seq, nt),
        in_specs=in_specs,
        out_specs=[
            pl.BlockSpec((seq_len, tw), lambda b, j: (blk0 + b, j)),
            pl.BlockSpec((seq_len, tw), lambda b, j: (blk0 + b, j)),
        ],
        out_shape=[
            jax.ShapeDtypeStruct((n_tok, SC_WIDTH), BF16),
            jax.ShapeDtypeStruct((n_tok, FT_WIDTH), BF16),
        ],
        input_output_aliases=aliases,
        compiler_params=_cparams(2),
        name="scft_lat" if prev is not None else "scft_ctx",
    )(*args)


def _mixout_kernel(a0_ref, a1_ref, a2_ref, g0_ref, g1_ref, g2_ref, w0_ref, w1_ref, w2_ref, wo_ref,
                   x_ref, mod_ref, ng_ref, o_ref, un_ref, acc_sc, m_sc, *, nk):
    k = pl.program_id(1)
    te = o_ref.shape[0]

    def merge():
        m = g0_ref[...] * jnp.dot(a0_ref[...], w0_ref[...].astype(BF16), preferred_element_type=F32)
        m += g1_ref[...] * jnp.dot(a1_ref[...], w1_ref[...].astype(BF16), preferred_element_type=F32)
        m += g2_ref[...] * jnp.dot(a2_ref[...], w2_ref[...].astype(BF16), preferred_element_type=F32)
        m_sc[k % 2] = m.astype(BF16)

    def project():
        return jnp.dot(m_sc[(k + 1) % 2], wo_ref[...].astype(BF16), preferred_element_type=F32)

    @pl.when(k == 0)
    def _():
        merge()

    @pl.when(k == 1)
    def _():
        acc_sc[...] = project()
        merge()

    @pl.when(jnp.logical_and(k > 1, k < nk))
    def _():
        acc_sc[...] += project()
        merge()

    @pl.when(k == nk)
    def _():
        acc_sc[...] += project()

    @pl.when(k >= nk)
    def _():
        r0 = pl.multiple_of((k - nk) * te, te)
        y = _rms(acc_sc[pl.ds(r0, te), :], ng_ref[3:4, :])
        x_new = x_ref[...] + mod_ref[5:6, :] * y
        o_ref[...] = x_new
        un_ref[...] = _norm_mod(x_new, ng_ref[4:5, :], mod_ref[6:7, :], mod_ref[7:8, :]).astype(BF16)


def _mixout(yssd, ysc, yft, tail, w_br_ssd, w_br_sc, w_br_ft, w_out, x, mod4, norm_g, l, rows,
            tm=1024, tk=256):
    n_tok, d = x.shape
    nk = d // tk
    gate0 = (HEAD_COLS + 3 * SC_WIDTH + FT_WIDTH) // tk
    per = d // tk
    grp = lambda i: rows.group_of_tile(i, tm)
    kk = lambda k: jnp.minimum(k, nk - 1)
    te = EPILOGUE_ROWS
    ne = tm // te
    erow = lambda i, k: (ne * i + jnp.clip(k - nk, 0, ne - 1), 0)
    kern = functools.partial(_mixout_kernel, nk=nk)
    return pl.pallas_call(
        kern,
        grid=(n_tok // tm, nk + ne),
        in_specs=[
            pl.BlockSpec((tm, SSD_INNER), lambda i, k: (i, 0)),
            pl.BlockSpec((tm, SC_WIDTH), lambda i, k: (i, 0)),
            pl.BlockSpec((tm, FT_WIDTH), lambda i, k: (i, 0)),
            pl.BlockSpec((tm, tk), lambda i, k: (i, gate0 + kk(k))),
            pl.BlockSpec((tm, tk), lambda i, k: (i, gate0 + per + kk(k))),
            pl.BlockSpec((tm, tk), lambda i, k: (i, gate0 + 2 * per + kk(k))),
            pl.BlockSpec((None, SSD_INNER, tk), lambda i, k: (l, 0, kk(k))),
            pl.BlockSpec((None, SC_WIDTH, tk), lambda i, k: (l, 0, kk(k))),
            pl.BlockSpec((None, FT_WIDTH, tk), lambda i, k: (l, 0, kk(k))),
            pl.BlockSpec((None, tk, d), lambda i, k: (l, jnp.clip(k - 1, 0, nk - 1), 0)),
            pl.BlockSpec((te, d), erow),
            pl.BlockSpec((None, None, N_MOD, d), lambda i, k: (l, grp(i), 0, 0)),
            pl.BlockSpec((None, 6, d), lambda i, k: (l, 0, 0)),
        ],
        out_specs=[pl.BlockSpec((te, d), erow), pl.BlockSpec((te, d), erow)],
        out_shape=[jax.ShapeDtypeStruct((n_tok, d), F32), jax.ShapeDtypeStruct((n_tok, d), BF16)],
        scratch_shapes=[pltpu.VMEM((tm, d), F32), pltpu.VMEM((2, tm, tk), BF16)],
        compiler_params=_cparams(2),
        name="mixout",
    )(yssd, ysc, yft, tail, tail, tail, w_br_ssd, w_br_sc, w_br_ft, w_out, x, mod4, norm_g)


def _grid_pos_emb(n_tok):
    rows = n_tok // GRID_W
    t = np.arange(rows * GRID_W)
    r = (t // GRID_W).astype(np.float32)[:, None]
    col = (t % GRID_W).astype(np.float32)[:, None]
    nf = D_MODEL // 4
    omega = (1.0 / (np.float32(POS_BASE) ** (np.arange(nf, dtype=np.float32) / np.float32(nf)))).astype(np.float32)
    ro = (r * omega).astype(np.float32).astype(np.float64)
    co = (col * omega).astype(np.float32).astype(np.float64)
    return np.concatenate([np.sin(ro), np.cos(ro), np.sin(co), np.cos(co)], axis=-1).astype(np.float32)


def _group_lanes(p, hpg=HEADS_PER_GROUP):
    depth = p.shape[0]
    t = p.reshape(depth, 2, SSD_GROUPS, hpg).transpose(0, 2, 1, 3).reshape(depth, SSD_GROUPS, 1, 2 * hpg)
    return jnp.pad(t, ((0, 0), (0, 0), (0, 0), (0, LANES - 2 * hpg)))


def kernel(x_prompt, x_sample, state_ssd, c, c_ctx, ada_w, ada_b, norm_g, ffn1_wgu, ffn1_wd, w_in,
           ssd_conv_w, ssd_conv_b, ssd_dt_bias, ssd_a_log, ssd_d, ssd_norm_g, sc_conv_w,
           w_br_ssd, w_br_sc, w_br_ft, w_out, ffn2_wgu, ffn2_wd):
    n_ctx, ctx_len, d = x_prompt.shape
    n_lat, lat_len, _ = x_sample.shape
    depth = ada_w.shape[0]
    rows = _Rows(n_ctx * ctx_len, n_lat, lat_len)

    c8 = jnp.concatenate([c_ctx[None, :], c, jnp.zeros((8 - 1 - n_lat, d), F32)], axis=0)
    mod4 = _modulation(c8, ada_w, ada_b)[:, :1 + n_lat].reshape(depth, 1 + n_lat, N_MOD, d)

    x, u = _embed(x_prompt.reshape(n_ctx * ctx_len, d), x_sample.reshape(n_lat * lat_len, d),
                  jnp.asarray(_grid_pos_emb(lat_len)), mod4, norm_g, rows)

    consts = _ssd_constants()
    w_in_t = jnp.swapaxes(w_in, 1, 2)
    conv_b3 = ssd_conv_b.reshape(depth, 1, SSD_XBC)
    dtb = jnp.pad(ssd_dt_bias.reshape(depth, 1, DT_WIDTH), ((0, 0), (0, 0), (0, LANES - DT_WIDTH)))
    alog_g = _group_lanes(ssd_a_log)
    dsk_g = jnp.repeat(ssd_d, SSD_HEAD_DIM, axis=-1).reshape(depth, SSD_GROUPS, 1, GROUP_WIDTH)
    ssd_norm_g3 = ssd_norm_g.reshape(depth, 1, SSD_INNER)
    state_shape = (n_ctx, depth, 2, SSD_HEADS, SSD_HEAD_DIM, SSD_STATE)

    states = None
    for l in range(depth):
        x, u = _ffn(x, u, mod4, norm_g, ffn1_wgu, ffn1_wd, l, 0, rows, "same")
        proj, dt = _inproj(u, w_in_t, dtb, l)
        tail = proj
        ssd_args = (proj, dt, ssd_conv_w, conv_b3, alog_g, dsk_g, ssd_norm_g3, consts, l)
        yssd, states = _ssd(*ssd_args, row0=0, n_seq=n_ctx, seq_len=ctx_len,
                            prev_state=states, state_shape=state_shape)
        (yssd,) = _ssd(*ssd_args, row0=rows.n_ctx_tok, n_seq=n_lat, seq_len=lat_len,
                       h0=state_ssd, prev_y=yssd)
        ysc, yft = _scft(tail, sc_conv_w, l, row0=0, n_seq=n_ctx, seq_len=ctx_len, n_grp=FT_GROUPS)
        ysc, yft = _scft(tail, sc_conv_w, l, row0=rows.n_ctx_tok, n_seq=n_lat, seq_len=lat_len,
                         n_grp=1, prev=(ysc, yft))
        x, u = _mixout(yssd, ysc, yft, tail, w_br_ssd, w_br_sc, w_br_ft, w_out, x, mod4, norm_g, l, rows)
        if l + 1 < depth:
            x, u = _ffn(x, u, mod4, norm_g, ffn2_wgu, ffn2_wd, l, 1, rows, "layer")
        else:
            y_ctx, y_lat = _ffn(x, u, mod4, norm_g, ffn2_wgu, ffn2_wd, l, 1, rows, "final")

    return (y_ctx.reshape(n_ctx, ctx_len, d), y_lat.reshape(n_lat, lat_len, d), states)
```

```python
import functools
import math

import numpy as np
import jax
import jax.numpy as jnp
from jax import lax
from jax.experimental import pallas as pl
from jax.experimental.pallas import tpu as pltpu

F32 = jnp.float32
BF16 = jnp.bfloat16

D_MODEL = 2048
DEPTH = 4
GRID_W = 64
POS_BASE = 10000.0
NORM_EPS = 1e-6
SSD_HEADS = 32
SSD_HEAD_DIM = 64
SSD_INNER = SSD_HEADS * SSD_HEAD_DIM
SSD_GROUPS = 4
SSD_STATE = 128
SSD_CONV = 4
SSD_CHUNK = 128
SSD_XBC = SSD_INNER + 2 * SSD_GROUPS * SSD_STATE
SC_WIDTH = 1024
SC_CONV = 3
FT_WIDTH = 1024
FT_GROUPS = 4
FT_GROUP_DIM = FT_WIDTH // FT_GROUPS
D_FF = 5504
N_BRANCH = 3
N_MOD = 9
HEADS_PER_GROUP = SSD_HEADS // SSD_GROUPS
GROUP_WIDTH = HEADS_PER_GROUP * SSD_HEAD_DIM
HEAD_COLS = SSD_INNER + SSD_XBC
DT_COL = HEAD_COLS
DT_WIDTH = 2 * SSD_HEADS
TAIL_COL = HEAD_COLS + DT_WIDTH
TAIL_COLS = 3 * SC_WIDTH + FT_WIDTH + N_BRANCH * D_MODEL
IN_COLS = TAIL_COL + TAIL_COLS

LANES = 128
BF16_ROWS = 16
VMEM_LIMIT_BYTES = 60 * 1024 * 1024


def _cparams(n_axes):
    return pltpu.CompilerParams(dimension_semantics=("arbitrary",) * n_axes,
                                vmem_limit_bytes=VMEM_LIMIT_BYTES)


def _silu(x):
    return x * jax.nn.sigmoid(x)


def _rms(x, g):
    ms = jnp.mean(x * x, axis=-1, keepdims=True)
    return x * lax.rsqrt(ms + NORM_EPS) * g


def _norm_mod(x, g, shift, scale):
    return _rms(x, g) * (1.0 + scale) + shift


def _split_bf16(x, n):
    parts = []
    r = x
    for i in range(n):
        p = r.astype(BF16)
        parts.append(p)
        if i + 1 < n:
            r = r - p.astype(F32)
    return parts


def _dot_r01(x, m01, n=3):
    acc = None
    for p in _split_bf16(x, n):
        t = jnp.dot(p, m01, preferred_element_type=F32)
        acc = t if acc is None else acc + t
    return acc


def _dot_l01(m01, x, n=3):
    acc = None
    for p in _split_bf16(x, n):
        t = jnp.dot(m01, p, preferred_element_type=F32)
        acc = t if acc is None else acc + t
    return acc


def _mod_kernel(c_ref, w_ref, b_ref, o_ref):
    s = _silu(c_ref[...]).astype(BF16)
    o_ref[...] = jnp.dot(s, w_ref[...].astype(BF16), preferred_element_type=F32) + b_ref[...]


def _modulation(c8, ada_w, ada_b, tn=1024):
    depth, d, n = ada_w.shape
    return pl.pallas_call(
        _mod_kernel,
        grid=(depth, n // tn),
        in_specs=[
            pl.BlockSpec((8, d), lambda l, j: (0, 0)),
            pl.BlockSpec((None, d, tn), lambda l, j: (l, 0, j)),
            pl.BlockSpec((None, 1, tn), lambda l, j: (l, 0, j)),
        ],
        out_specs=pl.BlockSpec((None, 8, tn), lambda l, j: (l, 0, j)),
        out_shape=jax.ShapeDtypeStruct((depth, 8, n), F32),
        compiler_params=_cparams(2),
        name="adaln_mod",
    )(c8, ada_w, ada_b.reshape(depth, 1, n))


class _Rows:
    def __init__(self, n_ctx_tok, n_lat, lat_len):
        self.n_ctx_tok = n_ctx_tok
        self.n_lat = n_lat
        self.lat_len = lat_len
        self.n_tok = n_ctx_tok + n_lat * lat_len

    def group_of_tile(self, i, tm):
        assert self.n_ctx_tok % tm == 0 and self.lat_len % tm == 0
        n_ctx_tiles = self.n_ctx_tok // tm
        per_lat = self.lat_len // tm
        return jnp.where(i < n_ctx_tiles, 0, 1 + (i - n_ctx_tiles) // per_lat)


def _single(block_shape, index_map):
    return pl.BlockSpec(block_shape, index_map, pipeline_mode=pl.Buffered(1))


FFN_TF = 512
FFN_SPLIT = 2
EPILOGUE_ROWS = 256


def _ffn_kernel(*refs, nk, mod_row, ng_row, next_kind, n_ctx_tiles):
    x_ref, u_ref, mod_ref, ng_ref, wg_ref, wu_ref, wd_ref = refs[:7]
    pos = 7
    if next_kind == "layer":
        modn_ref, ngn_ref = refs[pos:pos + 2]
        pos += 2
    o_ref = refs[pos]
    pos += 1
    if next_kind == "final":
        olat_ref = refs[pos]
    else:
        un_ref = refs[pos]
    pos += 1
    acc_sc, wup_sc, wdn_sc = refs[pos:pos + 3]
    k = pl.program_id(1)
    tf = FFN_TF
    te = o_ref.shape[0]

    @pl.when(k == 0)
    def _():
        acc_sc[...] = jnp.zeros_like(acc_sc)

    @pl.when(k < nk)
    def _():
        ts = tf // FFN_SPLIT
        for s in range(FFN_SPLIT):
            wup_sc[:, 2 * s * ts:(2 * s + 1) * ts] = wg_ref[:, s * ts:(s + 1) * ts].astype(BF16)
            wup_sc[:, (2 * s + 1) * ts:(2 * s + 2) * ts] = wu_ref[:, s * ts:(s + 1) * ts].astype(BF16)
        wdn_sc[...] = wd_ref[...].astype(BF16)

        overlap = nk * tf - D_FF
        assert overlap <= ts
        for s in range(FFN_SPLIT):
            h = jnp.dot(u_ref[...], wup_sc[:, 2 * s * ts:(2 * s + 2) * ts], preferred_element_type=F32)
            a = _silu(h[:, :ts]) * h[:, ts:]
            if s == 0:
                lane = lax.broadcasted_iota(jnp.int32, a.shape, 1)
                a = jnp.where(jnp.logical_and(k == nk - 1, lane < overlap), 0.0, a)
            acc_sc[...] += jnp.dot(a.astype(BF16), wdn_sc[s * ts:(s + 1) * ts, :], preferred_element_type=F32)

    @pl.when(k >= nk)
    def _():
        r0 = pl.multiple_of((k - nk) * te, te)
        y = _rms(acc_sc[pl.ds(r0, te), :], ng_ref[ng_row + 1:ng_row + 2, :])
        x_new = x_ref[...] + 0.5 * mod_ref[mod_row + 2:mod_row + 3, :] * y
        if next_kind == "final":
            is_ctx = pl.program_id(0) < n_ctx_tiles

            @pl.when(is_ctx)
            def _():
                o_ref[...] = x_new

            @pl.when(jnp.logical_not(is_ctx))
            def _():
                olat_ref[...] = x_new
        else:
            o_ref[...] = x_new
        if next_kind == "same":
            un = _norm_mod(x_new, ng_ref[ng_row + 2:ng_row + 3, :],
                           mod_ref[mod_row + 3:mod_row + 4, :], mod_ref[mod_row + 4:mod_row + 5, :])
            un_ref[...] = un.astype(BF16)
        elif next_kind == "layer":
            un = _norm_mod(x_new, ngn_ref[0:1, :], modn_ref[0:1, :], modn_ref[1:2, :])
            un_ref[...] = un.astype(BF16)


def _ffn(x, u, mod4, norm_g, wgu, wd, l, which, rows, next_kind, tm=1024):
    n_tok, d = x.shape
    tf = FFN_TF
    nk = pl.cdiv(D_FF, tf)
    mod_row = 0 if which == 0 else 6
    ng_row = 0 if which == 0 else 4
    grp = lambda i: rows.group_of_tile(i, tm)
    assert tf % LANES == 0 and D_FF % LANES == 0
    col = lambda k, base=0: LANES * (base // LANES + jnp.minimum(k * (tf // LANES), (D_FF - tf) // LANES))
    assert next_kind in ("same", "layer", "final")
    kern = functools.partial(_ffn_kernel, nk=nk, mod_row=mod_row, ng_row=ng_row, next_kind=next_kind,
                             n_ctx_tiles=rows.n_ctx_tok // tm)
    te = EPILOGUE_ROWS
    ne = tm // te
    erow = lambda i, k: (ne * i + jnp.clip(k - nk, 0, ne - 1), 0)
    in_specs = [
        pl.BlockSpec((te, d), erow),
        pl.BlockSpec((tm, d), lambda i, k: (i, 0)),
        pl.BlockSpec((None, None, N_MOD, d), lambda i, k: (l, grp(i), 0, 0)),
        pl.BlockSpec((None, 6, d), lambda i, k: (l, 0, 0)),
        pl.BlockSpec((pl.squeezed, pl.Element(d), pl.Element(tf)), lambda i, k: (l, 0, col(k))),
        pl.BlockSpec((pl.squeezed, pl.Element(d), pl.Element(tf)), lambda i, k: (l, 0, col(k, D_FF))),
        pl.BlockSpec((pl.squeezed, pl.Element(tf), pl.Element(d)), lambda i, k: (l, col(k), 0)),
    ]
    args = [x, u, mod4, norm_g, wgu, wgu, wd]
    if next_kind == "layer":
        in_specs += [pl.BlockSpec((None, None, N_MOD, d), lambda i, k: (l + 1, grp(i), 0, 0)),
                     pl.BlockSpec((None, 6, d), lambda i, k: (l + 1, 0, 0))]
        args += [mod4, norm_g]
    if next_kind == "final":
        n_ctx_blocks = rows.n_ctx_tok // te
        eblk = lambda i, k: ne * i + jnp.clip(k - nk, 0, ne - 1)
        out_specs = [pl.BlockSpec((te, d), lambda i, k: (jnp.minimum(eblk(i, k), n_ctx_blocks - 1), 0)),
                     pl.BlockSpec((te, d), lambda i, k: (jnp.maximum(eblk(i, k) - n_ctx_blocks, 0), 0))]
        out_shape = [jax.ShapeDtypeStruct((rows.n_ctx_tok, d), F32),
                     jax.ShapeDtypeStruct((n_tok - rows.n_ctx_tok, d), F32)]
    else:
        out_specs = [pl.BlockSpec((te, d), erow), pl.BlockSpec((te, d), erow)]
        out_shape = [jax.ShapeDtypeStruct((n_tok, d), F32), jax.ShapeDtypeStruct((n_tok, d), BF16)]
    return pl.pallas_call(
        kern,
        grid=(n_tok // tm, nk + ne),
        in_specs=in_specs,
        out_specs=out_specs,
        out_shape=out_shape,
        scratch_shapes=[
            pltpu.VMEM((tm, d), F32),
            pltpu.VMEM((d, 2 * tf), BF16),
            pltpu.VMEM((tf, d), BF16),
        ],
        compiler_params=_cparams(2),
        name="ffn",
    )(*args)


def _embed_kernel(xp_ref, xs_ref, pe_ref, mod_ref, ng_ref, x_ref, u_ref, *, n_ctx_tiles):
    i = pl.program_id(0)

    @pl.when(i < n_ctx_tiles)
    def _():
        x_ref[...] = xp_ref[...]

    @pl.when(i >= n_ctx_tiles)
    def _():
        x_ref[...] = xs_ref[...] + pe_ref[...]

    u_ref[...] = _norm_mod(x_ref[...], ng_ref[0:1, :], mod_ref[0:1, :], mod_ref[1:2, :]).astype(BF16)


def _embed(xp, xs, pe, mod4, norm_g, rows, tm=256):
    d = xp.shape[1]
    assert rows.lat_len % tm == 0 and rows.n_ctx_tok % tm == 0
    n_ctx_tiles = rows.n_ctx_tok // tm
    pe_tiles = rows.lat_len // tm
    grp = lambda i: rows.group_of_tile(i, tm)
    kern = functools.partial(_embed_kernel, n_ctx_tiles=n_ctx_tiles)
    return pl.pallas_call(
        kern,
        grid=(rows.n_tok // tm,),
        in_specs=[
            pl.BlockSpec((tm, d), lambda i: (jnp.minimum(i, n_ctx_tiles - 1), 0)),
            pl.BlockSpec((tm, d), lambda i: (jnp.maximum(i - n_ctx_tiles, 0), 0)),
            pl.BlockSpec((tm, d), lambda i: (jnp.maximum(i - n_ctx_tiles, 0) % pe_tiles, 0)),
            pl.BlockSpec((None, None, N_MOD, d), lambda i: (0, grp(i), 0, 0)),
            pl.BlockSpec((None, 6, d), lambda i: (0, 0, 0)),
        ],
        out_specs=[pl.BlockSpec((tm, d), lambda i: (i, 0)), pl.BlockSpec((tm, d), lambda i: (i, 0))],
        out_shape=[jax.ShapeDtypeStruct((rows.n_tok, d), F32), jax.ShapeDtypeStruct((rows.n_tok, d), BF16)],
        compiler_params=_cparams(1),
        name="embed",
    )(xp, xs, pe, mod4, norm_g)


def _dot_nt(a, b):
    return lax.dot_general(a, b, (((1,), (1,)), ((), ())), preferred_element_type=F32)


def _softplus(x):
    return jnp.maximum(x, 0.0) + jnp.log1p(jnp.exp(-jnp.abs(x)))


def _inproj_kernel(u_ref, wa_ref, wb_ref, dtb_ref, op_ref, odt_ref, w_sc, *, n_head, tm):
    j = pl.program_id(0)
    m = pl.program_id(1)
    tn = w_sc.shape[0]
    off = DT_WIDTH

    @pl.when(jnp.logical_and(m == 0, j < n_head))
    def _():
        w_sc[...] = wa_ref[...].astype(BF16)

    @pl.when(jnp.logical_and(m == 0, j == n_head))
    def _():
        w_sc[:off, :] = wb_ref[...].astype(BF16)

    @pl.when(jnp.logical_and(m == 0, j > n_head))
    def _():
        w_sc[:tn - off, :] = wa_ref[off:, :].astype(BF16)
        w_sc[tn - off:, :] = wb_ref[...].astype(BF16)

    u = u_ref[pl.ds(pl.multiple_of(m * tm, tm), tm), :]

    @pl.when(j != n_head)
    def _():
        op_ref[...] = _dot_nt(u, w_sc[...])

    @pl.when(j == n_head)
    def _():
        odt_ref[...] = _softplus(_dot_nt(u, w_sc[:LANES, :]) + dtb_ref[...])


def _inproj(u, w_in_t, dtb, l, tm=2048, tn=512):
    n_tok, d = u.shape
    off = DT_WIDTH
    assert TAIL_COL % tn == off and HEAD_COLS % tn == 0 and TAIL_COLS % tn == 0 and tn % off == 0
    n_head = HEAD_COLS // tn
    n_tail = TAIL_COLS // tn
    n_m = n_tok // tm
    sub = tn // off
    proj_j = lambda j: jnp.where(j < n_head, j, jnp.where(j == n_head, n_head - 1, j - 1))
    proj_m = lambda j, m: jnp.where(j == n_head, n_m - 1, m)
    dt_m = lambda j, m: jnp.where(j < n_head, 0, jnp.where(j == n_head, m, n_m - 1))
    wide_j = lambda j: jnp.where(j <= n_head, jnp.minimum(j, n_head), j - 1)
    narrow_j = lambda j: jnp.where(j <= n_head, DT_COL // off, j * sub)
    kern = functools.partial(_inproj_kernel, n_head=n_head, tm=tm)
    return pl.pallas_call(
        kern,
        grid=(n_head + 1 + n_tail, n_m),
        in_specs=[
            _single((n_tok, d), lambda j, m: (0, 0)),
            pl.BlockSpec((None, tn, d), lambda j, m: (l, wide_j(j), 0)),
            pl.BlockSpec((None, off, d), lambda j, m: (l, narrow_j(j), 0)),
            pl.BlockSpec((None, 1, LANES), lambda j, m: (l, 0, 0)),
        ],
        out_specs=[
            pl.BlockSpec((tm, tn), lambda j, m: (proj_m(j, m), proj_j(j))),
            pl.BlockSpec((tm, LANES), lambda j, m: (dt_m(j, m), 0)),
        ],
        out_shape=[
            jax.ShapeDtypeStruct((n_tok, HEAD_COLS + TAIL_COLS), F32),
            jax.ShapeDtypeStruct((n_tok, LANES), F32),
        ],
        scratch_shapes=[pltpu.VMEM((tn, d), BF16)],
        compiler_params=_cparams(2),
        name="inproj",
    )(u, w_in_t, w_in_t, dtb)


def _dwconv_rows(x, w, left):
    n_rows = x.shape[0]
    row = lax.broadcasted_iota(jnp.int32, x.shape, 0)
    out = None
    for k in range(w.shape[0]):
        off = k - left
        if off == 0:
            term = x
        else:
            shifted = pltpu.roll(x, (-off) % n_rows, axis=0)
            valid = (row < n_rows - off) if off > 0 else (row >= -off)
            term = jnp.where(valid, shifted, 0.0)
        term = term * w[k:k + 1, :]
        out = term if out is None else out + term
    return out


SSD_UNROLL = 4


def _ssd_kernel(*refs, seq_len, has_h0, n_alias, emit_state):
    (z_ref, xr_ref, br_ref, cr_ref, dt_ref, cwx_ref, cwb_ref, cwc_ref, cbx_ref, cbb_ref, cbc_ref,
     alog_ref, dsk_ref, gn_ref, sel_ref, e2_ref, tri_ref, trit_ref) = refs[:18]
    pos = 18
    h0_ref = None
    if has_h0:
        h0_ref = refs[pos]
        pos += 1
    pos += n_alias
    yn_ref = refs[pos]
    pos += 1
    hfin_ref = None
    if emit_state:
        hfin_ref = refs[pos]
        pos += 1
    xa_sc, ba_sc, ca_sc, dts_sc, a_sc, ht_sc, yg_sc, y_sc, ssq_sc = refs[pos:pos + 9]

    gi = pl.program_id(1)
    q = SSD_CHUNK
    nc = seq_len // q
    left = (SSD_CONV - 1) // 2
    hpg = HEADS_PER_GROUP
    gw = GROUP_WIDTH

    xa = _silu(_dwconv_rows(xr_ref[...], cwx_ref[...], left) + cbx_ref[...])
    xa_sc[...] = xa
    ba_sc[...] = _silu(_dwconv_rows(br_ref[...], cwb_ref[...], left) + cbb_ref[...])
    ca_sc[...] = _silu(_dwconv_rows(cr_ref[...], cwc_ref[...], left) + cbc_ref[...])
    yg_sc[...] = xa * dsk_ref[...]

    dts = _dot_r01(dt_ref[...], sel_ref[...])
    dts_sc[...] = dts
    a_sc[...] = dts * (-jnp.exp(alog_ref[...]))

    for d in range(2):
        if has_h0:
            for p in range(hpg // 2):
                blk = h0_ref[d, 2 * p:2 * p + 2].reshape(2 * SSD_HEAD_DIM, SSD_STATE)
                ht_sc[d, :, p * LANES:(p + 1) * LANES] = blk.T
        else:
            ht_sc[d] = jnp.zeros((SSD_STATE, gw), F32)

    ri = lax.broadcasted_iota(jnp.int32, (q, q), 0)
    ci = lax.broadcasted_iota(jnp.int32, (q, q), 1)
    lane_lo = lax.broadcasted_iota(jnp.int32, (q, LANES), 1) < SSD_HEAD_DIM
    neg_inf = jnp.float32(-jnp.inf)

    def chunk_dir(c, d):
        rows = pl.ds(pl.multiple_of(c * q, q), q)
        a_c = a_sc[rows, :]
        if d == 0:
            cs = _dot_l01(tri_ref[...], a_c)
            tot = cs[q - 1:q, :]
            mask = ri >= ci
        else:
            cs = _dot_l01(trit_ref[...], a_c)
            tot = cs[0:1, :]
            mask = ri <= ci
        cst = cs.T
        stack = jnp.concatenate([
            dts_sc[rows, :].astype(BF16),
            jnp.exp(cs).astype(BF16),
            jnp.exp(tot - cs).astype(BF16),
            jnp.broadcast_to(jnp.exp(tot), (BF16_ROWS, LANES)).astype(BF16)], axis=0)
        ex = jnp.dot(stack, e2_ref[:, d * gw:(d + 1) * gw], preferred_element_type=F32)
        dtx, ecs_x, dte_x, ea_x = ex[0:q], ex[q:2 * q], ex[2 * q:3 * q], ex[3 * q:3 * q + 1]
        b_c = ba_sc[rows, :]
        c_bf = ca_sc[rows, :].astype(BF16)
        xdt = xa_sc[rows, :] * dtx
        g = lax.dot_general(c_bf, b_c.astype(BF16), (((1,), (1,)), ((), ())),
                            preferred_element_type=F32)
        ht = ht_sc[d]
        y_off = jnp.dot(c_bf, ht.astype(BF16), preferred_element_type=F32) * ecs_x
        pieces = []
        for p in range(hpg // 2):
            ms = []
            for jj in range(2):
                r = d * hpg + 2 * p + jj
                diff = cs[:, r:r + 1] - cst[r:r + 1, :]
                ms.append((g * jnp.exp(jnp.where(mask, diff, neg_inf))).astype(BF16))
            xp = xdt[:, p * LANES:(p + 1) * LANES]
            rhs = jnp.concatenate([jnp.where(lane_lo, xp, 0.0), jnp.where(lane_lo, 0.0, xp)], axis=0)
            pieces.append(jnp.dot(jnp.concatenate(ms, axis=1), rhs.astype(BF16),
                                  preferred_element_type=F32))
        s_t = jnp.dot(b_c.T.astype(BF16), (xdt * dte_x).astype(BF16), preferred_element_type=F32)
        ht_sc[d] = ht * ea_x + s_t
        yg_sc[rows, :] += jnp.concatenate(pieces, axis=1) + y_off

    def body(c, carry):
        chunk_dir(c, 0)
        chunk_dir(nc - 1 - c, 1)
        return carry

    lax.fori_loop(0, nc, body, 0, unroll=min(nc, SSD_UNROLL))

    yz = yg_sc[...] * _silu(z_ref[...])
    y_sc[gi] = yz
    part = jnp.broadcast_to(jnp.sum(yz * yz, axis=-1, keepdims=True), ssq_sc.shape)

    @pl.when(gi == 0)
    def _():
        ssq_sc[...] = part

    @pl.when(gi > 0)
    def _():
        ssq_sc[...] += part

    @pl.when(gi == SSD_GROUPS - 1)
    def _():
        inv = lax.rsqrt(ssq_sc[:, 0:1] * (1.0 / SSD_INNER) + NORM_EPS)
        for gg in range(SSD_GROUPS):
            cols = slice(gg * gw, (gg + 1) * gw)
            yn_ref[:, cols] = (y_sc[gg] * inv * gn_ref[:, cols]).astype(BF16)

    if emit_state:
        for d in range(2):
            for p in range(hpg // 2):
                blk = ht_sc[d, :, p * LANES:(p + 1) * LANES].T
                hfin_ref[d, 2 * p:2 * p + 2] = blk.reshape(2, SSD_HEAD_DIM, SSD_STATE)


def _ssd_constants():
    hpg = HEADS_PER_GROUP
    sel = np.zeros((SSD_GROUPS, LANES, LANES), np.float32)
    for g in range(SSD_GROUPS):
        for d in range(2):
            for j in range(hpg):
                sel[g, d * SSD_HEADS + hpg * g + j, d * hpg + j] = 1.0
    e2 = np.zeros((LANES, 2 * GROUP_WIDTH), np.float32)
    for d in range(2):
        for j in range(hpg):
            lo = d * GROUP_WIDTH + j * SSD_HEAD_DIM
            e2[d * hpg + j, lo:lo + SSD_HEAD_DIM] = 1.0
    tri = np.tril(np.ones((SSD_CHUNK, SSD_CHUNK), np.float32))
    return (jnp.asarray(sel, BF16), jnp.asarray(e2, BF16), jnp.asarray(tri, BF16),
            jnp.asarray(tri.T, BF16))


def _ssd(head, dt, conv_w, conv_b3, alog_g, dsk_g, norm_g3, consts, l, *, row0, n_seq, seq_len,
         h0=None, prev_y=None, prev_state=None, state_shape=None):
    n_tok = head.shape[0]
    sel, e2, tri, trit = consts
    assert row0 % seq_len == 0 and seq_len % (2 * SSD_CHUNK) == 0
    blk0 = row0 // seq_len
    gw = GROUP_WIDTH
    has_h0 = h0 is not None
    emit_state = state_shape is not None
    xcol = SSD_INNER // gw
    bcol = (SSD_INNER + SSD_INNER) // LANES
    ccol = bcol + SSD_GROUPS
    cwb = SSD_INNER // LANES
    cwc = cwb + SSD_GROUPS
    in_specs = [
        pl.BlockSpec((seq_len, gw), lambda b, g: (blk0 + b, g)),
        pl.BlockSpec((seq_len, gw), lambda b, g: (blk0 + b, xcol + g)),
        pl.BlockSpec((seq_len, LANES), lambda b, g: (blk0 + b, bcol + g)),
        pl.BlockSpec((seq_len, LANES), lambda b, g: (blk0 + b, ccol + g)),
        pl.BlockSpec((seq_len, LANES), lambda b, g: (blk0 + b, 0)),
        pl.BlockSpec((None, SSD_CONV, gw), lambda b, g: (l, 0, g)),
        pl.BlockSpec((None, SSD_CONV, LANES), lambda b, g: (l, 0, cwb + g)),
        pl.BlockSpec((None, SSD_CONV, LANES), lambda b, g: (l, 0, cwc + g)),
        pl.BlockSpec((None, 1, gw), lambda b, g: (l, 0, g)),
        pl.BlockSpec((None, 1, LANES), lambda b, g: (l, 0, cwb + g)),
        pl.BlockSpec((None, 1, LANES), lambda b, g: (l, 0, cwc + g)),
        pl.BlockSpec((None, None, 1, LANES), lambda b, g: (l, g, 0, 0)),
        pl.BlockSpec((None, None, 1, gw), lambda b, g: (l, g, 0, 0)),
        pl.BlockSpec((None, 1, SSD_INNER), lambda b, g: (l, 0, 0)),
        pl.BlockSpec((None, LANES, LANES), lambda b, g: (g, 0, 0)),
        pl.BlockSpec((LANES, 2 * gw), lambda b, g: (0, 0)),
        pl.BlockSpec((SSD_CHUNK, SSD_CHUNK), lambda b, g: (0, 0)),
        pl.BlockSpec((SSD_CHUNK, SSD_CHUNK), lambda b, g: (0, 0)),
    ]
    args = [head, head, head, head, dt, conv_w, conv_w, conv_w, conv_b3, conv_b3, conv_b3,
            alog_g, dsk_g, norm_g3, sel, e2, tri, trit]
    if has_h0:
        in_specs.append(pl.BlockSpec((None, None, 2, HEADS_PER_GROUP, SSD_HEAD_DIM, SSD_STATE),
                                     lambda b, g: (b, l, 0, g, 0, 0)))
        args.append(h0)
    aliases = {}
    if prev_y is not None:
        in_specs.append(pl.BlockSpec(memory_space=pl.ANY))
        aliases[len(args)] = 0
        args.append(prev_y)
    if prev_state is not None:
        assert emit_state
        in_specs.append(pl.BlockSpec(memory_space=pl.ANY))
        aliases[len(args)] = 1
        args.append(prev_state)
    out_specs = [pl.BlockSpec((seq_len, SSD_INNER), lambda b, g: (blk0 + b, 0))]
    out_shape = [jax.ShapeDtypeStruct((n_tok, SSD_INNER), BF16)]
    if emit_state:
        out_specs.append(pl.BlockSpec((None, None, 2, HEADS_PER_GROUP, SSD_HEAD_DIM, SSD_STATE),
                                      lambda b, g: (b, l, 0, g, 0, 0)))
        out_shape.append(jax.ShapeDtypeStruct(state_shape, F32))
    kern = functools.partial(_ssd_kernel, seq_len=seq_len, has_h0=has_h0, n_alias=len(aliases),
                             emit_state=emit_state)
    return pl.pallas_call(
        kern,
        grid=(n_seq, SSD_GROUPS),
        in_specs=in_specs,
        out_specs=out_specs,
        out_shape=out_shape,
        input_output_aliases=aliases,
        scratch_shapes=[
            pltpu.VMEM((seq_len, gw), F32),
            pltpu.VMEM((seq_len, LANES), F32),
            pltpu.VMEM((seq_len, LANES), F32),
            pltpu.VMEM((seq_len, LANES), F32),
            pltpu.VMEM((seq_len, LANES), F32),
            pltpu.VMEM((2, SSD_STATE, gw), F32),
            pltpu.VMEM((seq_len, gw), F32),
            pltpu.VMEM((SSD_GROUPS, seq_len, gw), F32),
            pltpu.VMEM((seq_len, LANES), F32),
        ],
        compiler_params=_cparams(2),
        name="ssd_lat" if has_h0 else "ssd_ctx",
    )(*args)


def _scft_kernel(*refs, n_alias, n_grp):
    b_ref, c_ref, x_ref, f_ref, cw_ref, cl_ref, sl_ref, cc_ref, sc_ref = refs[:9]
    pos = 9 + n_alias
    ysc_ref, yft_ref = refs[pos], refs[pos + 1]
    v = c_ref[...] * x_ref[...]
    ysc_ref[...] = (b_ref[...] * _dwconv_rows(v, cw_ref[...], (SC_CONV - 1) // 2)).astype(BF16)
    tw = FT_GROUP_DIM
    for gg in range(n_grp):
        cols = slice(gg * tw, (gg + 1) * tw)
        u = f_ref[:, cols].astype(BF16)
        p = jnp.dot(u, cc_ref[...], preferred_element_type=F32).astype(BF16)
        s = jnp.dot(u, sc_ref[...], preferred_element_type=F32).astype(BF16)
        yft_ref[:, cols] = (jnp.dot(cl_ref[...], p, preferred_element_type=F32)
                            - jnp.dot(sl_ref[...], s, preferred_element_type=F32)).astype(BF16)


def _dft_mats(n):
    k = np.arange(n)
    ang = 2.0 * np.pi * ((k[:, None] * k[None, :]) % n) / n
    scale = 1.0 / math.sqrt(n)
    return jnp.asarray(np.cos(ang) * scale, BF16), jnp.asarray(np.sin(ang) * scale, BF16)


def _scft(tail, sc_conv_w, l, *, row0, n_seq, seq_len, n_grp, prev=None):
    n_tok = tail.shape[0]
    tw = n_grp * FT_GROUP_DIM
    assert row0 % seq_len == 0 and SC_WIDTH % tw == 0 and HEAD_COLS % tw == 0
    blk0 = row0 // seq_len
    nt = SC_WIDTH // tw
    c0 = HEAD_COLS // tw
    cl, sl = _dft_mats(seq_len)
    cc, sc = _dft_mats(FT_GROUP_DIM)
    in_specs = [
        pl.BlockSpec((seq_len, tw), lambda b, j: (blk0 + b, c0 + j)),
        pl.BlockSpec((seq_len, tw), lambda b, j: (blk0 + b, c0 + nt + j)),
        pl.BlockSpec((seq_len, tw), lambda b, j: (blk0 + b, c0 + 2 * nt + j)),
        pl.BlockSpec((seq_len, tw), lambda b, j: (blk0 + b, c0 + 3 * nt + j)),
        pl.BlockSpec((None, SC_CONV, tw), lambda b, j: (l, 0, j)),
        pl.BlockSpec((seq_len, seq_len), lambda b, j: (0, 0)),
        pl.BlockSpec((seq_len, seq_len), lambda b, j: (0, 0)),
        pl.BlockSpec((FT_GROUP_DIM, FT_GROUP_DIM), lambda b, j: (0, 0)),
        pl.BlockSpec((FT_GROUP_DIM, FT_GROUP_DIM), lambda b, j: (0, 0)),
    ]
    args = [tail, tail, tail, tail, sc_conv_w, cl, sl, cc, sc]
    aliases = {}
    if prev is not None:
        in_specs += [pl.BlockSpec(memory_space=pl.ANY), pl.BlockSpec(memory_space=pl.ANY)]
        aliases = {len(args): 0, len(args) + 1: 1}
        args += list(prev)
    kern = functools.partial(_scft_kernel, n_alias=len(aliases), n_grp=n_grp)
    return pl.pallas_call(
        kern,
        grid=(n_seq, nt),
        in_specs=in_specs,
        out_specs=[
            pl.BlockSpec((seq_len, tw), lambda b, j: (blk0 + b, j)),
            pl.BlockSpec((seq_len, tw), lambda b, j: (blk0 + b, j)),
        ],
        out_shape=[
            jax.ShapeDtypeStruct((n_tok, SC_WIDTH), BF16),
            jax.ShapeDtypeStruct((n_tok, FT_WIDTH), BF16),
        ],
        input_output_aliases=aliases,
        compiler_params=_cparams(2),
        name="scft_lat" if prev is not None else "scft_ctx",
    )(*args)


def _mixout_kernel(a0_ref, a1_ref, a2_ref, g0_ref, g1_ref, g2_ref, w0_ref, w1_ref, w2_ref, wo_ref,
                   x_ref, mod_ref, ng_ref, o_ref, un_ref, acc_sc, m_sc, *, nk):
    k = pl.program_id(1)
    te = o_ref.shape[0]

    def merge():
        m = jax.nn.sigmoid(g0_ref[...]) * jnp.dot(a0_ref[...], w0_ref[...].astype(BF16),
                                                   preferred_element_type=F32)
        m += jax.nn.sigmoid(g1_ref[...]) * jnp.dot(a1_ref[...], w1_ref[...].astype(BF16),
                                                    preferred_element_type=F32)
        m += jax.nn.sigmoid(g2_ref[...]) * jnp.dot(a2_ref[...], w2_ref[...].astype(BF16),
                                                    preferred_element_type=F32)
        m_sc[k % 2] = m.astype(BF16)

    def project():
        return jnp.dot(m_sc[(k + 1) % 2], wo_ref[...].astype(BF16), preferred_element_type=F32)

    @pl.when(k == 0)
    def _():
        merge()

    @pl.when(k == 1)
    def _():
        acc_sc[...] = project()
        merge()

    @pl.when(jnp.logical_and(k > 1, k < nk))
    def _():
        acc_sc[...] += project()
        merge()

    @pl.when(k == nk)
    def _():
        acc_sc[...] += project()

    @pl.when(k >= nk)
    def _():
        r0 = pl.multiple_of((k - nk) * te, te)
        y = _rms(acc_sc[pl.ds(r0, te), :], ng_ref[3:4, :])
        x_new = x_ref[...] + mod_ref[5:6, :] * y
        o_ref[...] = x_new
        un_ref[...] = _norm_mod(x_new, ng_ref[4:5, :], mod_ref[6:7, :], mod_ref[7:8, :]).astype(BF16)


def _mixout(yssd, ysc, yft, tail, w_br_ssd, w_br_sc, w_br_ft, w_out, x, mod4, norm_g, l, rows,
            tm=1024, tk=256):
    n_tok, d = x.shape
    nk = d // tk
    gate0 = (HEAD_COLS + 3 * SC_WIDTH + FT_WIDTH) // tk
    per = d // tk
    grp = lambda i: rows.group_of_tile(i, tm)
    kk = lambda k: jnp.minimum(k, nk - 1)
    te = EPILOGUE_ROWS
    ne = tm // te
    erow = lambda i, k: (ne * i + jnp.clip(k - nk, 0, ne - 1), 0)
    kern = functools.partial(_mixout_kernel, nk=nk)
    return pl.pallas_call(
        kern,
        grid=(n_tok // tm, nk + ne),
        in_specs=[
            pl.BlockSpec((tm, SSD_INNER), lambda i, k: (i, 0)),
            pl.BlockSpec((tm, SC_WIDTH), lambda i, k: (i, 0)),
            pl.BlockSpec((tm, FT_WIDTH), lambda i, k: (i, 0)),
            pl.BlockSpec((tm, tk), lambda i, k: (i, gate0 + kk(k))),
            pl.BlockSpec((tm, tk), lambda i, k: (i, gate0 + per + kk(k))),
            pl.BlockSpec((tm, tk), lambda i, k: (i, gate0 + 2 * per + kk(k))),
            pl.BlockSpec((None, SSD_INNER, tk), lambda i, k: (l, 0, kk(k))),
            pl.BlockSpec((None, SC_WIDTH, tk), lambda i, k: (l, 0, kk(k))),
            pl.BlockSpec((None, FT_WIDTH, tk), lambda i, k: (l, 0, kk(k))),
            pl.BlockSpec((None, tk, d), lambda i, k: (l, jnp.clip(k - 1, 0, nk - 1), 0)),
            pl.BlockSpec((te, d), erow),
            pl.BlockSpec((None, None, N_MOD, d), lambda i, k: (l, grp(i), 0, 0)),
            pl.BlockSpec((None, 6, d), lambda i, k: (l, 0, 0)),
        ],
        out_specs=[pl.BlockSpec((te, d), erow), pl.BlockSpec((te, d), erow)],
        out_shape=[jax.ShapeDtypeStruct((n_tok, d), F32), jax.ShapeDtypeStruct((n_tok, d), BF16)],
        scratch_shapes=[pltpu.VMEM((tm, d), F32), pltpu.VMEM((2, tm, tk), BF16)],
        compiler_params=_cparams(2),
        name="mixout",
    )(yssd, ysc, yft, tail, tail, tail, w_br_ssd, w_br_sc, w_br_ft, w_out, x, mod4, norm_g)


def _grid_pos_emb(n_tok):
    rows = n_tok // GRID_W
    t = np.arange(rows * GRID_W)
    r = (t // GRID_W).astype(np.float32)[:, None]
    col = (t % GRID_W).astype(np.float32)[:, None]
    nf = D_MODEL // 4
    omega = (1.0 / (np.float32(POS_BASE) ** (np.arange(nf, dtype=np.float32) / np.float32(nf)))).astype(np.float32)
    ro = (r * omega).astype(np.float32).astype(np.float64)
    co = (col * omega).astype(np.float32).astype(np.float64)
    return np.concatenate([np.sin(ro), np.cos(ro), np.sin(co), np.cos(co)], axis=-1).astype(np.float32)


def _group_lanes(p, hpg=HEADS_PER_GROUP):
    depth = p.shape[0]
    t = p.reshape(depth, 2, SSD_GROUPS, hpg).transpose(0, 2, 1, 3).reshape(depth, SSD_GROUPS, 1, 2 * hpg)
    return jnp.pad(t, ((0, 0), (0, 0), (0, 0), (0, LANES - 2 * hpg)))


def kernel(x_prompt, x_sample, state_ssd, c, c_ctx, ada_w, ada_b, norm_g, ffn1_wgu, ffn1_wd, w_in,
           ssd_conv_w, ssd_conv_b, ssd_dt_bias, ssd_a_log, ssd_d, ssd_norm_g, sc_conv_w,
           w_br_ssd, w_br_sc, w_br_ft, w_out, ffn2_wgu, ffn2_wd):
    n_ctx, ctx_len, d = x_prompt.shape
    n_lat, lat_len, _ = x_sample.shape
    depth = ada_w.shape[0]
    rows = _Rows(n_ctx * ctx_len, n_lat, lat_len)

    c8 = jnp.concatenate([c_ctx[None, :], c, jnp.zeros((8 - 1 - n_lat, d), F32)], axis=0)
    mod4 = _modulation(c8, ada_w, ada_b)[:, :1 + n_lat].reshape(depth, 1 + n_lat, N_MOD, d)

    x, u = _embed(x_prompt.reshape(n_ctx * ctx_len, d), x_sample.reshape(n_lat * lat_len, d),
                  jnp.asarray(_grid_pos_emb(lat_len)), mod4, norm_g, rows)

    consts = _ssd_constants()
    w_in_t = jnp.swapaxes(w_in, 1, 2)
    conv_b3 = ssd_conv_b.reshape(depth, 1, SSD_XBC)
    dtb = jnp.pad(ssd_dt_bias.reshape(depth, 1, DT_WIDTH), ((0, 0), (0, 0), (0, LANES - DT_WIDTH)))
    alog_g = _group_lanes(ssd_a_log)
    dsk_g = jnp.repeat(ssd_d, SSD_HEAD_DIM, axis=-1).reshape(depth, SSD_GROUPS, 1, GROUP_WIDTH)
    ssd_norm_g3 = ssd_norm_g.reshape(depth, 1, SSD_INNER)
    state_shape = (n_ctx, depth, 2, SSD_HEADS, SSD_HEAD_DIM, SSD_STATE)

    states = None
    for l in range(depth):
        x, u = _ffn(x, u, mod4, norm_g, ffn1_wgu, ffn1_wd, l, 0, rows, "same")
        proj, dt = _inproj(u, w_in_t, dtb, l)
        tail = proj
        ssd_args = (proj, dt, ssd_conv_w, conv_b3, alog_g, dsk_g, ssd_norm_g3, consts, l)
        yssd, states = _ssd(*ssd_args, row0=0, n_seq=n_ctx, seq_len=ctx_len,
                            prev_state=states, state_shape=state_shape)
        (yssd,) = _ssd(*ssd_args, row0=rows.n_ctx_tok, n_seq=n_lat, seq_len=lat_len,
                       h0=state_ssd, prev_y=yssd)
        ysc, yft = _scft(tail, sc_conv_w, l, row0=0, n_seq=n_ctx, seq_len=ctx_len, n_grp=FT_GROUPS)
        ysc, yft = _scft(tail, sc_conv_w, l, row0=rows.n_ctx_tok, n_seq=n_lat, seq_len=lat_len,
                         n_grp=1, prev=(ysc, yft))
        x, u = _mixout(yssd, ysc, yft, tail, w_br_ssd, w_br_sc, w_br_ft, w_out, x, mod4, norm_g, l, rows)
        if l + 1 < depth:
            x, u = _ffn(x, u, mod4, norm_g, ffn2_wgu, ffn2_wd, l, 1, rows, "layer")
        else:
            y_ctx, y_lat = _ffn(x, u, mod4, norm_g, ffn2_wgu, ffn2_wd, l, 1, rows, "final")

    return (y_ctx.reshape(n_ctx, ctx_len, d), y_lat.reshape(n_lat, lat_len, d), states)
```

```python
import functools
import math

import numpy as np
import jax
import jax.numpy as jnp
from jax import lax
from jax.experimental import pallas as pl
from jax.experimental.pallas import tpu as pltpu

F32 = jnp.float32
BF16 = jnp.bfloat16

D_MODEL = 2048
DEPTH = 4
GRID_W = 64
POS_BASE = 10000.0
NORM_EPS = 1e-6
SSD_HEADS = 32
SSD_HEAD_DIM = 64
SSD_INNER = SSD_HEADS * SSD_HEAD_DIM
SSD_GROUPS = 4
SSD_STATE = 128
SSD_CONV = 4
SSD_CHUNK = 128
SSD_XBC = SSD_INNER + 2 * SSD_GROUPS * SSD_STATE
SC_WIDTH = 1024
SC_CONV = 3
FT_WIDTH = 1024
FT_GROUPS = 4
FT_GROUP_DIM = FT_WIDTH // FT_GROUPS
D_FF = 5504
N_BRANCH = 3
N_MOD = 9
HEADS_PER_GROUP = SSD_HEADS // SSD_GROUPS
GROUP_WIDTH = HEADS_PER_GROUP * SSD_HEAD_DIM
HEAD_COLS = SSD_INNER + SSD_XBC
DT_COL = HEAD_COLS
DT_WIDTH = 2 * SSD_HEADS
TAIL_COL = HEAD_COLS + DT_WIDTH
TAIL_COLS = 3 * SC_WIDTH + FT_WIDTH + N_BRANCH * D_MODEL
IN_COLS = TAIL_COL + TAIL_COLS

LANES = 128
BF16_ROWS = 16
VMEM_LIMIT_BYTES = 60 * 1024 * 1024


def _cparams(n_axes):
    return pltpu.CompilerParams(dimension_semantics=("arbitrary",) * n_axes,
                                vmem_limit_bytes=VMEM_LIMIT_BYTES)


def _silu(x):
    return x * jax.nn.sigmoid(x)


def _rms(x, g):
    ms = jnp.mean(x * x, axis=-1, keepdims=True)
    return x * lax.rsqrt(ms + NORM_EPS) * g


def _norm_mod(x, g, shift, scale):
    return _rms(x, g) * (1.0 + scale) + shift


def _split_bf16(x, n):
    parts = []
    r = x
    for i in range(n):
        p = r.astype(BF16)
        parts.append(p)
        if i + 1 < n:
            r = r - p.astype(F32)
    return parts


def _dot_r01(x, m01, n=3):
    acc = None
    for p in _split_bf16(x, n):
        t = jnp.dot(p, m01, preferred_element_type=F32)
        acc = t if acc is None else acc + t
    return acc


def _dot_l01(m01, x, n=3):
    acc = None
    for p in _split_bf16(x, n):
        t = jnp.dot(m01, p, preferred_element_type=F32)
        acc = t if acc is None else acc + t
    return acc


def _mod_kernel(c_ref, w_ref, b_ref, o_ref):
    s = _silu(c_ref[...]).astype(BF16)
    o_ref[...] = jnp.dot(s, w_ref[...].astype(BF16), preferred_element_type=F32) + b_ref[...]


def _modulation(c8, ada_w, ada_b, tn=1024):
    depth, d, n = ada_w.shape
    return pl.pallas_call(
        _mod_kernel,
        grid=(depth, n // tn),
        in_specs=[
            pl.BlockSpec((8, d), lambda l, j: (0, 0)),
            pl.BlockSpec((None, d, tn), lambda l, j: (l, 0, j)),
            pl.BlockSpec((None, 1, tn), lambda l, j: (l, 0, j)),
        ],
        out_specs=pl.BlockSpec((None, 8, tn), lambda l, j: (l, 0, j)),
        out_shape=jax.ShapeDtypeStruct((depth, 8, n), F32),
        compiler_params=_cparams(2),
        name="adaln_mod",
    )(c8, ada_w, ada_b.reshape(depth, 1, n))


class _Rows:
    def __init__(self, n_ctx_tok, n_lat, lat_len):
        self.n_ctx_tok = n_ctx_tok
        self.n_lat = n_lat
        self.lat_len = lat_len
        self.n_tok = n_ctx_tok + n_lat * lat_len

    def group_of_tile(self, i, tm):
        assert self.n_ctx_tok % tm == 0 and self.lat_len % tm == 0
        n_ctx_tiles = self.n_ctx_tok // tm
        per_lat = self.lat_len // tm
        return jnp.where(i < n_ctx_tiles, 0, 1 + (i - n_ctx_tiles) // per_lat)


def _single(block_shape, index_map):
    return pl.BlockSpec(block_shape, index_map, pipeline_mode=pl.Buffered(1))


FFN_TF = 256
FFN_SPLIT = 1
EPILOGUE_ROWS = 256


def _ffn_kernel(*refs, nk, mod_row, ng_row, next_kind, n_ctx_tiles):
    x_ref, u_ref, mod_ref, ng_ref, wg_ref, wu_ref, wd_ref = refs[:7]
    pos = 7
    if next_kind == "layer":
        modn_ref, ngn_ref = refs[pos:pos + 2]
        pos += 2
    o_ref = refs[pos]
    pos += 1
    if next_kind == "final":
        olat_ref = refs[pos]
    else:
        un_ref = refs[pos]
    pos += 1
    acc_sc, wup_sc, wdn_sc = refs[pos:pos + 3]
    k = pl.program_id(1)
    tf = FFN_TF
    te = o_ref.shape[0]

    @pl.when(k == 0)
    def _():
        acc_sc[...] = jnp.zeros_like(acc_sc)

    @pl.when(k < nk)
    def _():
        ts = tf // FFN_SPLIT
        for s in range(FFN_SPLIT):
            wup_sc[:, 2 * s * ts:(2 * s + 1) * ts] = wg_ref[:, s * ts:(s + 1) * ts].astype(BF16)
            wup_sc[:, (2 * s + 1) * ts:(2 * s + 2) * ts] = wu_ref[:, s * ts:(s + 1) * ts].astype(BF16)
        wdn_sc[...] = wd_ref[...].astype(BF16)

        overlap = nk * tf - D_FF
        assert overlap <= ts
        for s in range(FFN_SPLIT):
            h = jnp.dot(u_ref[...], wup_sc[:, 2 * s * ts:(2 * s + 2) * ts], preferred_element_type=F32)
            a = _silu(h[:, :ts]) * h[:, ts:]
            if s == 0:
                lane = lax.broadcasted_iota(jnp.int32, a.shape, 1)
                a = jnp.where(jnp.logical_and(k == nk - 1, lane < overlap), 0.0, a)
            acc_sc[...] += jnp.dot(a.astype(BF16), wdn_sc[s * ts:(s + 1) * ts, :], preferred_element_type=F32)

    @pl.when(k >= nk)
    def _():
        r0 = pl.multiple_of((k - nk) * te, te)
        y = _rms(acc_sc[pl.ds(r0, te), :], ng_ref[ng_row + 1:ng_row + 2, :])
        x_new = x_ref[...] + 0.5 * mod_ref[mod_row + 2:mod_row + 3, :] * y
        if next_kind == "final":
            is_ctx = pl.program_id(0) < n_ctx_tiles

            @pl.when(is_ctx)
            def _():
                o_ref[...] = x_new

            @pl.when(jnp.logical_not(is_ctx))
            def _():
                olat_ref[...] = x_new
        else:
            o_ref[...] = x_new
        if next_kind == "same":
            un = _norm_mod(x_new, ng_ref[ng_row + 2:ng_row + 3, :],
                           mod_ref[mod_row + 3:mod_row + 4, :], mod_ref[mod_row + 4:mod_row + 5, :])
            un_ref[...] = un.astype(BF16)
        elif next_kind == "layer":
            un = _norm_mod(x_new, ngn_ref[0:1, :], modn_ref[0:1, :], modn_ref[1:2, :])
            un_ref[...] = un.astype(BF16)


def _ffn(x, u, mod4, norm_g, wgu, wd, l, which, rows, next_kind, tm=2048):
    n_tok, d = x.shape
    tf = FFN_TF
    nk = pl.cdiv(D_FF, tf)
    mod_row = 0 if which == 0 else 6
    ng_row = 0 if which == 0 else 4
    te = EPILOGUE_ROWS
    ne = tm // te
    assert rows.n_ctx_tok % tm == 0 and n_tok % tm == 0
    gran = math.gcd(rows.lat_len, tm)
    grp = lambda i, k: rows.group_of_tile((tm * i + jnp.clip(k - nk, 0, ne - 1) * te) // gran, gran)
    assert tf % LANES == 0 and D_FF % LANES == 0
    col = lambda k, base=0: LANES * (base // LANES + jnp.minimum(k * (tf // LANES), (D_FF - tf) // LANES))
    assert next_kind in ("same", "layer", "final")
    kern = functools.partial(_ffn_kernel, nk=nk, mod_row=mod_row, ng_row=ng_row, next_kind=next_kind,
                             n_ctx_tiles=rows.n_ctx_tok // tm)
    erow = lambda i, k: (ne * i + jnp.clip(k - nk, 0, ne - 1), 0)
    in_specs = [
        pl.BlockSpec((te, d), erow),
        _single((tm, d), lambda i, k: (i, 0)),
        pl.BlockSpec((None, None, N_MOD, d), lambda i, k: (l, grp(i, k), 0, 0)),
        pl.BlockSpec((None, 6, d), lambda i, k: (l, 0, 0)),
        pl.BlockSpec((pl.squeezed, pl.Element(d), pl.Element(tf)), lambda i, k: (l, 0, col(k))),
        pl.BlockSpec((pl.squeezed, pl.Element(d), pl.Element(tf)), lambda i, k: (l, 0, col(k, D_FF))),
        pl.BlockSpec((pl.squeezed, pl.Element(tf), pl.Element(d)), lambda i, k: (l, col(k), 0)),
    ]
    args = [x, u, mod4, norm_g, wgu, wgu, wd]
    if next_kind == "layer":
        in_specs += [pl.BlockSpec((None, None, N_MOD, d), lambda i, k: (l + 1, grp(i, k), 0, 0)),
                     pl.BlockSpec((None, 6, d), lambda i, k: (l + 1, 0, 0))]
        args += [mod4, norm_g]
    if next_kind == "final":
        n_ctx_blocks = rows.n_ctx_tok // te
        eblk = lambda i, k: ne * i + jnp.clip(k - nk, 0, ne - 1)
        out_specs = [pl.BlockSpec((te, d), lambda i, k: (jnp.minimum(eblk(i, k), n_ctx_blocks - 1), 0)),
                     pl.BlockSpec((te, d), lambda i, k: (jnp.maximum(eblk(i, k) - n_ctx_blocks, 0), 0))]
        out_shape = [jax.ShapeDtypeStruct((rows.n_ctx_tok, d), F32),
                     jax.ShapeDtypeStruct((n_tok - rows.n_ctx_tok, d), F32)]
    else:
        out_specs = [pl.BlockSpec((te, d), erow), pl.BlockSpec((te, d), erow)]
        out_shape = [jax.ShapeDtypeStruct((n_tok, d), F32), jax.ShapeDtypeStruct((n_tok, d), BF16)]
    return pl.pallas_call(
        kern,
        grid=(n_tok // tm, nk + ne),
        in_specs=in_specs,
        out_specs=out_specs,
        out_shape=out_shape,
        scratch_shapes=[
            pltpu.VMEM((tm, d), F32),
            pltpu.VMEM((d, 2 * tf), BF16),
            pltpu.VMEM((tf, d), BF16),
        ],
        compiler_params=_cparams(2),
        name="ffn",
    )(*args)


def _embed_kernel(xp_ref, xs_ref, pe_ref, mod_ref, ng_ref, x_ref, u_ref, *, n_ctx_tiles):
    i = pl.program_id(0)

    @pl.when(i < n_ctx_tiles)
    def _():
        x_ref[...] = xp_ref[...]

    @pl.when(i >= n_ctx_tiles)
    def _():
        x_ref[...] = xs_ref[...] + pe_ref[...]

    u_ref[...] = _norm_mod(x_ref[...], ng_ref[0:1, :], mod_ref[0:1, :], mod_ref[1:2, :]).astype(BF16)


def _embed(xp, xs, pe, mod4, norm_g, rows, tm=256):
    d = xp.shape[1]
    assert rows.lat_len % tm == 0 and rows.n_ctx_tok % tm == 0
    n_ctx_tiles = rows.n_ctx_tok // tm
    pe_tiles = rows.lat_len // tm
    grp = lambda i: rows.group_of_tile(i, tm)
    kern = functools.partial(_embed_kernel, n_ctx_tiles=n_ctx_tiles)
    return pl.pallas_call(
        kern,
        grid=(rows.n_tok // tm,),
        in_specs=[
            pl.BlockSpec((tm, d), lambda i: (jnp.minimum(i, n_ctx_tiles - 1), 0)),
            pl.BlockSpec((tm, d), lambda i: (jnp.maximum(i - n_ctx_tiles, 0), 0)),
            pl.BlockSpec((tm, d), lambda i: (jnp.maximum(i - n_ctx_tiles, 0) % pe_tiles, 0)),
            pl.BlockSpec((None, None, N_MOD, d), lambda i: (0, grp(i), 0, 0)),
            pl.BlockSpec((None, 6, d), lambda i: (0, 0, 0)),
        ],
        out_specs=[pl.BlockSpec((tm, d), lambda i: (i, 0)), pl.BlockSpec((tm, d), lambda i: (i, 0))],
        out_shape=[jax.ShapeDtypeStruct((rows.n_tok, d), F32), jax.ShapeDtypeStruct((rows.n_tok, d), BF16)],
        compiler_params=_cparams(1),
        name="embed",
    )(xp, xs, pe, mod4, norm_g)


def _dot_nt(a, b):
    return lax.dot_general(a, b, (((1,), (1,)), ((), ())), preferred_element_type=F32)


def _softplus(x):
    return jnp.maximum(x, 0.0) + jnp.log1p(jnp.exp(-jnp.abs(x)))


def _inproj_kernel(u_ref, wa_ref, wb_ref, dtb_ref, op_ref, odt_ref, w_sc, *, n_head, tm):
    j = pl.program_id(0)
    m = pl.program_id(1)
    tn = w_sc.shape[0]
    off = DT_WIDTH

    @pl.when(jnp.logical_and(m == 0, j < n_head))
    def _():
        w_sc[...] = wa_ref[...].astype(BF16)

    @pl.when(jnp.logical_and(m == 0, j == n_head))
    def _():
        w_sc[:off, :] = wb_ref[...].astype(BF16)

    @pl.when(jnp.logical_and(m == 0, j > n_head))
    def _():
        w_sc[:tn - off, :] = wa_ref[off:, :].astype(BF16)
        w_sc[tn - off:, :] = wb_ref[...].astype(BF16)

    u = u_ref[pl.ds(pl.multiple_of(m * tm, tm), tm), :]

    @pl.when(j != n_head)
    def _():
        op_ref[...] = _dot_nt(u, w_sc[...])

    @pl.when(j == n_head)
    def _():
        odt_ref[...] = _softplus(_dot_nt(u, w_sc[:LANES, :]) + dtb_ref[...])


def _inproj(u, w_in_t, dtb, l, tm=2048, tn=512):
    n_tok, d = u.shape
    off = DT_WIDTH
    assert TAIL_COL % tn == off and HEAD_COLS % tn == 0 and TAIL_COLS % tn == 0 and tn % off == 0
    n_head = HEAD_COLS // tn
    n_tail = TAIL_COLS // tn
    n_m = n_tok // tm
    sub = tn // off
    proj_j = lambda j: jnp.where(j < n_head, j, jnp.where(j == n_head, n_head - 1, j - 1))
    proj_m = lambda j, m: jnp.where(j == n_head, n_m - 1, m)
    dt_m = lambda j, m: jnp.where(j < n_head, 0, jnp.where(j == n_head, m, n_m - 1))
    wide_j = lambda j: jnp.where(j <= n_head, jnp.minimum(j, n_head), j - 1)
    narrow_j = lambda j: jnp.where(j <= n_head, DT_COL // off, j * sub)
    kern = functools.partial(_inproj_kernel, n_head=n_head, tm=tm)
    return pl.pallas_call(
        kern,
        grid=(n_head + 1 + n_tail, n_m),
        in_specs=[
            _single((n_tok, d), lambda j, m: (0, 0)),
            pl.BlockSpec((None, tn, d), lambda j, m: (l, wide_j(j), 0)),
            pl.BlockSpec((None, off, d), lambda j, m: (l, narrow_j(j), 0)),
            pl.BlockSpec((None, 1, LANES), lambda j, m: (l, 0, 0)),
        ],
        out_specs=[
            pl.BlockSpec((tm, tn), lambda j, m: (proj_m(j, m), proj_j(j))),
            pl.BlockSpec((tm, LANES), lambda j, m: (dt_m(j, m), 0)),
        ],
        out_shape=[
            jax.ShapeDtypeStruct((n_tok, HEAD_COLS + TAIL_COLS), F32),
            jax.ShapeDtypeStruct((n_tok, LANES), F32),
        ],
        scratch_shapes=[pltpu.VMEM((tn, d), BF16)],
        compiler_params=_cparams(2),
        name="inproj",
    )(u, w_in_t, w_in_t, dtb)


def _dwconv_rows(x, w, left):
    n_rows = x.shape[0]
    row = lax.broadcasted_iota(jnp.int32, x.shape, 0)
    out = None
    for k in range(w.shape[0]):
        off = k - left
        if off == 0:
            term = x
        else:
            shifted = pltpu.roll(x, (-off) % n_rows, axis=0)
            valid = (row < n_rows - off) if off > 0 else (row >= -off)
            term = jnp.where(valid, shifted, 0.0)
        term = term * w[k:k + 1, :]
        out = term if out is None else out + term
    return out


SSD_UNROLL = 4


def _ssd_kernel(*refs, seq_len, has_h0, n_alias, emit_state):
    (z_ref, xr_ref, br_ref, cr_ref, dt_ref, cwx_ref, cwb_ref, cwc_ref, cbx_ref, cbb_ref, cbc_ref,
     alog_ref, dsk_ref, gn_ref, sel_ref, e2_ref, tri_ref, trit_ref) = refs[:18]
    pos = 18
    h0_ref = None
    if has_h0:
        h0_ref = refs[pos]
        pos += 1
    pos += n_alias
    yn_ref = refs[pos]
    pos += 1
    hfin_ref = None
    if emit_state:
        hfin_ref = refs[pos]
        pos += 1
    xa_sc, ba_sc, ca_sc, dts_sc, a_sc, ht_sc, yg_sc, y_sc, ssq_sc = refs[pos:pos + 9]

    gi = pl.program_id(1)
    q = SSD_CHUNK
    nc = seq_len // q
    left = (SSD_CONV - 1) // 2
    hpg = HEADS_PER_GROUP
    gw = GROUP_WIDTH

    xa = _silu(_dwconv_rows(xr_ref[...], cwx_ref[...], left) + cbx_ref[...])
    xa_sc[...] = xa
    ba_sc[...] = _silu(_dwconv_rows(br_ref[...], cwb_ref[...], left) + cbb_ref[...])
    ca_sc[...] = _silu(_dwconv_rows(cr_ref[...], cwc_ref[...], left) + cbc_ref[...])
    yg_sc[...] = xa * dsk_ref[...]

    dts = _dot_r01(dt_ref[...], sel_ref[...])
    dts_sc[...] = dts
    a_sc[...] = dts * (-jnp.exp(alog_ref[...]))

    for d in range(2):
        if has_h0:
            for p in range(hpg // 2):
                blk = h0_ref[d, 2 * p:2 * p + 2].reshape(2 * SSD_HEAD_DIM, SSD_STATE)
                ht_sc[d, :, p * LANES:(p + 1) * LANES] = blk.T
        else:
            ht_sc[d] = jnp.zeros((SSD_STATE, gw), F32)

    ri = lax.broadcasted_iota(jnp.int32, (q, q), 0)
    ci = lax.broadcasted_iota(jnp.int32, (q, q), 1)
    lane_lo = lax.broadcasted_iota(jnp.int32, (q, LANES), 1) < SSD_HEAD_DIM
    neg_inf = jnp.float32(-jnp.inf)

    def chunk_dir(c, d):
        rows = pl.ds(pl.multiple_of(c * q, q), q)
        a_c = a_sc[rows, :]
        if d == 0:
            cs = _dot_l01(tri_ref[...], a_c)
            tot = cs[q - 1:q, :]
            mask = ri >= ci
        else:
            cs = _dot_l01(trit_ref[...], a_c)
            tot = cs[0:1, :]
            mask = ri <= ci
        cst = cs.T
        stack = jnp.concatenate([
            dts_sc[rows, :].astype(BF16),
            jnp.exp(cs).astype(BF16),
            jnp.exp(tot - cs).astype(BF16),
            jnp.broadcast_to(jnp.exp(tot), (BF16_ROWS, LANES)).astype(BF16)], axis=0)
        ex = jnp.dot(stack, e2_ref[:, d * gw:(d + 1) * gw], preferred_element_type=F32)
        dtx, ecs_x, dte_x, ea_x = ex[0:q], ex[q:2 * q], ex[2 * q:3 * q], ex[3 * q:3 * q + 1]
        b_c = ba_sc[rows, :]
        c_bf = ca_sc[rows, :].astype(BF16)
        xdt = xa_sc[rows, :] * dtx
        g = lax.dot_general(c_bf, b_c.astype(BF16), (((1,), (1,)), ((), ())),
                            preferred_element_type=F32)
        ht = ht_sc[d]
        y_off = jnp.dot(c_bf, ht.astype(BF16), preferred_element_type=F32) * ecs_x
        pieces = []
        for p in range(hpg // 2):
            ms = []
            for jj in range(2):
                r = d * hpg + 2 * p + jj
                diff = cs[:, r:r + 1] - cst[r:r + 1, :]
                ms.append((g * jnp.exp(jnp.where(mask, diff, neg_inf))).astype(BF16))
            xp = xdt[:, p * LANES:(p + 1) * LANES]
            rhs = jnp.concatenate([jnp.where(lane_lo, xp, 0.0), jnp.where(lane_lo, 0.0, xp)], axis=0)
            pieces.append(jnp.dot(jnp.concatenate(ms, axis=1), rhs.astype(BF16),
                                  preferred_element_type=F32))
        s_t = jnp.dot(b_c.T.astype(BF16), (xdt * dte_x).astype(BF16), preferred_element_type=F32)
        ht_sc[d] = ht * ea_x + s_t
        yg_sc[rows, :] += jnp.concatenate(pieces, axis=1) + y_off

    def body(c, carry):
        chunk_dir(c, 0)
        chunk_dir(nc - 1 - c, 1)
        return carry

    lax.fori_loop(0, nc, body, 0, unroll=min(nc, SSD_UNROLL))

    yz = yg_sc[...] * _silu(z_ref[...])
    y_sc[gi] = yz
    part = jnp.broadcast_to(jnp.sum(yz * yz, axis=-1, keepdims=True), ssq_sc.shape)

    @pl.when(gi == 0)
    def _():
        ssq_sc[...] = part

    @pl.when(gi > 0)
    def _():
        ssq_sc[...] += part

    @pl.when(gi == SSD_GROUPS - 1)
    def _():
        inv = lax.rsqrt(ssq_sc[:, 0:1] * (1.0 / SSD_INNER) + NORM_EPS)
        for gg in range(SSD_GROUPS):
            cols = slice(gg * gw, (gg + 1) * gw)
            yn_ref[:, cols] = (y_sc[gg] * inv * gn_ref[:, cols]).astype(BF16)

    if emit_state:
        for d in range(2):
            for p in range(hpg // 2):
                blk = ht_sc[d, :, p * LANES:(p + 1) * LANES].T
                hfin_ref[d, 2 * p:2 * p + 2] = blk.reshape(2, SSD_HEAD_DIM, SSD_STATE)


def _ssd_constants():
    hpg = HEADS_PER_GROUP
    sel = np.zeros((SSD_GROUPS, LANES, LANES), np.float32)
    for g in range(SSD_GROUPS):
        for d in range(2):
            for j in range(hpg):
                sel[g, d * SSD_HEADS + hpg * g + j, d * hpg + j] = 1.0
    e2 = np.zeros((LANES, 2 * GROUP_WIDTH), np.float32)
    for d in range(2):
        for j in range(hpg):
            lo = d * GROUP_WIDTH + j * SSD_HEAD_DIM
            e2[d * hpg + j, lo:lo + SSD_HEAD_DIM] = 1.0
    tri = np.tril(np.ones((SSD_CHUNK, SSD_CHUNK), np.float32))
    return (jnp.asarray(sel, BF16), jnp.asarray(e2, BF16), jnp.asarray(tri, BF16),
            jnp.asarray(tri.T, BF16))


def _ssd(head, dt, conv_w, conv_b3, alog_g, dsk_g, norm_g3, consts, l, *, row0, n_seq, seq_len,
         h0=None, prev_y=None, prev_state=None, state_shape=None):
    n_tok = head.shape[0]
    sel, e2, tri, trit = consts
    assert row0 % seq_len == 0 and seq_len % (2 * SSD_CHUNK) == 0
    blk0 = row0 // seq_len
    gw = GROUP_WIDTH
    has_h0 = h0 is not None
    emit_state = state_shape is not None
    xcol = SSD_INNER // gw
    bcol = (SSD_INNER + SSD_INNER) // LANES
    ccol = bcol + SSD_GROUPS
    cwb = SSD_INNER // LANES
    cwc = cwb + SSD_GROUPS
    in_specs = [
        pl.BlockSpec((seq_len, gw), lambda b, g: (blk0 + b, g)),
        pl.BlockSpec((seq_len, gw), lambda b, g: (blk0 + b, xcol + g)),
        pl.BlockSpec((seq_len, LANES), lambda b, g: (blk0 + b, bcol + g)),
        pl.BlockSpec((seq_len, LANES), lambda b, g: (blk0 + b, ccol + g)),
        pl.BlockSpec((seq_len, LANES), lambda b, g: (blk0 + b, 0)),
        pl.BlockSpec((None, SSD_CONV, gw), lambda b, g: (l, 0, g)),
        pl.BlockSpec((None, SSD_CONV, LANES), lambda b, g: (l, 0, cwb + g)),
        pl.BlockSpec((None, SSD_CONV, LANES), lambda b, g: (l, 0, cwc + g)),
        pl.BlockSpec((None, 1, gw), lambda b, g: (l, 0, g)),
        pl.BlockSpec((None, 1, LANES), lambda b, g: (l, 0, cwb + g)),
        pl.BlockSpec((None, 1, LANES), lambda b, g: (l, 0, cwc + g)),
        pl.BlockSpec((None, None, 1, LANES), lambda b, g: (l, g, 0, 0)),
        pl.BlockSpec((None, None, 1, gw), lambda b, g: (l, g, 0, 0)),
        pl.BlockSpec((None, 1, SSD_INNER), lambda b, g: (l, 0, 0)),
        pl.BlockSpec((None, LANES, LANES), lambda b, g: (g, 0, 0)),
        pl.BlockSpec((LANES, 2 * gw), lambda b, g: (0, 0)),
        pl.BlockSpec((SSD_CHUNK, SSD_CHUNK), lambda b, g: (0, 0)),
        pl.BlockSpec((SSD_CHUNK, SSD_CHUNK), lambda b, g: (0, 0)),
    ]
    args = [head, head, head, head, dt, conv_w, conv_w, conv_w, conv_b3, conv_b3, conv_b3,
            alog_g, dsk_g, norm_g3, sel, e2, tri, trit]
    if has_h0:
        in_specs.append(pl.BlockSpec((None, None, 2, HEADS_PER_GROUP, SSD_HEAD_DIM, SSD_STATE),
                                     lambda b, g: (b, l, 0, g, 0, 0)))
        args.append(h0)
    aliases = {}
    if prev_y is not None:
        in_specs.append(pl.BlockSpec(memory_space=pl.ANY))
        aliases[len(args)] = 0
        args.append(prev_y)
    if prev_state is not None:
        assert emit_state
        in_specs.append(pl.BlockSpec(memory_space=pl.ANY))
        aliases[len(args)] = 1
        args.append(prev_state)
    out_specs = [pl.BlockSpec((seq_len, SSD_INNER), lambda b, g: (blk0 + b, 0))]
    out_shape = [jax.ShapeDtypeStruct((n_tok, SSD_INNER), BF16)]
    if emit_state:
        out_specs.append(pl.BlockSpec((None, None, 2, HEADS_PER_GROUP, SSD_HEAD_DIM, SSD_STATE),
                                      lambda b, g: (b, l, 0, g, 0, 0)))
        out_shape.append(jax.ShapeDtypeStruct(state_shape, F32))
    kern = functools.partial(_ssd_kernel, seq_len=seq_len, has_h0=has_h0, n_alias=len(aliases),
                             emit_state=emit_state)
    return pl.pallas_call(
        kern,
        grid=(n_seq, SSD_GROUPS),
        in_specs=in_specs,
        out_specs=out_specs,
        out_shape=out_shape,
        input_output_aliases=aliases,
        scratch_shapes=[
            pltpu.VMEM((seq_len, gw), F32),
            pltpu.VMEM((seq_len, LANES), F32),
            pltpu.VMEM((seq_len, LANES), F32),
            pltpu.VMEM((seq_len, LANES), F32),
            pltpu.VMEM((seq_len, LANES), F32),
            pltpu.VMEM((2, SSD_STATE, gw), F32),
            pltpu.VMEM((seq_len, gw), F32),
            pltpu.VMEM((SSD_GROUPS, seq_len, gw), F32),
            pltpu.VMEM((seq_len, LANES), F32),
        ],
        compiler_params=_cparams(2),
        name="ssd_lat" if has_h0 else "ssd_ctx",
    )(*args)


def _scft_kernel(*refs, n_alias, n_grp):
    b_ref, c_ref, x_ref, f_ref, cw_ref, cl_ref, sl_ref, cc_ref, sc_ref = refs[:9]
    pos = 9 + n_alias
    ysc_ref, yft_ref = refs[pos], refs[pos + 1]
    v = c_ref[...] * x_ref[...]
    ysc_ref[...] = (b_ref[...] * _dwconv_rows(v, cw_ref[...], (SC_CONV - 1) // 2)).astype(BF16)
    tw = FT_GROUP_DIM
    for gg in range(n_grp):
        cols = slice(gg * tw, (gg + 1) * tw)
        u = f_ref[:, cols].astype(BF16)
        p = jnp.dot(u, cc_ref[...], preferred_element_type=F32).astype(BF16)
        s = jnp.dot(u, sc_ref[...], preferred_element_type=F32).astype(BF16)
        yft_ref[:, cols] = (jnp.dot(cl_ref[...], p, preferred_element_type=F32)
                            - jnp.dot(sl_ref[...], s, preferred_element_type=F32)).astype(BF16)


def _dft_mats(n):
    k = np.arange(n)
    ang = 2.0 * np.pi * ((k[:, None] * k[None, :]) % n) / n
    scale = 1.0 / math.sqrt(n)
    return jnp.asarray(np.cos(ang) * scale, BF16), jnp.asarray(np.sin(ang) * scale, BF16)


def _scft(tail, sc_conv_w, l, *, row0, n_seq, seq_len, n_grp, prev=None):
    n_tok = tail.shape[0]
    tw = n_grp * FT_GROUP_DIM
    assert row0 % seq_len == 0 and SC_WIDTH % tw == 0 and HEAD_COLS % tw == 0
    blk0 = row0 // seq_len
    nt = SC_WIDTH // tw
    c0 = HEAD_COLS // tw
    cl, sl = _dft_mats(seq_len)
    cc, sc = _dft_mats(FT_GROUP_DIM)
    in_specs = [
        pl.BlockSpec((seq_len, tw), lambda b, j: (blk0 + b, c0 + j)),
        pl.BlockSpec((seq_len, tw), lambda b, j: (blk0 + b, c0 + nt + j)),
        pl.BlockSpec((seq_len, tw), lambda b, j: (blk0 + b, c0 + 2 * nt + j)),
        pl.BlockSpec((seq_len, tw), lambda b, j: (blk0 + b, c0 + 3 * nt + j)),
        pl.BlockSpec((None, SC_CONV, tw), lambda b, j: (l, 0, j)),
        pl.BlockSpec((seq_len, seq_len), lambda b, j: (0, 0)),
        pl.BlockSpec((seq_len, seq_len), lambda b, j: (0, 0)),
        pl.BlockSpec((FT_GROUP_DIM, FT_GROUP_DIM), lambda b, j: (0, 0)),
        pl.BlockSpec((FT_GROUP_DIM, FT_GROUP_DIM), lambda b, j: (0, 0)),
    ]
    args = [tail, tail, tail, tail, sc_conv_w, cl, sl, cc, sc]
    aliases = {}
    if prev is not None:
        in_specs += [pl.BlockSpec(memory_space=pl.ANY), pl.BlockSpec(memory_space=pl.ANY)]
        aliases = {len(args): 0, len(args) + 1: 1}
        args += list(prev)
    kern = functools.partial(_scft_kernel, n_alias=len(aliases), n_grp=n_grp)
    return pl.pallas_call(
        kern,
        grid=(n_seq, nt),
        in_specs=in_specs,
        out_specs=[
            pl.BlockSpec((seq_len, tw), lambda b, j: (blk0 + b, j)),
            pl.BlockSpec((seq_len, tw), lambda b, j: (blk0 + b, j)),
        ],
        out_shape=[
            jax.ShapeDtypeStruct((n_tok, SC_WIDTH), BF16),
            jax.ShapeDtypeStruct((n_tok, FT_WIDTH), BF16),
        ],
        input_output_aliases=aliases,
        compiler_params=_cparams(2),
        name="scft_lat" if prev is not None else "scft_ctx",
    )(*args)


def _mixout_kernel(a0_ref, a1_ref, a2_ref, g0_ref, g1_ref, g2_ref, w0_ref, w1_ref, w2_ref, wo_ref,
                   x_ref, mod_ref, ng_ref, o_ref, un_ref, acc_sc, m_sc, *, nk):
    k = pl.program_id(1)
    te = o_ref.shape[0]

    def merge():
        m = jax.nn.sigmoid(g0_ref[...]) * jnp.dot(a0_ref[...], w0_ref[...].astype(BF16),
                                                   preferred_element_type=F32)
        m += jax.nn.sigmoid(g1_ref[...]) * jnp.dot(a1_ref[...], w1_ref[...].astype(BF16),
                                                    preferred_element_type=F32)
        m += jax.nn.sigmoid(g2_ref[...]) * jnp.dot(a2_ref[...], w2_ref[...].astype(BF16),
                                                    preferred_element_type=F32)
        m_sc[k % 2] = m.astype(BF16)

    def project():
        return jnp.dot(m_sc[(k + 1) % 2], wo_ref[...].astype(BF16), preferred_element_type=F32)

    @pl.when(k == 0)
    def _():
        merge()

    @pl.when(k == 1)
    def _():
        acc_sc[...] = project()
        merge()

    @pl.when(jnp.logical_and(k > 1, k < nk))
    def _():
        acc_sc[...] += project()
        merge()

    @pl.when(k == nk)
    def _():
        acc_sc[...] += project()

    @pl.when(k >= nk)
    def _():
        r0 = pl.multiple_of((k - nk) * te, te)
        y = _rms(acc_sc[pl.ds(r0, te), :], ng_ref[3:4, :])
        x_new = x_ref[...] + mod_ref[5:6, :] * y
        o_ref[...] = x_new
        un_ref[...] = _norm_mod(x_new, ng_ref[4:5, :], mod_ref[6:7, :], mod_ref[7:8, :]).astype(BF16)


def _mixout(yssd, ysc, yft, tail, w_br_ssd, w_br_sc, w_br_ft, w_out, x, mod4, norm_g, l, rows,
            tm=1024, tk=256):
    n_tok, d = x.shape
    nk = d // tk
    gate0 = (HEAD_COLS + 3 * SC_WIDTH + FT_WIDTH) // tk
    per = d // tk
    grp = lambda i: rows.group_of_tile(i, tm)
    kk = lambda k: jnp.minimum(k, nk - 1)
    te = EPILOGUE_ROWS
    ne = tm // te
    erow = lambda i, k: (ne * i + jnp.clip(k - nk, 0, ne - 1), 0)
    kern = functools.partial(_mixout_kernel, nk=nk)
    return pl.pallas_call(
        kern,
        grid=(n_tok // tm, nk + ne),
        in_specs=[
            pl.BlockSpec((tm, SSD_INNER), lambda i, k: (i, 0)),
            pl.BlockSpec((tm, SC_WIDTH), lambda i, k: (i, 0)),
            pl.BlockSpec((tm, FT_WIDTH), lambda i, k: (i, 0)),
            pl.BlockSpec((tm, tk), lambda i, k: (i, gate0 + kk(k))),
            pl.BlockSpec((tm, tk), lambda i, k: (i, gate0 + per + kk(k))),
            pl.BlockSpec((tm, tk), lambda i, k: (i, gate0 + 2 * per + kk(k))),
            pl.BlockSpec((None, SSD_INNER, tk), lambda i, k: (l, 0, kk(k))),
            pl.BlockSpec((None, SC_WIDTH, tk), lambda i, k: (l, 0, kk(k))),
            pl.BlockSpec((None, FT_WIDTH, tk), lambda i, k: (l, 0, kk(k))),
            pl.BlockSpec((None, tk, d), lambda i, k: (l, jnp.clip(k - 1, 0, nk - 1), 0)),
            pl.BlockSpec((te, d), erow),
            pl.BlockSpec((None, None, N_MOD, d), lambda i, k: (l, grp(i), 0, 0)),
            pl.BlockSpec((None, 6, d), lambda i, k: (l, 0, 0)),
        ],
        out_specs=[pl.BlockSpec((te, d), erow), pl.BlockSpec((te, d), erow)],
        out_shape=[jax.ShapeDtypeStruct((n_tok, d), F32), jax.ShapeDtypeStruct((n_tok, d), BF16)],
        scratch_shapes=[pltpu.VMEM((tm, d), F32), pltpu.VMEM((2, tm, tk), BF16)],
        compiler_params=_cparams(2),
        name="mixout",
    )(yssd, ysc, yft, tail, tail, tail, w_br_ssd, w_br_sc, w_br_ft, w_out, x, mod4, norm_g)


def _grid_pos_emb(n_tok):
    rows = n_tok // GRID_W
    t = np.arange(rows * GRID_W)
    r = (t // GRID_W).astype(np.float32)[:, None]
    col = (t % GRID_W).astype(np.float32)[:, None]
    nf = D_MODEL // 4
    omega = (1.0 / (np.float32(POS_BASE) ** (np.arange(nf, dtype=np.float32) / np.float32(nf)))).astype(np.float32)
    ro = (r * omega).astype(np.float32).astype(np.float64)
    co = (col * omega).astype(np.float32).astype(np.float64)
    return np.concatenate([np.sin(ro), np.cos(ro), np.sin(co), np.cos(co)], axis=-1).astype(np.float32)


def _group_lanes(p, hpg=HEADS_PER_GROUP):
    depth = p.shape[0]
    t = p.reshape(depth, 2, SSD_GROUPS, hpg).transpose(0, 2, 1, 3).reshape(depth, SSD_GROUPS, 1, 2 * hpg)
    return jnp.pad(t, ((0, 0), (0, 0), (0, 0), (0, LANES - 2 * hpg)))


def kernel(x_prompt, x_sample, state_ssd, c, c_ctx, ada_w, ada_b, norm_g, ffn1_wgu, ffn1_wd, w_in,
           ssd_conv_w, ssd_conv_b, ssd_dt_bias, ssd_a_log, ssd_d, ssd_norm_g, sc_conv_w,
           w_br_ssd, w_br_sc, w_br_ft, w_out, ffn2_wgu, ffn2_wd):
    n_ctx, ctx_len, d = x_prompt.shape
    n_lat, lat_len, _ = x_sample.shape
    depth = ada_w.shape[0]
    rows = _Rows(n_ctx * ctx_len, n_lat, lat_len)

    c8 = jnp.concatenate([c_ctx[None, :], c, jnp.zeros((8 - 1 - n_lat, d), F32)], axis=0)
    mod4 = _modulation(c8, ada_w, ada_b)[:, :1 + n_lat].reshape(depth, 1 + n_lat, N_MOD, d)

    x, u = _embed(x_prompt.reshape(n_ctx * ctx_len, d), x_sample.reshape(n_lat * lat_len, d),
                  jnp.asarray(_grid_pos_emb(lat_len)), mod4, norm_g, rows)

    consts = _ssd_constants()
    w_in_t = jnp.swapaxes(w_in, 1, 2)
    conv_b3 = ssd_conv_b.reshape(depth, 1, SSD_XBC)
    dtb = jnp.pad(ssd_dt_bias.reshape(depth, 1, DT_WIDTH), ((0, 0), (0, 0), (0, LANES - DT_WIDTH)))
    alog_g = _group_lanes(ssd_a_log)
    dsk_g = jnp.repeat(ssd_d, SSD_HEAD_DIM, axis=-1).reshape(depth, SSD_GROUPS, 1, GROUP_WIDTH)
    ssd_norm_g3 = ssd_norm_g.reshape(depth, 1, SSD_INNER)
    state_shape = (n_ctx, depth, 2, SSD_HEADS, SSD_HEAD_DIM, SSD_STATE)

    states = None
    for l in range(depth):
        x, u = _ffn(x, u, mod4, norm_g, ffn1_wgu, ffn1_wd, l, 0, rows, "same")
        proj, dt = _inproj(u, w_in_t, dtb, l)
        tail = proj
        ssd_args = (proj, dt, ssd_conv_w, conv_b3, alog_g, dsk_g, ssd_norm_g3, consts, l)
        yssd, states = _ssd(*ssd_args, row0=0, n_seq=n_ctx, seq_len=ctx_len,
                            prev_state=states, state_shape=state_shape)
        (yssd,) = _ssd(*ssd_args, row0=rows.n_ctx_tok, n_seq=n_lat, seq_len=lat_len,
                       h0=state_ssd, prev_y=yssd)
        ysc, yft = _scft(tail, sc_conv_w, l, row0=0, n_seq=n_ctx, seq_len=ctx_len, n_grp=FT_GROUPS)
        ysc, yft = _scft(tail, sc_conv_w, l, row0=rows.n_ctx_tok, n_seq=n_lat, seq_len=lat_len,
                         n_grp=1, prev=(ysc, yft))
        x, u = _mixout(yssd, ysc, yft, tail, w_br_ssd, w_br_sc, w_br_ft, w_out, x, mod4, norm_g, l, rows)
        if l + 1 < depth:
            x, u = _ffn(x, u, mod4, norm_g, ffn2_wgu, ffn2_wd, l, 1, rows, "layer")
        else:
            y_ctx, y_lat = _ffn(x, u, mod4, norm_g, ffn2_wgu, ffn2_wd, l, 1, rows, "final")

    return (y_ctx.reshape(n_ctx, ctx_len, d), y_lat.reshape(n_lat, lat_len, d), states)
```

```python
import functools
import math

import numpy as np
import jax
import jax.numpy as jnp
from jax import lax
from jax.experimental import pallas as pl
from jax.experimental.pallas import tpu as pltpu

F32 = jnp.float32
BF16 = jnp.bfloat16

D_MODEL = 2048
DEPTH = 4
GRID_W = 64
POS_BASE = 10000.0
NORM_EPS = 1e-6
SSD_HEADS = 32
SSD_HEAD_DIM = 64
SSD_INNER = SSD_HEADS * SSD_HEAD_DIM
SSD_GROUPS = 4
SSD_STATE = 128
SSD_CONV = 4
SSD_CHUNK = 128
SSD_XBC = SSD_INNER + 2 * SSD_GROUPS * SSD_STATE
SC_WIDTH = 1024
SC_CONV = 3
FT_WIDTH = 1024
FT_GROUPS = 4
FT_GROUP_DIM = FT_WIDTH // FT_GROUPS
D_FF = 5504
N_BRANCH = 3
N_MOD = 9
HEADS_PER_GROUP = SSD_HEADS // SSD_GROUPS
GROUP_WIDTH = HEADS_PER_GROUP * SSD_HEAD_DIM
HEAD_COLS = SSD_INNER + SSD_XBC
DT_COL = HEAD_COLS
DT_WIDTH = 2 * SSD_HEADS
TAIL_COL = HEAD_COLS + DT_WIDTH
TAIL_COLS = 3 * SC_WIDTH + FT_WIDTH + N_BRANCH * D_MODEL
IN_COLS = TAIL_COL + TAIL_COLS

LANES = 128
BF16_ROWS = 16
VMEM_LIMIT_BYTES = 60 * 1024 * 1024


def _cparams(n_axes):
    return pltpu.CompilerParams(dimension_semantics=("arbitrary",) * n_axes,
                                vmem_limit_bytes=VMEM_LIMIT_BYTES)


def _silu(x):
    return x * jax.nn.sigmoid(x)


def _rms(x, g):
    ms = jnp.mean(x * x, axis=-1, keepdims=True)
    return x * lax.rsqrt(ms + NORM_EPS) * g


def _norm_mod(x, g, shift, scale):
    return _rms(x, g) * (1.0 + scale) + shift


def _split_bf16(x, n):
    parts = []
    r = x
    for i in range(n):
        p = r.astype(BF16)
        parts.append(p)
        if i + 1 < n:
            r = r - p.astype(F32)
    return parts


def _dot_r01(x, m01, n=3):
    acc = None
    for p in _split_bf16(x, n):
        t = jnp.dot(p, m01, preferred_element_type=F32)
        acc = t if acc is None else acc + t
    return acc


def _dot_l01(m01, x, n=3):
    acc = None
    for p in _split_bf16(x, n):
        t = jnp.dot(m01, p, preferred_element_type=F32)
        acc = t if acc is None else acc + t
    return acc


def _mod_kernel(c_ref, w_ref, b_ref, o_ref):
    s = _silu(c_ref[...]).astype(BF16)
    o_ref[...] = jnp.dot(s, w_ref[...].astype(BF16), preferred_element_type=F32) + b_ref[...]


def _modulation(c8, ada_w, ada_b, tn=1024):
    depth, d, n = ada_w.shape
    return pl.pallas_call(
        _mod_kernel,
        grid=(depth, n // tn),
        in_specs=[
            pl.BlockSpec((8, d), lambda l, j: (0, 0)),
            pl.BlockSpec((None, d, tn), lambda l, j: (l, 0, j)),
            pl.BlockSpec((None, 1, tn), lambda l, j: (l, 0, j)),
        ],
        out_specs=pl.BlockSpec((None, 8, tn), lambda l, j: (l, 0, j)),
        out_shape=jax.ShapeDtypeStruct((depth, 8, n), F32),
        compiler_params=_cparams(2),
        name="adaln_mod",
    )(c8, ada_w, ada_b.reshape(depth, 1, n))


class _Rows:
    def __init__(self, n_ctx_tok, n_lat, lat_len):
        self.n_ctx_tok = n_ctx_tok
        self.n_lat = n_lat
        self.lat_len = lat_len
        self.n_tok = n_ctx_tok + n_lat * lat_len

    def group_of_tile(self, i, tm):
        assert self.n_ctx_tok % tm == 0 and self.lat_len % tm == 0
        n_ctx_tiles = self.n_ctx_tok // tm
        per_lat = self.lat_len // tm
        return jnp.where(i < n_ctx_tiles, 0, 1 + (i - n_ctx_tiles) // per_lat)


def _single(block_shape, index_map):
    return pl.BlockSpec(block_shape, index_map, pipeline_mode=pl.Buffered(1))


FFN_TF = 256
FFN_SPLIT = 1
EPILOGUE_ROWS = 256


def _ffn_kernel(*refs, nk, mod_row, ng_row, next_kind, n_ctx_tiles):
    x_ref, u_ref, mod_ref, ng_ref, wg_ref, wu_ref, wd_ref = refs[:7]
    pos = 7
    if next_kind == "layer":
        modn_ref, ngn_ref = refs[pos:pos + 2]
        pos += 2
    o_ref = refs[pos]
    pos += 1
    if next_kind == "final":
        olat_ref = refs[pos]
    else:
        un_ref = refs[pos]
    pos += 1
    acc_sc, wup_sc, wdn_sc = refs[pos:pos + 3]
    k = pl.program_id(1)
    tf = FFN_TF
    te = o_ref.shape[0]

    @pl.when(k == 0)
    def _():
        acc_sc[...] = jnp.zeros_like(acc_sc)

    @pl.when(k < nk)
    def _():
        ts = tf // FFN_SPLIT
        for s in range(FFN_SPLIT):
            wup_sc[:, 2 * s * ts:(2 * s + 1) * ts] = wg_ref[:, s * ts:(s + 1) * ts].astype(BF16)
            wup_sc[:, (2 * s + 1) * ts:(2 * s + 2) * ts] = wu_ref[:, s * ts:(s + 1) * ts].astype(BF16)
        wdn_sc[...] = wd_ref[...].astype(BF16)

        overlap = nk * tf - D_FF
        assert overlap <= ts
        for s in range(FFN_SPLIT):
            h = jnp.dot(u_ref[...], wup_sc[:, 2 * s * ts:(2 * s + 2) * ts], preferred_element_type=F32)
            a = _silu(h[:, :ts]) * h[:, ts:]
            if s == 0:
                lane = lax.broadcasted_iota(jnp.int32, a.shape, 1)
                a = jnp.where(jnp.logical_and(k == nk - 1, lane < overlap), 0.0, a)
            acc_sc[...] += jnp.dot(a.astype(BF16), wdn_sc[s * ts:(s + 1) * ts, :], preferred_element_type=F32)

    @pl.when(k >= nk)
    def _():
        r0 = pl.multiple_of((k - nk) * te, te)
        y = _rms(acc_sc[pl.ds(r0, te), :], ng_ref[ng_row + 1:ng_row + 2, :])
        x_new = x_ref[...] + 0.5 * mod_ref[mod_row + 2:mod_row + 3, :] * y
        if next_kind == "final":
            is_ctx = pl.program_id(0) < n_ctx_tiles

            @pl.when(is_ctx)
            def _():
                o_ref[...] = x_new

            @pl.when(jnp.logical_not(is_ctx))
            def _():
                olat_ref[...] = x_new
        else:
            o_ref[...] = x_new
        if next_kind == "same":
            un = _norm_mod(x_new, ng_ref[ng_row + 2:ng_row + 3, :],
                           mod_ref[mod_row + 3:mod_row + 4, :], mod_ref[mod_row + 4:mod_row + 5, :])
            un_ref[...] = un.astype(BF16)
        elif next_kind == "layer":
            un = _norm_mod(x_new, ngn_ref[0:1, :], modn_ref[0:1, :], modn_ref[1:2, :])
            un_ref[...] = un.astype(BF16)


def _ffn(x, u, mod4, norm_g, wgu, wd, l, which, rows, next_kind, tm=2048):
    n_tok, d = x.shape
    tf = FFN_TF
    nk = pl.cdiv(D_FF, tf)
    mod_row = 0 if which == 0 else 6
    ng_row = 0 if which == 0 else 4
    te = EPILOGUE_ROWS
    ne = tm // te
    assert rows.n_ctx_tok % tm == 0 and n_tok % tm == 0
    gran = math.gcd(rows.lat_len, tm)
    grp = lambda i, k: rows.group_of_tile((tm * i + jnp.clip(k - nk, 0, ne - 1) * te) // gran, gran)
    assert tf % LANES == 0 and D_FF % LANES == 0
    col = lambda k, base=0: LANES * (base // LANES + jnp.minimum(k * (tf // LANES), (D_FF - tf) // LANES))
    assert next_kind in ("same", "layer", "final")
    kern = functools.partial(_ffn_kernel, nk=nk, mod_row=mod_row, ng_row=ng_row, next_kind=next_kind,
                             n_ctx_tiles=rows.n_ctx_tok // tm)
    erow = lambda i, k: (ne * i + jnp.clip(k - nk, 0, ne - 1), 0)
    in_specs = [
        pl.BlockSpec((te, d), erow),
        (_single if next_kind == "final" else pl.BlockSpec)((tm, d), lambda i, k: (i, 0)),
        pl.BlockSpec((None, None, N_MOD, d), lambda i, k: (l, grp(i, k), 0, 0)),
        pl.BlockSpec((None, 6, d), lambda i, k: (l, 0, 0)),
        pl.BlockSpec((pl.squeezed, pl.Element(d), pl.Element(tf)), lambda i, k: (l, 0, col(k))),
        pl.BlockSpec((pl.squeezed, pl.Element(d), pl.Element(tf)), lambda i, k: (l, 0, col(k, D_FF))),
        pl.BlockSpec((pl.squeezed, pl.Element(tf), pl.Element(d)), lambda i, k: (l, col(k), 0)),
    ]
    args = [x, u, mod4, norm_g, wgu, wgu, wd]
    if next_kind == "layer":
        in_specs += [pl.BlockSpec((None, None, N_MOD, d), lambda i, k: (l + 1, grp(i, k), 0, 0)),
                     pl.BlockSpec((None, 6, d), lambda i, k: (l + 1, 0, 0))]
        args += [mod4, norm_g]
    if next_kind == "final":
        n_ctx_blocks = rows.n_ctx_tok // te
        eblk = lambda i, k: ne * i + jnp.clip(k - nk, 0, ne - 1)
        out_specs = [pl.BlockSpec((te, d), lambda i, k: (jnp.minimum(eblk(i, k), n_ctx_blocks - 1), 0)),
                     pl.BlockSpec((te, d), lambda i, k: (jnp.maximum(eblk(i, k) - n_ctx_blocks, 0), 0))]
        out_shape = [jax.ShapeDtypeStruct((rows.n_ctx_tok, d), F32),
                     jax.ShapeDtypeStruct((n_tok - rows.n_ctx_tok, d), F32)]
    else:
        out_specs = [pl.BlockSpec((te, d), erow), pl.BlockSpec((te, d), erow)]
        out_shape = [jax.ShapeDtypeStruct((n_tok, d), F32), jax.ShapeDtypeStruct((n_tok, d), BF16)]
    return pl.pallas_call(
        kern,
        grid=(n_tok // tm, nk + ne),
        in_specs=in_specs,
        out_specs=out_specs,
        out_shape=out_shape,
        scratch_shapes=[
            pltpu.VMEM((tm, d), F32),
            pltpu.VMEM((d, 2 * tf), BF16),
            pltpu.VMEM((tf, d), BF16),
        ],
        compiler_params=_cparams(2),
        name="ffn",
    )(*args)


def _embed_kernel(xp_ref, xs_ref, pe_ref, mod_ref, ng_ref, x_ref, u_ref, *, n_ctx_tiles):
    i = pl.program_id(0)

    @pl.when(i < n_ctx_tiles)
    def _():
        x_ref[...] = xp_ref[...]

    @pl.when(i >= n_ctx_tiles)
    def _():
        x_ref[...] = xs_ref[...] + pe_ref[...]

    u_ref[...] = _norm_mod(x_ref[...], ng_ref[0:1, :], mod_ref[0:1, :], mod_ref[1:2, :]).astype(BF16)


def _embed(xp, xs, pe, mod4, norm_g, rows, tm=256):
    d = xp.shape[1]
    assert rows.lat_len % tm == 0 and rows.n_ctx_tok % tm == 0
    n_ctx_tiles = rows.n_ctx_tok // tm
    pe_tiles = rows.lat_len // tm
    grp = lambda i: rows.group_of_tile(i, tm)
    kern = functools.partial(_embed_kernel, n_ctx_tiles=n_ctx_tiles)
    return pl.pallas_call(
        kern,
        grid=(rows.n_tok // tm,),
        in_specs=[
            pl.BlockSpec((tm, d), lambda i: (jnp.minimum(i, n_ctx_tiles - 1), 0)),
            pl.BlockSpec((tm, d), lambda i: (jnp.maximum(i - n_ctx_tiles, 0), 0)),
            pl.BlockSpec((tm, d), lambda i: (jnp.maximum(i - n_ctx_tiles, 0) % pe_tiles, 0)),
            pl.BlockSpec((None, None, N_MOD, d), lambda i: (0, grp(i), 0, 0)),
            pl.BlockSpec((None, 6, d), lambda i: (0, 0, 0)),
        ],
        out_specs=[pl.BlockSpec((tm, d), lambda i: (i, 0)), pl.BlockSpec((tm, d), lambda i: (i, 0))],
        out_shape=[jax.ShapeDtypeStruct((rows.n_tok, d), F32), jax.ShapeDtypeStruct((rows.n_tok, d), BF16)],
        compiler_params=_cparams(1),
        name="embed",
    )(xp, xs, pe, mod4, norm_g)


def _dot_nt(a, b):
    return lax.dot_general(a, b, (((1,), (1,)), ((), ())), preferred_element_type=F32)


def _softplus(x):
    return jnp.maximum(x, 0.0) + jnp.log1p(jnp.exp(-jnp.abs(x)))


def _inproj_kernel(u_ref, wa_ref, wb_ref, dtb_ref, op_ref, odt_ref, w_sc, *, n_head, tm):
    j = pl.program_id(0)
    m = pl.program_id(1)
    tn = w_sc.shape[0]
    off = DT_WIDTH

    @pl.when(jnp.logical_and(m == 0, j < n_head))
    def _():
        w_sc[...] = wa_ref[...].astype(BF16)

    @pl.when(jnp.logical_and(m == 0, j == n_head))
    def _():
        w_sc[:off, :] = wb_ref[...].astype(BF16)

    @pl.when(jnp.logical_and(m == 0, j > n_head))
    def _():
        w_sc[:tn - off, :] = wa_ref[off:, :].astype(BF16)
        w_sc[tn - off:, :] = wb_ref[...].astype(BF16)

    u = u_ref[pl.ds(pl.multiple_of(m * tm, tm), tm), :]

    @pl.when(j != n_head)
    def _():
        op_ref[...] = _dot_nt(u, w_sc[...])

    @pl.when(j == n_head)
    def _():
        odt_ref[...] = _softplus(_dot_nt(u, w_sc[:LANES, :]) + dtb_ref[...])


def _inproj(u, w_in_t, dtb, l, tm=2048, tn=512):
    n_tok, d = u.shape
    off = DT_WIDTH
    assert TAIL_COL % tn == off and HEAD_COLS % tn == 0 and TAIL_COLS % tn == 0 and tn % off == 0
    n_head = HEAD_COLS // tn
    n_tail = TAIL_COLS // tn
    n_m = n_tok // tm
    sub = tn // off
    proj_j = lambda j: jnp.where(j < n_head, j, jnp.where(j == n_head, n_head - 1, j - 1))
    proj_m = lambda j, m: jnp.where(j == n_head, n_m - 1, m)
    dt_m = lambda j, m: jnp.where(j < n_head, 0, jnp.where(j == n_head, m, n_m - 1))
    wide_j = lambda j: jnp.where(j <= n_head, jnp.minimum(j, n_head), j - 1)
    narrow_j = lambda j: jnp.where(j <= n_head, DT_COL // off, j * sub)
    kern = functools.partial(_inproj_kernel, n_head=n_head, tm=tm)
    return pl.pallas_call(
        kern,
        grid=(n_head + 1 + n_tail, n_m),
        in_specs=[
            _single((n_tok, d), lambda j, m: (0, 0)),
            pl.BlockSpec((None, tn, d), lambda j, m: (l, wide_j(j), 0)),
            pl.BlockSpec((None, off, d), lambda j, m: (l, narrow_j(j), 0)),
            pl.BlockSpec((None, 1, LANES), lambda j, m: (l, 0, 0)),
        ],
        out_specs=[
            pl.BlockSpec((tm, tn), lambda j, m: (proj_m(j, m), proj_j(j))),
            pl.BlockSpec((tm, LANES), lambda j, m: (dt_m(j, m), 0)),
        ],
        out_shape=[
            jax.ShapeDtypeStruct((n_tok, HEAD_COLS + TAIL_COLS), F32),
            jax.ShapeDtypeStruct((n_tok, LANES), F32),
        ],
        scratch_shapes=[pltpu.VMEM((tn, d), BF16)],
        compiler_params=_cparams(2),
        name="inproj",
    )(u, w_in_t, w_in_t, dtb)


def _dwconv_rows(x, w, left):
    n_rows = x.shape[0]
    row = lax.broadcasted_iota(jnp.int32, x.shape, 0)
    out = None
    for k in range(w.shape[0]):
        off = k - left
        if off == 0:
            term = x
        else:
            shifted = pltpu.roll(x, (-off) % n_rows, axis=0)
            valid = (row < n_rows - off) if off > 0 else (row >= -off)
            term = jnp.where(valid, shifted, 0.0)
        term = term * w[k:k + 1, :]
        out = term if out is None else out + term
    return out


SSD_UNROLL = 4


def _ssd_kernel(*refs, seq_len, has_h0, n_alias, emit_state):
    (z_ref, xr_ref, br_ref, cr_ref, dt_ref, cwx_ref, cwb_ref, cwc_ref, cbx_ref, cbb_ref, cbc_ref,
     alog_ref, dsk_ref, gn_ref, sel_ref, e2_ref, tri_ref, trit_ref) = refs[:18]
    pos = 18
    h0_ref = None
    if has_h0:
        h0_ref = refs[pos]
        pos += 1
    pos += n_alias
    yn_ref = refs[pos]
    pos += 1
    hfin_ref = None
    if emit_state:
        hfin_ref = refs[pos]
        pos += 1
    xa_sc, ba_sc, ca_sc, dts_sc, a_sc, ht_sc, yg_sc, y_sc, ssq_sc = refs[pos:pos + 9]

    gi = pl.program_id(1)
    q = SSD_CHUNK
    nc = seq_len // q
    left = (SSD_CONV - 1) // 2
    hpg = HEADS_PER_GROUP
    gw = GROUP_WIDTH

    xa = _silu(_dwconv_rows(xr_ref[...], cwx_ref[...], left) + cbx_ref[...])
    xa_sc[...] = xa
    ba_sc[...] = _silu(_dwconv_rows(br_ref[...], cwb_ref[...], left) + cbb_ref[...])
    ca_sc[...] = _silu(_dwconv_rows(cr_ref[...], cwc_ref[...], left) + cbc_ref[...])
    yg_sc[...] = xa * dsk_ref[...]

    dts = _dot_r01(dt_ref[...], sel_ref[...])
    dts_sc[...] = dts
    a_sc[...] = dts * (-jnp.exp(alog_ref[...]))

    for d in range(2):
        if has_h0:
            for p in range(hpg // 2):
                blk = h0_ref[d, 2 * p:2 * p + 2].reshape(2 * SSD_HEAD_DIM, SSD_STATE)
                ht_sc[d, :, p * LANES:(p + 1) * LANES] = blk.T
        else:
            ht_sc[d] = jnp.zeros((SSD_STATE, gw), F32)

    ri = lax.broadcasted_iota(jnp.int32, (q, q), 0)
    ci = lax.broadcasted_iota(jnp.int32, (q, q), 1)
    lane_lo = lax.broadcasted_iota(jnp.int32, (q, LANES), 1) < SSD_HEAD_DIM
    neg_inf = jnp.float32(-jnp.inf)

    def chunk_dir(c, d):
        rows = pl.ds(pl.multiple_of(c * q, q), q)
        a_c = a_sc[rows, :]
        if d == 0:
            cs = _dot_l01(tri_ref[...], a_c)
            tot = cs[q - 1:q, :]
            mask = ri >= ci
        else:
            cs = _dot_l01(trit_ref[...], a_c)
            tot = cs[0:1, :]
            mask = ri <= ci
        cst = cs.T
        stack = jnp.concatenate([
            dts_sc[rows, :].astype(BF16),
            jnp.exp(cs).astype(BF16),
            jnp.exp(tot - cs).astype(BF16),
            jnp.broadcast_to(jnp.exp(tot), (BF16_ROWS, LANES)).astype(BF16)], axis=0)
        ex = jnp.dot(stack, e2_ref[:, d * gw:(d + 1) * gw], preferred_element_type=F32)
        dtx, ecs_x, dte_x, ea_x = ex[0:q], ex[q:2 * q], ex[2 * q:3 * q], ex[3 * q:3 * q + 1]
        b_c = ba_sc[rows, :]
        c_bf = ca_sc[rows, :].astype(BF16)
        xdt = xa_sc[rows, :] * dtx
        g = lax.dot_general(c_bf, b_c.astype(BF16), (((1,), (1,)), ((), ())),
                            preferred_element_type=F32)
        ht = ht_sc[d]
        y_off = jnp.dot(c_bf, ht.astype(BF16), preferred_element_type=F32) * ecs_x
        pieces = []
        for p in range(hpg // 2):
            ms = []
            for jj in range(2):
                r = d * hpg + 2 * p + jj
                diff = cs[:, r:r + 1] - cst[r:r + 1, :]
                ms.append((g * jnp.exp(jnp.where(mask, diff, neg_inf))).astype(BF16))
            xp = xdt[:, p * LANES:(p + 1) * LANES]
            rhs = jnp.concatenate([jnp.where(lane_lo, xp, 0.0), jnp.where(lane_lo, 0.0, xp)], axis=0)
            pieces.append(jnp.dot(jnp.concatenate(ms, axis=1), rhs.astype(BF16),
                                  preferred_element_type=F32))
        s_t = jnp.dot(b_c.T.astype(BF16), (xdt * dte_x).astype(BF16), preferred_element_type=F32)
        ht_sc[d] = ht * ea_x + s_t
        yg_sc[rows, :] += jnp.concatenate(pieces, axis=1) + y_off

    def body(c, carry):
        chunk_dir(c, 0)
        chunk_dir(nc - 1 - c, 1)
        return carry

    lax.fori_loop(0, nc, body, 0, unroll=min(nc, SSD_UNROLL))

    yz = yg_sc[...] * _silu(z_ref[...])
    y_sc[gi] = yz
    part = jnp.broadcast_to(jnp.sum(yz * yz, axis=-1, keepdims=True), ssq_sc.shape)

    @pl.when(gi == 0)
    def _():
        ssq_sc[...] = part

    @pl.when(gi > 0)
    def _():
        ssq_sc[...] += part

    @pl.when(gi == SSD_GROUPS - 1)
    def _():
        inv = lax.rsqrt(ssq_sc[:, 0:1] * (1.0 / SSD_INNER) + NORM_EPS)
        for gg in range(SSD_GROUPS):
            cols = slice(gg * gw, (gg + 1) * gw)
            yn_ref[:, cols] = (y_sc[gg] * inv * gn_ref[:, cols]).astype(BF16)

    if emit_state:
        for d in range(2):
            for p in range(hpg // 2):
                blk = ht_sc[d, :, p * LANES:(p + 1) * LANES].T
                hfin_ref[d, 2 * p:2 * p + 2] = blk.reshape(2, SSD_HEAD_DIM, SSD_STATE)


def _ssd_constants():
    hpg = HEADS_PER_GROUP
    sel = np.zeros((SSD_GROUPS, LANES, LANES), np.float32)
    for g in range(SSD_GROUPS):
        for d in range(2):
            for j in range(hpg):
                sel[g, d * SSD_HEADS + hpg * g + j, d * hpg + j] = 1.0
    e2 = np.zeros((LANES, 2 * GROUP_WIDTH), np.float32)
    for d in range(2):
        for j in range(hpg):
            lo = d * GROUP_WIDTH + j * SSD_HEAD_DIM
            e2[d * hpg + j, lo:lo + SSD_HEAD_DIM] = 1.0
    tri = np.tril(np.ones((SSD_CHUNK, SSD_CHUNK), np.float32))
    return (jnp.asarray(sel, BF16), jnp.asarray(e2, BF16), jnp.asarray(tri, BF16),
            jnp.asarray(tri.T, BF16))


def _ssd(head, dt, conv_w, conv_b3, alog_g, dsk_g, norm_g3, consts, l, *, row0, n_seq, seq_len,
         h0=None, prev_y=None, prev_state=None, state_shape=None):
    n_tok = head.shape[0]
    sel, e2, tri, trit = consts
    assert row0 % seq_len == 0 and seq_len % (2 * SSD_CHUNK) == 0
    blk0 = row0 // seq_len
    gw = GROUP_WIDTH
    has_h0 = h0 is not None
    emit_state = state_shape is not None
    xcol = SSD_INNER // gw
    bcol = (SSD_INNER + SSD_INNER) // LANES
    ccol = bcol + SSD_GROUPS
    cwb = SSD_INNER // LANES
    cwc = cwb + SSD_GROUPS
    in_specs = [
        pl.BlockSpec((seq_len, gw), lambda b, g: (blk0 + b, g)),
        pl.BlockSpec((seq_len, gw), lambda b, g: (blk0 + b, xcol + g)),
        pl.BlockSpec((seq_len, LANES), lambda b, g: (blk0 + b, bcol + g)),
        pl.BlockSpec((seq_len, LANES), lambda b, g: (blk0 + b, ccol + g)),
        pl.BlockSpec((seq_len, LANES), lambda b, g: (blk0 + b, 0)),
        pl.BlockSpec((None, SSD_CONV, gw), lambda b, g: (l, 0, g)),
        pl.BlockSpec((None, SSD_CONV, LANES), lambda b, g: (l, 0, cwb + g)),
        pl.BlockSpec((None, SSD_CONV, LANES), lambda b, g: (l, 0, cwc + g)),
        pl.BlockSpec((None, 1, gw), lambda b, g: (l, 0, g)),
        pl.BlockSpec((None, 1, LANES), lambda b, g: (l, 0, cwb + g)),
        pl.BlockSpec((None, 1, LANES), lambda b, g: (l, 0, cwc + g)),
        pl.BlockSpec((None, None, 1, LANES), lambda b, g: (l, g, 0, 0)),
        pl.BlockSpec((None, None, 1, gw), lambda b, g: (l, g, 0, 0)),
        pl.BlockSpec((None, 1, SSD_INNER), lambda b, g: (l, 0, 0)),
        pl.BlockSpec((None, LANES, LANES), lambda b, g: (g, 0, 0)),
        pl.BlockSpec((LANES, 2 * gw), lambda b, g: (0, 0)),
        pl.BlockSpec((SSD_CHUNK, SSD_CHUNK), lambda b, g: (0, 0)),
        pl.BlockSpec((SSD_CHUNK, SSD_CHUNK), lambda b, g: (0, 0)),
    ]
    args = [head, head, head, head, dt, conv_w, conv_w, conv_w, conv_b3, conv_b3, conv_b3,
            alog_g, dsk_g, norm_g3, sel, e2, tri, trit]
    if has_h0:
        in_specs.append(pl.BlockSpec((None, None, 2, HEADS_PER_GROUP, SSD_HEAD_DIM, SSD_STATE),
                                     lambda b, g: (b, l, 0, g, 0, 0)))
        args.append(h0)
    aliases = {}
    if prev_y is not None:
        in_specs.append(pl.BlockSpec(memory_space=pl.ANY))
        aliases[len(args)] = 0
        args.append(prev_y)
    if prev_state is not None:
        assert emit_state
        in_specs.append(pl.BlockSpec(memory_space=pl.ANY))
        aliases[len(args)] = 1
        args.append(prev_state)
    out_specs = [pl.BlockSpec((seq_len, SSD_INNER), lambda b, g: (blk0 + b, 0))]
    out_shape = [jax.ShapeDtypeStruct((n_tok, SSD_INNER), BF16)]
    if emit_state:
        out_specs.append(pl.BlockSpec((None, None, 2, HEADS_PER_GROUP, SSD_HEAD_DIM, SSD_STATE),
                                      lambda b, g: (b, l, 0, g, 0, 0)))
        out_shape.append(jax.ShapeDtypeStruct(state_shape, F32))
    kern = functools.partial(_ssd_kernel, seq_len=seq_len, has_h0=has_h0, n_alias=len(aliases),
                             emit_state=emit_state)
    return pl.pallas_call(
        kern,
        grid=(n_seq, SSD_GROUPS),
        in_specs=in_specs,
        out_specs=out_specs,
        out_shape=out_shape,
        input_output_aliases=aliases,
        scratch_shapes=[
            pltpu.VMEM((seq_len, gw), F32),
            pltpu.VMEM((seq_len, LANES), F32),
            pltpu.VMEM((seq_len, LANES), F32),
            pltpu.VMEM((seq_len, LANES), F32),
            pltpu.VMEM((seq_len, LANES), F32),
            pltpu.VMEM((2, SSD_STATE, gw), F32),
            pltpu.VMEM((seq_len, gw), F32),
            pltpu.VMEM((SSD_GROUPS, seq_len, gw), F32),
            pltpu.VMEM((seq_len, LANES), F32),
        ],
        compiler_params=_cparams(2),
        name="ssd_lat" if has_h0 else "ssd_ctx",
    )(*args)


def _scft_kernel(*refs, n_alias, n_grp):
    b_ref, c_ref, x_ref, f_ref, cw_ref, cl_ref, sl_ref, cc_ref, sc_ref = refs[:9]
    pos = 9 + n_alias
    ysc_ref, yft_ref = refs[pos], refs[pos + 1]
    v = c_ref[...] * x_ref[...]
    ysc_ref[...] = (b_ref[...] * _dwconv_rows(v, cw_ref[...], (SC_CONV - 1) // 2)).astype(BF16)
    tw = FT_GROUP_DIM
    for gg in range(n_grp):
        cols = slice(gg * tw, (gg + 1) * tw)
        u = f_ref[:, cols].astype(BF16)
        p = jnp.dot(u, cc_ref[...], preferred_element_type=F32).astype(BF16)
        s = jnp.dot(u, sc_ref[...], preferred_element_type=F32).astype(BF16)
        yft_ref[:, cols] = (jnp.dot(cl_ref[...], p, preferred_element_type=F32)
                            - jnp.dot(sl_ref[...], s, preferred_element_type=F32)).astype(BF16)


def _dft_mats(n):
    k = np.arange(n)
    ang = 2.0 * np.pi * ((k[:, None] * k[None, :]) % n) / n
    scale = 1.0 / math.sqrt(n)
    return jnp.asarray(np.cos(ang) * scale, BF16), jnp.asarray(np.sin(ang) * scale, BF16)


def _scft(tail, sc_conv_w, l, *, row0, n_seq, seq_len, n_grp, prev=None):
    n_tok = tail.shape[0]
    tw = n_grp * FT_GROUP_DIM
    assert row0 % seq_len == 0 and SC_WIDTH % tw == 0 and HEAD_COLS % tw == 0
    blk0 = row0 // seq_len
    nt = SC_WIDTH // tw
    c0 = HEAD_COLS // tw
    cl, sl = _dft_mats(seq_len)
    cc, sc = _dft_mats(FT_GROUP_DIM)
    in_specs = [
        pl.BlockSpec((seq_len, tw), lambda b, j: (blk0 + b, c0 + j)),
        pl.BlockSpec((seq_len, tw), lambda b, j: (blk0 + b, c0 + nt + j)),
        pl.BlockSpec((seq_len, tw), lambda b, j: (blk0 + b, c0 + 2 * nt + j)),
        pl.BlockSpec((seq_len, tw), lambda b, j: (blk0 + b, c0 + 3 * nt + j)),
        pl.BlockSpec((None, SC_CONV, tw), lambda b, j: (l, 0, j)),
        pl.BlockSpec((seq_len, seq_len), lambda b, j: (0, 0)),
        pl.BlockSpec((seq_len, seq_len), lambda b, j: (0, 0)),
        pl.BlockSpec((FT_GROUP_DIM, FT_GROUP_DIM), lambda b, j: (0, 0)),
        pl.BlockSpec((FT_GROUP_DIM, FT_GROUP_DIM), lambda b, j: (0, 0)),
    ]
    args = [tail, tail, tail, tail, sc_conv_w, cl, sl, cc, sc]
    aliases = {}
    if prev is not None:
        in_specs += [pl.BlockSpec(memory_space=pl.ANY), pl.BlockSpec(memory_space=pl.ANY)]
        aliases = {len(args): 0, len(args) + 1: 1}
        args += list(prev)
    kern = functools.partial(_scft_kernel, n_alias=len(aliases), n_grp=n_grp)
    return pl.pallas_call(
        kern,
        grid=(n_seq, nt),
        in_specs=in_specs,
        out_specs=[
            pl.BlockSpec((seq_len, tw), lambda b, j: (blk0 + b, j)),
            pl.BlockSpec((seq_len, tw), lambda b, j: (blk0 + b, j)),
        ],
        out_shape=[
            jax.ShapeDtypeStruct((n_tok, SC_WIDTH), BF16),
            jax.ShapeDtypeStruct((n_tok, FT_WIDTH), BF16),
        ],
        input_output_aliases=aliases,
        compiler_params=_cparams(2),
        name="scft_lat" if prev is not None else "scft_ctx",
    )(*args)


def _mixout_kernel(a0_ref, a1_ref, a2_ref, g0_ref, g1_ref, g2_ref, w0_ref, w1_ref, w2_ref, wo_ref,
                   x_ref, mod_ref, ng_ref, o_ref, un_ref, acc_sc, m_sc, *, nk):
    k = pl.program_id(1)
    te = o_ref.shape[0]

    def merge():
        m = jax.nn.sigmoid(g0_ref[...]) * jnp.dot(a0_ref[...], w0_ref[...].astype(BF16),
                                                   preferred_element_type=F32)
        m += jax.nn.sigmoid(g1_ref[...]) * jnp.dot(a1_ref[...], w1_ref[...].astype(BF16),
                                                    preferred_element_type=F32)
        m += jax.nn.sigmoid(g2_ref[...]) * jnp.dot(a2_ref[...], w2_ref[...].astype(BF16),
                                                    preferred_element_type=F32)
        m_sc[k % 2] = m.astype(BF16)

    def project():
        return jnp.dot(m_sc[(k + 1) % 2], wo_ref[...].astype(BF16), preferred_element_type=F32)

    @pl.when(k == 0)
    def _():
        merge()

    @pl.when(k == 1)
    def _():
        acc_sc[...] = project()
        merge()

    @pl.when(jnp.logical_and(k > 1, k < nk))
    def _():
        acc_sc[...] += project()
        merge()

    @pl.when(k == nk)
    def _():
        acc_sc[...] += project()

    @pl.when(k >= nk)
    def _():
        r0 = pl.multiple_of((k - nk) * te, te)
        y = _rms(acc_sc[pl.ds(r0, te), :], ng_ref[3:4, :])
        x_new = x_ref[...] + mod_ref[5:6, :] * y
        o_ref[...] = x_new
        un_ref[...] = _norm_mod(x_new, ng_ref[4:5, :], mod_ref[6:7, :], mod_ref[7:8, :]).astype(BF16)


def _mixout(yssd, ysc, yft, tail, w_br_ssd, w_br_sc, w_br_ft, w_out, x, mod4, norm_g, l, rows,
            tm=1024, tk=256):
    n_tok, d = x.shape
    nk = d // tk
    gate0 = (HEAD_COLS + 3 * SC_WIDTH + FT_WIDTH) // tk
    per = d // tk
    grp = lambda i: rows.group_of_tile(i, tm)
    kk = lambda k: jnp.minimum(k, nk - 1)
    te = EPILOGUE_ROWS
    ne = tm // te
    erow = lambda i, k: (ne * i + jnp.clip(k - nk, 0, ne - 1), 0)
    kern = functools.partial(_mixout_kernel, nk=nk)
    return pl.pallas_call(
        kern,
        grid=(n_tok // tm, nk + ne),
        in_specs=[
            pl.BlockSpec((tm, SSD_INNER), lambda i, k: (i, 0)),
            pl.BlockSpec((tm, SC_WIDTH), lambda i, k: (i, 0)),
            pl.BlockSpec((tm, FT_WIDTH), lambda i, k: (i, 0)),
            pl.BlockSpec((tm, tk), lambda i, k: (i, gate0 + kk(k))),
            pl.BlockSpec((tm, tk), lambda i, k: (i, gate0 + per + kk(k))),
            pl.BlockSpec((tm, tk), lambda i, k: (i, gate0 + 2 * per + kk(k))),
            pl.BlockSpec((None, SSD_INNER, tk), lambda i, k: (l, 0, kk(k))),
            pl.BlockSpec((None, SC_WIDTH, tk), lambda i, k: (l, 0, kk(k))),
            pl.BlockSpec((None, FT_WIDTH, tk), lambda i, k: (l, 0, kk(k))),
            pl.BlockSpec((None, tk, d), lambda i, k: (l, jnp.clip(k - 1, 0, nk - 1), 0)),
            pl.BlockSpec((te, d), erow),
            pl.BlockSpec((None, None, N_MOD, d), lambda i, k: (l, grp(i), 0, 0)),
            pl.BlockSpec((None, 6, d), lambda i, k: (l, 0, 0)),
        ],
        out_specs=[pl.BlockSpec((te, d), erow), pl.BlockSpec((te, d), erow)],
        out_shape=[jax.ShapeDtypeStruct((n_tok, d), F32), jax.ShapeDtypeStruct((n_tok, d), BF16)],
        scratch_shapes=[pltpu.VMEM((tm, d), F32), pltpu.VMEM((2, tm, tk), BF16)],
        compiler_params=_cparams(2),
        name="mixout",
    )(yssd, ysc, yft, tail, tail, tail, w_br_ssd, w_br_sc, w_br_ft, w_out, x, mod4, norm_g)


def _grid_pos_emb(n_tok):
    rows = n_tok // GRID_W
    t = np.arange(rows * GRID_W)
    r = (t // GRID_W).astype(np.float32)[:, None]
    col = (t % GRID_W).astype(np.float32)[:, None]
    nf = D_MODEL // 4
    omega = (1.0 / (np.float32(POS_BASE) ** (np.arange(nf, dtype=np.float32) / np.float32(nf)))).astype(np.float32)
    ro = (r * omega).astype(np.float32).astype(np.float64)
    co = (col * omega).astype(np.float32).astype(np.float64)
    return np.concatenate([np.sin(ro), np.cos(ro), np.sin(co), np.cos(co)], axis=-1).astype(np.float32)


def _group_lanes(p, hpg=HEADS_PER_GROUP):
    depth = p.shape[0]
    t = p.reshape(depth, 2, SSD_GROUPS, hpg).transpose(0, 2, 1, 3).reshape(depth, SSD_GROUPS, 1, 2 * hpg)
    return jnp.pad(t, ((0, 0), (0, 0), (0, 0), (0, LANES - 2 * hpg)))


def kernel(x_prompt, x_sample, state_ssd, c, c_ctx, ada_w, ada_b, norm_g, ffn1_wgu, ffn1_wd, w_in,
           ssd_conv_w, ssd_conv_b, ssd_dt_bias, ssd_a_log, ssd_d, ssd_norm_g, sc_conv_w,
           w_br_ssd, w_br_sc, w_br_ft, w_out, ffn2_wgu, ffn2_wd):
    n_ctx, ctx_len, d = x_prompt.shape
    n_lat, lat_len, _ = x_sample.shape
    depth = ada_w.shape[0]
    rows = _Rows(n_ctx * ctx_len, n_lat, lat_len)

    c8 = jnp.concatenate([c_ctx[None, :], c, jnp.zeros((8 - 1 - n_lat, d), F32)], axis=0)
    mod4 = _modulation(c8, ada_w, ada_b)[:, :1 + n_lat].reshape(depth, 1 + n_lat, N_MOD, d)

    x, u = _embed(x_prompt.reshape(n_ctx * ctx_len, d), x_sample.reshape(n_lat * lat_len, d),
                  jnp.asarray(_grid_pos_emb(lat_len)), mod4, norm_g, rows)

    consts = _ssd_constants()
    w_in_t = jnp.swapaxes(w_in, 1, 2)
    conv_b3 = ssd_conv_b.reshape(depth, 1, SSD_XBC)
    dtb = jnp.pad(ssd_dt_bias.reshape(depth, 1, DT_WIDTH), ((0, 0), (0, 0), (0, LANES - DT_WIDTH)))
    alog_g = _group_lanes(ssd_a_log)
    dsk_g = jnp.repeat(ssd_d, SSD_HEAD_DIM, axis=-1).reshape(depth, SSD_GROUPS, 1, GROUP_WIDTH)
    ssd_norm_g3 = ssd_norm_g.reshape(depth, 1, SSD_INNER)
    state_shape = (n_ctx, depth, 2, SSD_HEADS, SSD_HEAD_DIM, SSD_STATE)

    states = None
    for l in range(depth):
        x, u = _ffn(x, u, mod4, norm_g, ffn1_wgu, ffn1_wd, l, 0, rows, "same")
        proj, dt = _inproj(u, w_in_t, dtb, l)
        tail = proj
        ssd_args = (proj, dt, ssd_conv_w, conv_b3, alog_g, dsk_g, ssd_norm_g3, consts, l)
        yssd, states = _ssd(*ssd_args, row0=0, n_seq=n_ctx, seq_len=ctx_len,
                            prev_state=states, state_shape=state_shape)
        (yssd,) = _ssd(*ssd_args, row0=rows.n_ctx_tok, n_seq=n_lat, seq_len=lat_len,
                       h0=state_ssd, prev_y=yssd)
        ysc, yft = _scft(tail, sc_conv_w, l, row0=0, n_seq=n_ctx, seq_len=ctx_len, n_grp=FT_GROUPS)
        ysc, yft = _scft(tail, sc_conv_w, l, row0=rows.n_ctx_tok, n_seq=n_lat, seq_len=lat_len,
                         n_grp=1, prev=(ysc, yft))
        x, u = _mixout(yssd, ysc, yft, tail, w_br_ssd, w_br_sc, w_br_ft, w_out, x, mod4, norm_g, l, rows)
        if l + 1 < depth:
            x, u = _ffn(x, u, mod4, norm_g, ffn2_wgu, ffn2_wd, l, 1, rows, "layer")
        else:
            y_ctx, y_lat = _ffn(x, u, mod4, norm_g, ffn2_wgu, ffn2_wd, l, 1, rows, "final")

    return (y_ctx.reshape(n_ctx, ctx_len, d), y_lat.reshape(n_lat, lat_len, d), states)
```

```python
import functools
import math

import numpy as np
import jax
import jax.numpy as jnp
from jax import lax
from jax.experimental import pallas as pl
from jax.experimental.pallas import tpu as pltpu

F32 = jnp.float32
BF16 = jnp.bfloat16

D_MODEL = 2048
DEPTH = 4
GRID_W = 64
POS_BASE = 10000.0
NORM_EPS = 1e-6
SSD_HEADS = 32
SSD_HEAD_DIM = 64
SSD_INNER = SSD_HEADS * SSD_HEAD_DIM
SSD_GROUPS = 4
SSD_STATE = 128
SSD_CONV = 4
SSD_CHUNK = 128
SSD_XBC = SSD_INNER + 2 * SSD_GROUPS * SSD_STATE
SC_WIDTH = 1024
SC_CONV = 3
FT_WIDTH = 1024
FT_GROUPS = 4
FT_GROUP_DIM = FT_WIDTH // FT_GROUPS
D_FF = 5504
N_BRANCH = 3
N_MOD = 9
HEADS_PER_GROUP = SSD_HEADS // SSD_GROUPS
GROUP_WIDTH = HEADS_PER_GROUP * SSD_HEAD_DIM
HEAD_COLS = SSD_INNER + SSD_XBC
DT_COL = HEAD_COLS
DT_WIDTH = 2 * SSD_HEADS
TAIL_COL = HEAD_COLS + DT_WIDTH
TAIL_COLS = 3 * SC_WIDTH + FT_WIDTH + N_BRANCH * D_MODEL
IN_COLS = TAIL_COL + TAIL_COLS

LANES = 128
BF16_ROWS = 16
VMEM_LIMIT_BYTES = 60 * 1024 * 1024


def _cparams(n_axes):
    return pltpu.CompilerParams(dimension_semantics=("arbitrary",) * n_axes,
                                vmem_limit_bytes=VMEM_LIMIT_BYTES)


def _silu(x):
    return x * jax.nn.sigmoid(x)


def _rms(x, g):
    ms = jnp.mean(x * x, axis=-1, keepdims=True)
    return x * lax.rsqrt(ms + NORM_EPS) * g


def _norm_mod(x, g, shift, scale):
    return _rms(x, g) * (1.0 + scale) + shift


def _split_bf16(x, n):
    parts = []
    r = x
    for i in range(n):
        p = r.astype(BF16)
        parts.append(p)
        if i + 1 < n:
            r = r - p.astype(F32)
    return parts


def _dot_r01(x, m01, n=3):
    acc = None
    for p in _split_bf16(x, n):
        t = jnp.dot(p, m01, preferred_element_type=F32)
        acc = t if acc is None else acc + t
    return acc


def _dot_l01(m01, x, n=3):
    acc = None
    for p in _split_bf16(x, n):
        t = jnp.dot(m01, p, preferred_element_type=F32)
        acc = t if acc is None else acc + t
    return acc


def _mod_kernel(c_ref, w_ref, b_ref, o_ref):
    s = _silu(c_ref[...]).astype(BF16)
    o_ref[...] = jnp.dot(s, w_ref[...].astype(BF16), preferred_element_type=F32) + b_ref[...]


def _modulation(c8, ada_w, ada_b, tn=1024):
    depth, d, n = ada_w.shape
    return pl.pallas_call(
        _mod_kernel,
        grid=(depth, n // tn),
        in_specs=[
            pl.BlockSpec((8, d), lambda l, j: (0, 0)),
            pl.BlockSpec((None, d, tn), lambda l, j: (l, 0, j)),
            pl.BlockSpec((None, 1, tn), lambda l, j: (l, 0, j)),
        ],
        out_specs=pl.BlockSpec((None, 8, tn), lambda l, j: (l, 0, j)),
        out_shape=jax.ShapeDtypeStruct((depth, 8, n), F32),
        compiler_params=_cparams(2),
        name="adaln_mod",
    )(c8, ada_w, ada_b.reshape(depth, 1, n))


class _Rows:
    def __init__(self, n_ctx_tok, n_lat, lat_len):
        self.n_ctx_tok = n_ctx_tok
        self.n_lat = n_lat
        self.lat_len = lat_len
        self.n_tok = n_ctx_tok + n_lat * lat_len

    def group_of_tile(self, i, tm):
        assert self.n_ctx_tok % tm == 0 and self.lat_len % tm == 0
        n_ctx_tiles = self.n_ctx_tok // tm
        per_lat = self.lat_len // tm
        return jnp.where(i < n_ctx_tiles, 0, 1 + (i - n_ctx_tiles) // per_lat)


def _single(block_shape, index_map):
    return pl.BlockSpec(block_shape, index_map, pipeline_mode=pl.Buffered(1))


FFN_TF = 256
FFN_SPLIT = 1
EPILOGUE_ROWS = 256


def _ffn_kernel(*refs, nk, mod_row, ng_row, next_kind, n_ctx_tiles):
    x_ref, u_ref, mod_ref, ng_ref, wg_ref, wu_ref, wd_ref = refs[:7]
    pos = 7
    if next_kind == "layer":
        modn_ref, ngn_ref = refs[pos:pos + 2]
        pos += 2
    o_ref = refs[pos]
    pos += 1
    if next_kind == "final":
        olat_ref = refs[pos]
    else:
        un_ref = refs[pos]
    pos += 1
    acc_sc, wup_sc, wdn_sc = refs[pos:pos + 3]
    k = pl.program_id(1)
    tf = FFN_TF
    te = o_ref.shape[0]

    @pl.when(k == 0)
    def _():
        acc_sc[...] = jnp.zeros_like(acc_sc)

    @pl.when(k < nk)
    def _():
        ts = tf // FFN_SPLIT
        for s in range(FFN_SPLIT):
            wup_sc[:, 2 * s * ts:(2 * s + 1) * ts] = wg_ref[:, s * ts:(s + 1) * ts].astype(BF16)
            wup_sc[:, (2 * s + 1) * ts:(2 * s + 2) * ts] = wu_ref[:, s * ts:(s + 1) * ts].astype(BF16)
        wdn_sc[...] = wd_ref[...].astype(BF16)

        overlap = nk * tf - D_FF
        assert overlap <= ts
        for s in range(FFN_SPLIT):
            h = jnp.dot(u_ref[...], wup_sc[:, 2 * s * ts:(2 * s + 2) * ts], preferred_element_type=F32)
            a = _silu(h[:, :ts]) * h[:, ts:]
            if s == 0:
                lane = lax.broadcasted_iota(jnp.int32, a.shape, 1)
                a = jnp.where(jnp.logical_and(k == nk - 1, lane < overlap), 0.0, a)
            acc_sc[...] += jnp.dot(a.astype(BF16), wdn_sc[s * ts:(s + 1) * ts, :], preferred_element_type=F32)

    @pl.when(k >= nk)
    def _():
        r0 = pl.multiple_of((k - nk) * te, te)
        y = _rms(acc_sc[pl.ds(r0, te), :], ng_ref[ng_row + 1:ng_row + 2, :])
        x_new = x_ref[...] + 0.5 * mod_ref[mod_row + 2:mod_row + 3, :] * y
        if next_kind == "final":
            is_ctx = pl.program_id(0) < n_ctx_tiles

            @pl.when(is_ctx)
            def _():
                o_ref[...] = x_new

            @pl.when(jnp.logical_not(is_ctx))
            def _():
                olat_ref[...] = x_new
        else:
            o_ref[...] = x_new
        if next_kind == "same":
            un = _norm_mod(x_new, ng_ref[ng_row + 2:ng_row + 3, :],
                           mod_ref[mod_row + 3:mod_row + 4, :], mod_ref[mod_row + 4:mod_row + 5, :])
            un_ref[...] = un.astype(BF16)
        elif next_kind == "layer":
            un = _norm_mod(x_new, ngn_ref[0:1, :], modn_ref[0:1, :], modn_ref[1:2, :])
            un_ref[...] = un.astype(BF16)


def _ffn(x, u, mod4, norm_g, wgu, wd, l, which, rows, next_kind, tm=2048):
    n_tok, d = x.shape
    tf = FFN_TF
    nk = pl.cdiv(D_FF, tf)
    mod_row = 0 if which == 0 else 6
    ng_row = 0 if which == 0 else 4
    te = EPILOGUE_ROWS
    ne = tm // te
    assert rows.n_ctx_tok % tm == 0 and n_tok % tm == 0
    gran = math.gcd(rows.lat_len, tm)
    grp = lambda i, k: rows.group_of_tile((tm * i + jnp.clip(k - nk, 0, ne - 1) * te) // gran, gran)
    assert tf % LANES == 0 and D_FF % LANES == 0
    col = lambda k, base=0: LANES * (base // LANES + jnp.minimum(k * (tf // LANES), (D_FF - tf) // LANES))
    assert next_kind in ("same", "layer", "final")
    kern = functools.partial(_ffn_kernel, nk=nk, mod_row=mod_row, ng_row=ng_row, next_kind=next_kind,
                             n_ctx_tiles=rows.n_ctx_tok // tm)
    erow = lambda i, k: (ne * i + jnp.clip(k - nk, 0, ne - 1), 0)
    in_specs = [
        pl.BlockSpec((te, d), erow),
        _single((tm, d), lambda i, k: (i, 0)),
        pl.BlockSpec((None, None, N_MOD, d), lambda i, k: (l, grp(i, k), 0, 0)),
        pl.BlockSpec((None, 6, d), lambda i, k: (l, 0, 0)),
        pl.BlockSpec((pl.squeezed, pl.Element(d), pl.Element(tf)), lambda i, k: (l, 0, col(k))),
        pl.BlockSpec((pl.squeezed, pl.Element(d), pl.Element(tf)), lambda i, k: (l, 0, col(k, D_FF))),
        pl.BlockSpec((pl.squeezed, pl.Element(tf), pl.Element(d)), lambda i, k: (l, col(k), 0)),
    ]
    args = [x, u, mod4, norm_g, wgu, wgu, wd]
    if next_kind == "layer":
        in_specs += [pl.BlockSpec((None, None, N_MOD, d), lambda i, k: (l + 1, grp(i, k), 0, 0)),
                     pl.BlockSpec((None, 6, d), lambda i, k: (l + 1, 0, 0))]
        args += [mod4, norm_g]
    if next_kind == "final":
        n_ctx_blocks = rows.n_ctx_tok // te
        eblk = lambda i, k: ne * i + jnp.clip(k - nk, 0, ne - 1)
        out_specs = [pl.BlockSpec((te, d), lambda i, k: (jnp.minimum(eblk(i, k), n_ctx_blocks - 1), 0)),
                     pl.BlockSpec((te, d), lambda i, k: (jnp.maximum(eblk(i, k) - n_ctx_blocks, 0), 0))]
        out_shape = [jax.ShapeDtypeStruct((rows.n_ctx_tok, d), F32),
                     jax.ShapeDtypeStruct((n_tok - rows.n_ctx_tok, d), F32)]
    else:
        out_specs = [pl.BlockSpec((te, d), erow), pl.BlockSpec((te, d), erow)]
        out_shape = [jax.ShapeDtypeStruct((n_tok, d), F32), jax.ShapeDtypeStruct((n_tok, d), BF16)]
    return pl.pallas_call(
        kern,
        grid=(n_tok // tm, nk + ne),
        in_specs=in_specs,
        out_specs=out_specs,
        out_shape=out_shape,
        scratch_shapes=[
            pltpu.VMEM((tm, d), F32),
            pltpu.VMEM((d, 2 * tf), BF16),
            pltpu.VMEM((tf, d), BF16),
        ],
        compiler_params=_cparams(2),
        name="ffn",
    )(*args)


def _embed_kernel(xp_ref, xs_ref, pe_ref, mod_ref, ng_ref, x_ref, u_ref, *, n_ctx_tiles):
    i = pl.program_id(0)

    @pl.when(i < n_ctx_tiles)
    def _():
        x_ref[...] = xp_ref[...]

    @pl.when(i >= n_ctx_tiles)
    def _():
        x_ref[...] = xs_ref[...] + pe_ref[...]

    u_ref[...] = _norm_mod(x_ref[...], ng_ref[0:1, :], mod_ref[0:1, :], mod_ref[1:2, :]).astype(BF16)


def _embed(xp, xs, pe, mod4, norm_g, rows, tm=256):
    d = xp.shape[1]
    assert rows.lat_len % tm == 0 and rows.n_ctx_tok % tm == 0
    n_ctx_tiles = rows.n_ctx_tok // tm
    pe_tiles = rows.lat_len // tm
    grp = lambda i: rows.group_of_tile(i, tm)
    kern = functools.partial(_embed_kernel, n_ctx_tiles=n_ctx_tiles)
    return pl.pallas_call(
        kern,
        grid=(rows.n_tok // tm,),
        in_specs=[
            pl.BlockSpec((tm, d), lambda i: (jnp.minimum(i, n_ctx_tiles - 1), 0)),
            pl.BlockSpec((tm, d), lambda i: (jnp.maximum(i - n_ctx_tiles, 0), 0)),
            pl.BlockSpec((tm, d), lambda i: (jnp.maximum(i - n_ctx_tiles, 0) % pe_tiles, 0)),
            pl.BlockSpec((None, None, N_MOD, d), lambda i: (0, grp(i), 0, 0)),
            pl.BlockSpec((None, 6, d), lambda i: (0, 0, 0)),
        ],
        out_specs=[pl.BlockSpec((tm, d), lambda i: (i, 0)), pl.BlockSpec((tm, d), lambda i: (i, 0))],
        out_shape=[jax.ShapeDtypeStruct((rows.n_tok, d), F32), jax.ShapeDtypeStruct((rows.n_tok, d), BF16)],
        compiler_params=_cparams(1),
        name="embed",
    )(xp, xs, pe, mod4, norm_g)


def _dot_nt(a, b):
    return lax.dot_general(a, b, (((1,), (1,)), ((), ())), preferred_element_type=F32)


def _softplus(x):
    return jnp.maximum(x, 0.0) + jnp.log1p(jnp.exp(-jnp.abs(x)))


def _inproj_kernel(u_ref, wa_ref, wb_ref, dtb_ref, op_ref, odt_ref, w_sc, *, n_head, tm):
    j = pl.program_id(0)
    m = pl.program_id(1)
    tn = w_sc.shape[0]
    off = DT_WIDTH

    @pl.when(jnp.logical_and(m == 0, j < n_head))
    def _():
        w_sc[...] = wa_ref[...].astype(BF16)

    @pl.when(jnp.logical_and(m == 0, j == n_head))
    def _():
        w_sc[:off, :] = wb_ref[...].astype(BF16)

    @pl.when(jnp.logical_and(m == 0, j > n_head))
    def _():
        w_sc[:tn - off, :] = wa_ref[off:, :].astype(BF16)
        w_sc[tn - off:, :] = wb_ref[...].astype(BF16)

    u = u_ref[pl.ds(pl.multiple_of(m * tm, tm), tm), :]

    @pl.when(j != n_head)
    def _():
        op_ref[...] = _dot_nt(u, w_sc[...])

    @pl.when(j == n_head)
    def _():
        odt_ref[...] = _softplus(_dot_nt(u, w_sc[:LANES, :]) + dtb_ref[...])


def _inproj(u, w_in_t, dtb, l, tm=2048, tn=512):
    n_tok, d = u.shape
    off = DT_WIDTH
    assert TAIL_COL % tn == off and HEAD_COLS % tn == 0 and TAIL_COLS % tn == 0 and tn % off == 0
    n_head = HEAD_COLS // tn
    n_tail = TAIL_COLS // tn
    n_m = n_tok // tm
    sub = tn // off
    proj_j = lambda j: jnp.where(j < n_head, j, jnp.where(j == n_head, n_head - 1, j - 1))
    proj_m = lambda j, m: jnp.where(j == n_head, n_m - 1, m)
    dt_m = lambda j, m: jnp.where(j < n_head, 0, jnp.where(j == n_head, m, n_m - 1))
    wide_j = lambda j: jnp.where(j <= n_head, jnp.minimum(j, n_head), j - 1)
    narrow_j = lambda j: jnp.where(j <= n_head, DT_COL // off, j * sub)
    kern = functools.partial(_inproj_kernel, n_head=n_head, tm=tm)
    return pl.pallas_call(
        kern,
        grid=(n_head + 1 + n_tail, n_m),
        in_specs=[
            _single((n_tok, d), lambda j, m: (0, 0)),
            pl.BlockSpec((None, tn, d), lambda j, m: (l, wide_j(j), 0)),
            pl.BlockSpec((None, off, d), lambda j, m: (l, narrow_j(j), 0)),
            pl.BlockSpec((None, 1, LANES), lambda j, m: (l, 0, 0)),
        ],
        out_specs=[
            pl.BlockSpec((tm, tn), lambda j, m: (proj_m(j, m), proj_j(j))),
            pl.BlockSpec((tm, LANES), lambda j, m: (dt_m(j, m), 0)),
        ],
        out_shape=[
            jax.ShapeDtypeStruct((n_tok, HEAD_COLS + TAIL_COLS), F32),
            jax.ShapeDtypeStruct((n_tok, LANES), F32),
        ],
        scratch_shapes=[pltpu.VMEM((tn, d), BF16)],
        compiler_params=_cparams(2),
        name="inproj",
    )(u, w_in_t, w_in_t, dtb)


def _dwconv_rows(x, w, left):
    n_rows = x.shape[0]
    row = lax.broadcasted_iota(jnp.int32, x.shape, 0)
    out = None
    for k in range(w.shape[0]):
        off = k - left
        if off == 0:
            term = x
        else:
            shifted = pltpu.roll(x, (-off) % n_rows, axis=0)
            valid = (row < n_rows - off) if off > 0 else (row >= -off)
            term = jnp.where(valid, shifted, 0.0)
        term = term * w[k:k + 1, :]
        out = term if out is None else out + term
    return out


SSD_UNROLL = 4


def _ssd_kernel(*refs, seq_len, has_h0, n_alias, emit_state):
    (z_ref, xr_ref, br_ref, cr_ref, dt_ref, cwx_ref, cwb_ref, cwc_ref, cbx_ref, cbb_ref, cbc_ref,
     alog_ref, dsk_ref, gn_ref, sel_ref, e2_ref, tri_ref, trit_ref) = refs[:18]
    pos = 18
    h0_ref = None
    if has_h0:
        h0_ref = refs[pos]
        pos += 1
    pos += n_alias
    yn_ref = refs[pos]
    pos += 1
    hfin_ref = None
    if emit_state:
        hfin_ref = refs[pos]
        pos += 1
    xa_sc, ba_sc, ca_sc, dts_sc, a_sc, ht_sc, yg_sc, y_sc, ssq_sc = refs[pos:pos + 9]

    gi = pl.program_id(1)
    q = SSD_CHUNK
    nc = seq_len // q
    left = (SSD_CONV - 1) // 2
    hpg = HEADS_PER_GROUP
    gw = GROUP_WIDTH

    xa = _silu(_dwconv_rows(xr_ref[...], cwx_ref[...], left) + cbx_ref[...])
    xa_sc[...] = xa
    ba_sc[...] = _silu(_dwconv_rows(br_ref[...], cwb_ref[...], left) + cbb_ref[...])
    ca_sc[...] = _silu(_dwconv_rows(cr_ref[...], cwc_ref[...], left) + cbc_ref[...])
    yg_sc[...] = xa * dsk_ref[...]

    dts = _dot_r01(dt_ref[...], sel_ref[...])
    dts_sc[...] = dts
    a_sc[...] = dts * (-jnp.exp(alog_ref[...]))

    for d in range(2):
        if has_h0:
            for p in range(hpg // 2):
                blk = h0_ref[d, 2 * p:2 * p + 2].reshape(2 * SSD_HEAD_DIM, SSD_STATE)
                ht_sc[d, :, p * LANES:(p + 1) * LANES] = blk.T
        else:
            ht_sc[d] = jnp.zeros((SSD_STATE, gw), F32)

    ri = lax.broadcasted_iota(jnp.int32, (q, q), 0)
    ci = lax.broadcasted_iota(jnp.int32, (q, q), 1)
    lane_lo = lax.broadcasted_iota(jnp.int32, (q, LANES), 1) < SSD_HEAD_DIM
    neg_inf = jnp.float32(-jnp.inf)

    def chunk_dir(c, d):
        rows = pl.ds(pl.multiple_of(c * q, q), q)
        a_c = a_sc[rows, :]
        if d == 0:
            cs = _dot_l01(tri_ref[...], a_c)
            tot = cs[q - 1:q, :]
            mask = ri >= ci
        else:
            cs = _dot_l01(trit_ref[...], a_c)
            tot = cs[0:1, :]
            mask = ri <= ci
        cst = cs.T
        stack = jnp.concatenate([
            dts_sc[rows, :].astype(BF16),
            jnp.exp(cs).astype(BF16),
            jnp.exp(tot - cs).astype(BF16),
            jnp.broadcast_to(jnp.exp(tot), (BF16_ROWS, LANES)).astype(BF16)], axis=0)
        ex = jnp.dot(stack, e2_ref[:, d * gw:(d + 1) * gw], preferred_element_type=F32)
        dtx, ecs_x, dte_x, ea_x = ex[0:q], ex[q:2 * q], ex[2 * q:3 * q], ex[3 * q:3 * q + 1]
        b_c = ba_sc[rows, :]
        c_bf = ca_sc[rows, :].astype(BF16)
        xdt = xa_sc[rows, :] * dtx
        g = lax.dot_general(c_bf, b_c.astype(BF16), (((1,), (1,)), ((), ())),
                            preferred_element_type=F32)
        ht = ht_sc[d]
        y_off = jnp.dot(c_bf, ht.astype(BF16), preferred_element_type=F32) * ecs_x
        pieces = []
        for p in range(hpg // 2):
            ms = []
            for jj in range(2):
                r = d * hpg + 2 * p + jj
                diff = cs[:, r:r + 1] - cst[r:r + 1, :]
                ms.append((g * jnp.exp(jnp.where(mask, diff, neg_inf))).astype(BF16))
            xp = xdt[:, p * LANES:(p + 1) * LANES]
            rhs = jnp.concatenate([jnp.where(lane_lo, xp, 0.0), jnp.where(lane_lo, 0.0, xp)], axis=0)
            pieces.append(jnp.dot(jnp.concatenate(ms, axis=1), rhs.astype(BF16),
                                  preferred_element_type=F32))
        s_t = jnp.dot(b_c.T.astype(BF16), (xdt * dte_x).astype(BF16), preferred_element_type=F32)
        ht_sc[d] = ht * ea_x + s_t
        yg_sc[rows, :] += jnp.concatenate(pieces, axis=1) + y_off

    def body(c, carry):
        chunk_dir(c, 0)
        chunk_dir(nc - 1 - c, 1)
        return carry

    lax.fori_loop(0, nc, body, 0, unroll=min(nc, SSD_UNROLL))

    yz = yg_sc[...] * _silu(z_ref[...])
    y_sc[gi] = yz
    part = jnp.broadcast_to(jnp.sum(yz * yz, axis=-1, keepdims=True), ssq_sc.shape)

    @pl.when(gi == 0)
    def _():
        ssq_sc[...] = part

    @pl.when(gi > 0)
    def _():
        ssq_sc[...] += part

    @pl.when(gi == SSD_GROUPS - 1)
    def _():
        inv = lax.rsqrt(ssq_sc[:, 0:1] * (1.0 / SSD_INNER) + NORM_EPS)
        for gg in range(SSD_GROUPS):
            cols = slice(gg * gw, (gg + 1) * gw)
            yn_ref[:, cols] = (y_sc[gg] * inv * gn_ref[:, cols]).astype(BF16)

    if emit_state:
        for d in range(2):
            for p in range(hpg // 2):
                blk = ht_sc[d, :, p * LANES:(p + 1) * LANES].T
                hfin_ref[d, 2 * p:2 * p + 2] = blk.reshape(2, SSD_HEAD_DIM, SSD_STATE)


def _ssd_constants():
    hpg = HEADS_PER_GROUP
    sel = np.zeros((SSD_GROUPS, LANES, LANES), np.float32)
    for g in range(SSD_GROUPS):
        for d in range(2):
            for j in range(hpg):
                sel[g, d * SSD_HEADS + hpg * g + j, d * hpg + j] = 1.0
    e2 = np.zeros((LANES, 2 * GROUP_WIDTH), np.float32)
    for d in range(2):
        for j in range(hpg):
            lo = d * GROUP_WIDTH + j * SSD_HEAD_DIM
            e2[d * hpg + j, lo:lo + SSD_HEAD_DIM] = 1.0
    tri = np.tril(np.ones((SSD_CHUNK, SSD_CHUNK), np.float32))
    return (jnp.asarray(sel, BF16), jnp.asarray(e2, BF16), jnp.asarray(tri, BF16),
            jnp.asarray(tri.T, BF16))


def _ssd(head, dt, conv_w, conv_b3, alog_g, dsk_g, norm_g3, consts, l, *, row0, n_seq, seq_len,
         h0=None, prev_y=None, prev_state=None, state_shape=None):
    n_tok = head.shape[0]
    sel, e2, tri, trit = consts
    assert row0 % seq_len == 0 and seq_len % (2 * SSD_CHUNK) == 0
    blk0 = row0 // seq_len
    gw = GROUP_WIDTH
    has_h0 = h0 is not None
    emit_state = state_shape is not None
    xcol = SSD_INNER // gw
    bcol = (SSD_INNER + SSD_INNER) // LANES
    ccol = bcol + SSD_GROUPS
    cwb = SSD_INNER // LANES
    cwc = cwb + SSD_GROUPS
    in_specs = [
        pl.BlockSpec((seq_len, gw), lambda b, g: (blk0 + b, g)),
        pl.BlockSpec((seq_len, gw), lambda b, g: (blk0 + b, xcol + g)),
        pl.BlockSpec((seq_len, LANES), lambda b, g: (blk0 + b, bcol + g)),
        pl.BlockSpec((seq_len, LANES), lambda b, g: (blk0 + b, ccol + g)),
        pl.BlockSpec((seq_len, LANES), lambda b, g: (blk0 + b, 0)),
        pl.BlockSpec((None, SSD_CONV, gw), lambda b, g: (l, 0, g)),
        pl.BlockSpec((None, SSD_CONV, LANES), lambda b, g: (l, 0, cwb + g)),
        pl.BlockSpec((None, SSD_CONV, LANES), lambda b, g: (l, 0, cwc + g)),
        pl.BlockSpec((None, 1, gw), lambda b, g: (l, 0, g)),
        pl.BlockSpec((None, 1, LANES), lambda b, g: (l, 0, cwb + g)),
        pl.BlockSpec((None, 1, LANES), lambda b, g: (l, 0, cwc + g)),
        pl.BlockSpec((None, None, 1, LANES), lambda b, g: (l, g, 0, 0)),
        pl.BlockSpec((None, None, 1, gw), lambda b, g: (l, g, 0, 0)),
        pl.BlockSpec((None, 1, SSD_INNER), lambda b, g: (l, 0, 0)),
        pl.BlockSpec((None, LANES, LANES), lambda b, g: (g, 0, 0)),
        pl.BlockSpec((LANES, 2 * gw), lambda b, g: (0, 0)),
        pl.BlockSpec((SSD_CHUNK, SSD_CHUNK), lambda b, g: (0, 0)),
        pl.BlockSpec((SSD_CHUNK, SSD_CHUNK), lambda b, g: (0, 0)),
    ]
    args = [head, head, head, head, dt, conv_w, conv_w, conv_w, conv_b3, conv_b3, conv_b3,
            alog_g, dsk_g, norm_g3, sel, e2, tri, trit]
    if has_h0:
        in_specs.append(pl.BlockSpec((None, None, 2, HEADS_PER_GROUP, SSD_HEAD_DIM, SSD_STATE),
                                     lambda b, g: (b, l, 0, g, 0, 0)))
        args.append(h0)
    aliases = {}
    if prev_y is not None:
        in_specs.append(pl.BlockSpec(memory_space=pl.ANY))
        aliases[len(args)] = 0
        args.append(prev_y)
    if prev_state is not None:
        assert emit_state
        in_specs.append(pl.BlockSpec(memory_space=pl.ANY))
        aliases[len(args)] = 1
        args.append(prev_state)
    out_specs = [pl.BlockSpec((seq_len, SSD_INNER), lambda b, g: (blk0 + b, 0))]
    out_shape = [jax.ShapeDtypeStruct((n_tok, SSD_INNER), BF16)]
    if emit_state:
        out_specs.append(pl.BlockSpec((None, None, 2, HEADS_PER_GROUP, SSD_HEAD_DIM, SSD_STATE),
                                      lambda b, g: (b, l, 0, g, 0, 0)))
        out_shape.append(jax.ShapeDtypeStruct(state_shape, F32))
    kern = functools.partial(_ssd_kernel, seq_len=seq_len, has_h0=has_h0, n_alias=len(aliases),
                             emit_state=emit_state)
    return pl.pallas_call(
        kern,
        grid=(n_seq, SSD_GROUPS),
        in_specs=in_specs,
        out_specs=out_specs,
        out_shape=out_shape,
        input_output_aliases=aliases,
        scratch_shapes=[
            pltpu.VMEM((seq_len, gw), F32),
            pltpu.VMEM((seq_len, LANES), F32),
            pltpu.VMEM((seq_len, LANES), F32),
            pltpu.VMEM((seq_len, LANES), F32),
            pltpu.VMEM((seq_len, LANES), F32),
            pltpu.VMEM((2, SSD_STATE, gw), F32),
            pltpu.VMEM((seq_len, gw), F32),
            pltpu.VMEM((SSD_GROUPS, seq_len, gw), F32),
            pltpu.VMEM((seq_len, LANES), F32),
        ],
        compiler_params=_cparams(2),
        name="ssd_lat" if has_h0 else "ssd_ctx",
    )(*args)


def _scft_kernel(*refs, n_alias, n_grp):
    b_ref, c_ref, x_ref, f_ref, cw_ref, clsl_ref, ccsc_ref = refs[:7]
    pos = 7 + n_alias
    ysc_ref, yft_ref = refs[pos], refs[pos + 1]
    v = c_ref[...] * x_ref[...]
    ysc_ref[...] = (b_ref[...] * _dwconv_rows(v, cw_ref[...], (SC_CONV - 1) // 2)).astype(BF16)
    tw = FT_GROUP_DIM
    for gg in range(n_grp):
        cols = slice(gg * tw, (gg + 1) * tw)
        u = f_ref[:, cols].astype(BF16)
        ps = jnp.dot(u, ccsc_ref[...], preferred_element_type=F32).astype(BF16)
        stacked = jnp.concatenate([ps[:, :tw], ps[:, tw:]], axis=0)
        yft_ref[:, cols] = jnp.dot(clsl_ref[...], stacked, preferred_element_type=F32).astype(BF16)


def _dft_mats(n):
    k = np.arange(n)
    ang = 2.0 * np.pi * ((k[:, None] * k[None, :]) % n) / n
    scale = 1.0 / math.sqrt(n)
    return jnp.asarray(np.cos(ang) * scale, BF16), jnp.asarray(np.sin(ang) * scale, BF16)


def _scft(tail, sc_conv_w, l, *, row0, n_seq, seq_len, n_grp, prev=None):
    n_tok = tail.shape[0]
    tw = n_grp * FT_GROUP_DIM
    assert row0 % seq_len == 0 and SC_WIDTH % tw == 0 and HEAD_COLS % tw == 0
    blk0 = row0 // seq_len
    nt = SC_WIDTH // tw
    c0 = HEAD_COLS // tw
    cl, sl = _dft_mats(seq_len)
    cc, sc = _dft_mats(FT_GROUP_DIM)
    clsl = jnp.concatenate([cl, -sl], axis=1)
    ccsc = jnp.concatenate([cc, sc], axis=1)
    in_specs = [
        pl.BlockSpec((seq_len, tw), lambda b, j: (blk0 + b, c0 + j)),
        pl.BlockSpec((seq_len, tw), lambda b, j: (blk0 + b, c0 + nt + j)),
        pl.BlockSpec((seq_len, tw), lambda b, j: (blk0 + b, c0 + 2 * nt + j)),
        pl.BlockSpec((seq_len, tw), lambda b, j: (blk0 + b, c0 + 3 * nt + j)),
        pl.BlockSpec((None, SC_CONV, tw), lambda b, j: (l, 0, j)),
        pl.BlockSpec((seq_len, 2 * seq_len), lambda b, j: (0, 0)),
        pl.BlockSpec((FT_GROUP_DIM, 2 * FT_GROUP_DIM), lambda b, j: (0, 0)),
    ]
    args = [tail, tail, tail, tail, sc_conv_w, clsl, ccsc]
    aliases = {}
    if prev is not None:
        in_specs += [pl.BlockSpec(memory_space=pl.ANY), pl.BlockSpec(memory_space=pl.ANY)]
        aliases = {len(args): 0, len(args) + 1: 1}
        args += list(prev)
    kern = functools.partial(_scft_kernel, n_alias=len(aliases), n_grp=n_grp)
    return pl.pallas_call(
        kern,
        grid=(n_seq, nt),
        in_specs=in_specs,
        out_specs=[
            pl.BlockSpec((seq_len, tw), lambda b, j: (blk0 + b, j)),
            pl.BlockSpec((seq_len, tw), lambda b, j: (blk0 + b, j)),
        ],
        out_shape=[
            jax.ShapeDtypeStruct((n_tok, SC_WIDTH), BF16),
            jax.ShapeDtypeStruct((n_tok, FT_WIDTH), BF16),
        ],
        input_output_aliases=aliases,
        compiler_params=_cparams(2),
        name="scft_lat" if prev is not None else "scft_ctx",
    )(*args)


def _mixout_kernel(a0_ref, a1_ref, a2_ref, g0_ref, g1_ref, g2_ref, w0_ref, w1_ref, w2_ref, wo_ref,
                   x_ref, mod_ref, ng_ref, o_ref, un_ref, acc_sc, m_sc, *, nk):
    k = pl.program_id(1)
    te = o_ref.shape[0]

    def merge():
        m = jax.nn.sigmoid(g0_ref[...]) * jnp.dot(a0_ref[...], w0_ref[...].astype(BF16),
                                                   preferred_element_type=F32)
        m += jax.nn.sigmoid(g1_ref[...]) * jnp.dot(a1_ref[...], w1_ref[...].astype(BF16),
                                                    preferred_element_type=F32)
        m += jax.nn.sigmoid(g2_ref[...]) * jnp.dot(a2_ref[...], w2_ref[...].astype(BF16),
                                                    preferred_element_type=F32)
        m_sc[k % 2] = m.astype(BF16)

    def project():
        return jnp.dot(m_sc[(k + 1) % 2], wo_ref[...].astype(BF16), preferred_element_type=F32)

    @pl.when(k == 0)
    def _():
        merge()

    @pl.when(k == 1)
    def _():
        acc_sc[...] = project()
        merge()

    @pl.when(jnp.logical_and(k > 1, k < nk))
    def _():
        acc_sc[...] += project()
        merge()

    @pl.when(k == nk)
    def _():
        acc_sc[...] += project()

    @pl.when(k >= nk)
    def _():
        r0 = pl.multiple_of((k - nk) * te, te)
        y = _rms(acc_sc[pl.ds(r0, te), :], ng_ref[3:4, :])
        x_new = x_ref[...] + mod_ref[5:6, :] * y
        o_ref[...] = x_new
        un_ref[...] = _norm_mod(x_new, ng_ref[4:5, :], mod_ref[6:7, :], mod_ref[7:8, :]).astype(BF16)


def _mixout(yssd, ysc, yft, tail, w_br_ssd, w_br_sc, w_br_ft, w_out, x, mod4, norm_g, l, rows,
            tm=1024, tk=256):
    n_tok, d = x.shape
    nk = d // tk
    gate0 = (HEAD_COLS + 3 * SC_WIDTH + FT_WIDTH) // tk
    per = d // tk
    grp = lambda i: rows.group_of_tile(i, tm)
    kk = lambda k: jnp.minimum(k, nk - 1)
    te = EPILOGUE_ROWS
    ne = tm // te
    erow = lambda i, k: (ne * i + jnp.clip(k - nk, 0, ne - 1), 0)
    kern = functools.partial(_mixout_kernel, nk=nk)
    return pl.pallas_call(
        kern,
        grid=(n_tok // tm, nk + ne),
        in_specs=[
            pl.BlockSpec((tm, SSD_INNER), lambda i, k: (i, 0)),
            pl.BlockSpec((tm, SC_WIDTH), lambda i, k: (i, 0)),
            pl.BlockSpec((tm, FT_WIDTH), lambda i, k: (i, 0)),
            pl.BlockSpec((tm, tk), lambda i, k: (i, gate0 + kk(k))),
            pl.BlockSpec((tm, tk), lambda i, k: (i, gate0 + per + kk(k))),
            pl.BlockSpec((tm, tk), lambda i, k: (i, gate0 + 2 * per + kk(k))),
            pl.BlockSpec((None, SSD_INNER, tk), lambda i, k: (l, 0, kk(k))),
            pl.BlockSpec((None, SC_WIDTH, tk), lambda i, k: (l, 0, kk(k))),
            pl.BlockSpec((None, FT_WIDTH, tk), lambda i, k: (l, 0, kk(k))),
            pl.BlockSpec((None, tk, d), lambda i, k: (l, jnp.clip(k - 1, 0, nk - 1), 0)),
            pl.BlockSpec((te, d), erow),
            pl.BlockSpec((None, None, N_MOD, d), lambda i, k: (l, grp(i), 0, 0)),
            pl.BlockSpec((None, 6, d), lambda i, k: (l, 0, 0)),
        ],
        out_specs=[pl.BlockSpec((te, d), erow), pl.BlockSpec((te, d), erow)],
        out_shape=[jax.ShapeDtypeStruct((n_tok, d), F32), jax.ShapeDtypeStruct((n_tok, d), BF16)],
        scratch_shapes=[pltpu.VMEM((tm, d), F32), pltpu.VMEM((2, tm, tk), BF16)],
        compiler_params=_cparams(2),
        name="mixout",
    )(yssd, ysc, yft, tail, tail, tail, w_br_ssd, w_br_sc, w_br_ft, w_out, x, mod4, norm_g)


def _grid_pos_emb(n_tok):
    rows = n_tok // GRID_W
    t = np.arange(rows * GRID_W)
    r = (t // GRID_W).astype(np.float32)[:, None]
    col = (t % GRID_W).astype(np.float32)[:, None]
    nf = D_MODEL // 4
    omega = (1.0 / (np.float32(POS_BASE) ** (np.arange(nf, dtype=np.float32) / np.float32(nf)))).astype(np.float32)
    ro = (r * omega).astype(np.float32).astype(np.float64)
    co = (col * omega).astype(np.float32).astype(np.float64)
    return np.concatenate([np.sin(ro), np.cos(ro), np.sin(co), np.cos(co)], axis=-1).astype(np.float32)


def _group_lanes(p, hpg=HEADS_PER_GROUP):
    depth = p.shape[0]
    t = p.reshape(depth, 2, SSD_GROUPS, hpg).transpose(0, 2, 1, 3).reshape(depth, SSD_GROUPS, 1, 2 * hpg)
    return jnp.pad(t, ((0, 0), (0, 0), (0, 0), (0, LANES - 2 * hpg)))


def kernel(x_prompt, x_sample, state_ssd, c, c_ctx, ada_w, ada_b, norm_g, ffn1_wgu, ffn1_wd, w_in,
           ssd_conv_w, ssd_conv_b, ssd_dt_bias, ssd_a_log, ssd_d, ssd_norm_g, sc_conv_w,
           w_br_ssd, w_br_sc, w_br_ft, w_out, ffn2_wgu, ffn2_wd):
    n_ctx, ctx_len, d = x_prompt.shape
    n_lat, lat_len, _ = x_sample.shape
    depth = ada_w.shape[0]
    rows = _Rows(n_ctx * ctx_len, n_lat, lat_len)

    c8 = jnp.concatenate([c_ctx[None, :], c, jnp.zeros((8 - 1 - n_lat, d), F32)], axis=0)
    mod4 = _modulation(c8, ada_w, ada_b)[:, :1 + n_lat].reshape(depth, 1 + n_lat, N_MOD, d)

    x, u = _embed(x_prompt.reshape(n_ctx * ctx_len, d), x_sample.reshape(n_lat * lat_len, d),
                  jnp.asarray(_grid_pos_emb(lat_len)), mod4, norm_g, rows)

    consts = _ssd_constants()
    w_in_t = jnp.swapaxes(w_in, 1, 2)
    conv_b3 = ssd_conv_b.reshape(depth, 1, SSD_XBC)
    dtb = jnp.pad(ssd_dt_bias.reshape(depth, 1, DT_WIDTH), ((0, 0), (0, 0), (0, LANES - DT_WIDTH)))
    alog_g = _group_lanes(ssd_a_log)
    dsk_g = jnp.repeat(ssd_d, SSD_HEAD_DIM, axis=-1).reshape(depth, SSD_GROUPS, 1, GROUP_WIDTH)
    ssd_norm_g3 = ssd_norm_g.reshape(depth, 1, SSD_INNER)
    state_shape = (n_ctx, depth, 2, SSD_HEADS, SSD_HEAD_DIM, SSD_STATE)

    states = None
    for l in range(depth):
        x, u = _ffn(x, u, mod4, norm_g, ffn1_wgu, ffn1_wd, l, 0, rows, "same")
        proj, dt = _inproj(u, w_in_t, dtb, l)
        tail = proj
        ssd_args = (proj, dt, ssd_conv_w, conv_b3, alog_g, dsk_g, ssd_norm_g3, consts, l)
        yssd, states = _ssd(*ssd_args, row0=0, n_seq=n_ctx, seq_len=ctx_len,
                            prev_state=states, state_shape=state_shape)
        (yssd,) = _ssd(*ssd_args, row0=rows.n_ctx_tok, n_seq=n_lat, seq_len=lat_len,
                       h0=state_ssd, prev_y=yssd)
        ysc, yft = _scft(tail, sc_conv_w, l, row0=0, n_seq=n_ctx, seq_len=ctx_len, n_grp=FT_GROUPS)
        ysc, yft = _scft(tail, sc_conv_w, l, row0=rows.n_ctx_tok, n_seq=n_lat, seq_len=lat_len,
                         n_grp=1, prev=(ysc, yft))
        x, u = _mixout(yssd, ysc, yft, tail, w_br_ssd, w_br_sc, w_br_ft, w_out, x, mod4, norm_g, l, rows)
        if l + 1 < depth:
            x, u = _ffn(x, u, mod4, norm_g, ffn2_wgu, ffn2_wd, l, 1, rows, "layer")
        else:
            y_ctx, y_lat = _ffn(x, u, mod4, norm_g, ffn2_wgu, ffn2_wd, l, 1, rows, "final")

    return (y_ctx.reshape(n_ctx, ctx_len, d), y_lat.reshape(n_lat, lat_len, d), states)
```
